```python
import jax, jax.numpy as jnp
from jax import lax
import numpy as np

D_MODEL = 2048
BATCH = 2
SEQ = 4096
DEPTH = 1

GLA_HEADS = 4
GLA_DK = D_MODEL // (2 * GLA_HEADS)
GLA_DV = D_MODEL // GLA_HEADS
GLA_RANK = 16
GLA_TAU = 16.0
GLA_CHUNK = 64
FOX_HEADS = 8
FOX_DH = 128
FOX_BLOCK = 128
D_FF = 5632
CONV_W = 3
PLE_DIM = 256
LN_EPS = 1e-5
RMS_EPS = 1e-6
ALPHA = (2 * DEPTH) ** 0.25
BETA = (8 * DEPTH) ** -0.25

GLA_QK = GLA_HEADS * GLA_DK
GLA_V = GLA_HEADS * GLA_DV
FOX_W = FOX_HEADS * FOX_DH
SPLITS = (GLA_QK, GLA_QK, GLA_V, GLA_V, GLA_RANK,
          FOX_W, FOX_W, FOX_W, FOX_HEADS,
          D_MODEL, D_MODEL)
N_IN = sum(SPLITS)

kernel_name = "hybrid_gla_fox_gated_merge_deepnorm"


def layer_norm(x, g, b):
    xf = x.astype(jnp.float32)
    mu = jnp.mean(xf, -1, keepdims=True)
    var = jnp.mean(jnp.square(xf - mu), -1, keepdims=True)
    return ((xf - mu) * lax.rsqrt(var + LN_EPS) * g.astype(jnp.float32)
            + b.astype(jnp.float32)).astype(x.dtype)


def gla_chunked(q, k, v, log_a):
    B, S, H, DK = q.shape
    DV = v.shape[-1]
    n = S // GLA_CHUNK

    def to_chunks(t):
        return t.reshape(B, n, GLA_CHUNK, H, t.shape[-1]).transpose(1, 0, 3, 2, 4)

    qc, kc, vc, gc = (to_chunks(t) for t in (q, k, v, log_a))
    causal = jnp.tril(jnp.ones((GLA_CHUNK, GLA_CHUNK), bool))[:, :, None]

    def step(state, inp):
        qb, kb, vb, gb = inp
        bcum = jnp.cumsum(gb, axis=-2)
        inter = jnp.einsum('bhck,bhkv->bhcv', qb * jnp.exp(bcum), state)
        diff = bcum[..., :, None, :] - bcum[..., None, :, :]
        decay = jnp.exp(jnp.where(causal, diff, -jnp.inf))
        attn = jnp.einsum('bhtk,bhsk,bhtsk->bhts', qb, kb, decay)
        intra = jnp.einsum('bhts,bhsv->bhtv', attn, vb)
        blast = bcum[..., -1:, :]
        state = (jnp.exp(blast[..., 0, :])[..., None] * state
                 + jnp.einsum('bhsk,bhsv->bhkv', kb * jnp.exp(blast - bcum), vb))
        return state, inter + intra

    s0 = jnp.zeros((B, H, DK, DV), jnp.float32)
    _, o = lax.scan(step, s0, (qc, kc, vc, gc))
    return o.transpose(1, 0, 3, 2, 4).reshape(B, S, H, DV)


def fox_attention(q, k, v, log_f):
    B, S, H, Dh = q.shape
    nb = S // FOX_BLOCK
    c = jnp.cumsum(log_f, axis=1).transpose(0, 2, 1)
    qh = q.transpose(0, 2, 1, 3) * (Dh ** -0.5)
    kh = k.transpose(0, 2, 1, 3)
    vh = v.transpose(0, 2, 1, 3)
    qb = qh.reshape(B, H, nb, FOX_BLOCK, Dh).transpose(2, 0, 1, 3, 4)
    cb = c.reshape(B, H, nb, FOX_BLOCK).transpose(2, 0, 1, 3)
    pos_k = jnp.arange(S)

    def block(args):
        i, q_i, c_i = args
        pos_q = i * FOX_BLOCK + jnp.arange(FOX_BLOCK)
        s = (jnp.einsum('bhqd,bhkd->bhqk', q_i, kh)
             + c_i[..., :, None] - c[..., None, :])
        s = jnp.where(pos_k[None, :] <= pos_q[:, None], s, -jnp.inf)
        pr = jax.nn.softmax(s, axis=-1)
        return jnp.einsum('bhqk,bhkd->bhqd', pr, vh)

    o = lax.map(block, (jnp.arange(nb), qb, cb))
    return o.transpose(1, 0, 3, 2, 4).reshape(B, S, H, Dh)


def token_mixers(x, w_in, w_gla_lr, b_gla_lr, gla_norm_g, b_forget,
                 w_branch_gla, w_branch_fox, w_out):
    B, S, _ = x.shape
    f32 = jnp.float32
    proj = x @ w_in
    offsets = np.cumsum(SPLITS)[:-1].tolist()
    gq, gk, gv, gr, glr, fq, fk, fv, ff, ma, mb = jnp.split(proj, offsets, axis=-1)

    q = gq.reshape(B, S, GLA_HEADS, GLA_DK).astype(f32) * (GLA_DK ** -0.5)
    k = gk.reshape(B, S, GLA_HEADS, GLA_DK).astype(f32)
    v = gv.reshape(B, S, GLA_HEADS, GLA_DV).astype(f32)
    log_a = (jax.nn.log_sigmoid((glr @ w_gla_lr + b_gla_lr).astype(f32))
             / GLA_TAU).reshape(B, S, GLA_HEADS, GLA_DK)
    o = gla_chunked(q, k, v, log_a)
    o = o * lax.rsqrt(jnp.mean(o * o, -1, keepdims=True) + RMS_EPS) * gla_norm_g.astype(f32)
    o = (o.reshape(B, S, GLA_V) * jax.nn.silu(gr.astype(f32))).astype(x.dtype)
    y_gla = o @ w_branch_gla

    fqh = fq.reshape(B, S, FOX_HEADS, FOX_DH).astype(f32)
    fkh = fk.reshape(B, S, FOX_HEADS, FOX_DH).astype(f32)
    fvh = fv.reshape(B, S, FOX_HEADS, FOX_DH).astype(f32)
    log_f = jax.nn.log_sigmoid((ff + b_forget).astype(f32))
    of = fox_attention(fqh, fkh, fvh, log_f).reshape(B, S, FOX_W).astype(x.dtype)
    y_fox = of @ w_branch_fox

    merged = jax.nn.sigmoid(ma) * y_gla + jax.nn.sigmoid(mb) * y_fox
    return merged @ w_out


def causal_dwconv(h, w, b):
    S = h.shape[1]
    hp = jnp.pad(h, ((0, 0), (CONV_W - 1, 0), (0, 0)))
    y = b
    for j in range(CONV_W):
        y = y + hp[:, j:j + S, :] * w[j]
    return y


def conv_gated_mlp(x, w_gate, w_up, conv_w, conv_b, w_down):
    g = causal_dwconv(x @ w_gate, conv_w, conv_b)
    h = jax.nn.gelu(g) * (x @ w_up)
    return h @ w_down


def setup_inputs(seed: int = 0) -> dict:
    key = jax.random.key(seed)
    ks = jax.random.split(key, 24)
    f32 = jnp.float32

    def nrm(k, shape, scale):
        return jax.random.normal(k, shape, f32) * scale

    L = DEPTH
    return {
        "x": nrm(ks[0], (BATCH, SEQ, D_MODEL), 1.0),
        "p": nrm(ks[1], (DEPTH, BATCH, SEQ, PLE_DIM), 1.0),
        "w_in": nrm(ks[2], (L, D_MODEL, N_IN), D_MODEL ** -0.5),
        "w_gla_lr": nrm(ks[3], (L, GLA_RANK, GLA_QK), GLA_RANK ** -0.5),
        "b_gla_lr": nrm(ks[4], (L, GLA_QK), 0.1),
        "gla_norm_g": 1.0 + nrm(ks[5], (L, GLA_DV), 0.02),
        "b_forget": jax.random.uniform(ks[6], (L, FOX_HEADS), f32, 1.0, 5.0),
        "w_branch_gla": nrm(ks[7], (L, GLA_V, D_MODEL), GLA_V ** -0.5),
        "w_branch_fox": nrm(ks[8], (L, FOX_W, D_MODEL), FOX_W ** -0.5),
        "w_out": nrm(ks[9], (L, D_MODEL, D_MODEL), BETA * D_MODEL ** -0.5),
        "ln1_g": 1.0 + nrm(ks[10], (L, D_MODEL), 0.02),
        "ln1_b": nrm(ks[11], (L, D_MODEL), 0.02),
        "w_gate": nrm(ks[12], (L, D_MODEL, D_FF), D_MODEL ** -0.5),
        "w_up": nrm(ks[13], (L, D_MODEL, D_FF), D_MODEL ** -0.5),
        "conv_w": nrm(ks[14], (L, CONV_W, D_FF), CONV_W ** -0.5),
        "conv_b": nrm(ks[15], (L, D_FF), 0.02),
        "w_down": nrm(ks[16], (L, D_FF, D_MODEL), BETA * D_FF ** -0.5),
        "ln2_g": 1.0 + nrm(ks[17], (L, D_MODEL), 0.02),
        "ln2_b": nrm(ks[18], (L, D_MODEL), 0.02),
        "w_ple_gate": nrm(ks[19], (L, D_MODEL, D_MODEL), D_MODEL ** -0.5),
        "w_ple_proj": nrm(ks[20], (L, PLE_DIM, D_MODEL), BETA * PLE_DIM ** -0.5),
    }


def reference(x, p, w_in, w_gla_lr, b_gla_lr, gla_norm_g, b_forget,
              w_branch_gla, w_branch_fox, w_out, ln1_g, ln1_b,
              w_gate, w_up, conv_w, conv_b, w_down, ln2_g, ln2_b,
              w_ple_gate, w_ple_proj):
    for i in range(DEPTH):
        mix = token_mixers(x, w_in[i], w_gla_lr[i], b_gla_lr[i], gla_norm_g[i],
                           b_forget[i], w_branch_gla[i], w_branch_fox[i], w_out[i])
        x = layer_norm(ALPHA * x + mix, ln1_g[i], ln1_b[i])
        ffn = conv_gated_mlp(x, w_gate[i], w_up[i], conv_w[i], conv_b[i], w_down[i])
        x = layer_norm(ALPHA * x + ffn, ln2_g[i], ln2_b[i])
        x = x + jax.nn.sigmoid(x @ w_ple_gate[i]) * (p[i] @ w_ple_proj[i])
    return x
```

```python
import functools

import jax
import jax.numpy as jnp
from jax import lax
from jax.experimental import pallas as pl
from jax.experimental.pallas import tpu as pltpu

F32 = jnp.float32
BF16 = jnp.bfloat16

GLA_HEADS = 4
GLA_TAU = 16.0
FOX_HEADS = 8
CONV_W = 3
LN_EPS = 1e-5
RMS_EPS = 1e-6

LANES = 128
SUBLANES = 8
VMEM_BYTES_V7X = 64 * 1024 * 1024

GLA_CHUNK = 128
FOX_TILE = 512


def _vmem_limit(block_bytes, extra_bytes=0):
    est = 2 * block_bytes + extra_bytes + (4 << 20)
    return int(min(max(est, 16 << 20), VMEM_BYTES_V7X - (8 << 20)))


def _nbytes(shape, dtype):
    n = 1
    for s in shape:
        n *= s
    return n * jnp.dtype(dtype).itemsize


def _log_sigmoid(z):
    return jnp.minimum(z, 0.0) - jnp.log1p(jnp.exp(-jnp.abs(z)))


def _layer_norm(y, g, b):
    mu = jnp.mean(y, axis=-1, keepdims=True)
    d = y - mu
    var = jnp.mean(d * d, axis=-1, keepdims=True)
    return d * lax.rsqrt(var + LN_EPS) * g + b


def _mm_body(a_ref, w_ref, o_ref):
    o_ref[...] = jnp.dot(a_ref[...], w_ref[...],
                         preferred_element_type=F32).astype(o_ref.dtype)


def _matmul(a, w, out_dtype, tm, tn, name):
    M, K = a.shape
    N = w.shape[1]
    tm = min(tm, M)
    tn = min(tn, N)
    assert M % tm == 0 and N % tn == 0
    blocks = (_nbytes((tm, K), a.dtype) + _nbytes((K, tn), w.dtype)
              + _nbytes((tm, tn), out_dtype))
    return pl.pallas_call(
        _mm_body,
        grid=(N // tn, M // tm),
        in_specs=[pl.BlockSpec((tm, K), lambda j, i: (i, 0)),
                  pl.BlockSpec((K, tn), lambda j, i: (0, j))],
        out_specs=pl.BlockSpec((tm, tn), lambda j, i: (i, j)),
        out_shape=jax.ShapeDtypeStruct((M, N), out_dtype),
        compiler_params=pltpu.CompilerParams(
            dimension_semantics=("parallel", "parallel"),
            vmem_limit_bytes=_vmem_limit(blocks, _nbytes((tm, tn), F32))),
        name=name,
    )(a, w)


def _fcum_body(ff_ref, bias_ref, c_ref, *, groups):
    z = ff_ref[...] + bias_ref[...]
    x = _log_sigmoid(z)
    rows, lanes = x.shape
    lane = lax.broadcasted_iota(jnp.int32, x.shape, 1)
    s = 1
    while s < lanes:
        x = x + jnp.where(lane >= s, pltpu.roll(x, s, axis=1), 0.0)
        s *= 2
    tot = jnp.broadcast_to(x[:, lanes - 1:lanes], x.shape)
    grp = lax.broadcasted_iota(jnp.int32, x.shape, 0) % groups
    inc = tot
    s = 1
    while s < groups:
        inc = inc + jnp.where(grp >= s, pltpu.roll(inc, s, axis=0), 0.0)
        s *= 2
    c_ref[...] = x + (inc - tot)


def _forget_cumsum(ff, b_forget, B, S):
    H = ff.shape[1]
    groups = S // LANES
    ff_t = ff.reshape(B, S, H).transpose(0, 2, 1).reshape(B * H * groups, LANES)
    bias = jnp.broadcast_to(b_forget.reshape(1, H, 1, 1),
                            (B, H, groups, LANES)).reshape(B * H * groups, LANES)
    c = pl.pallas_call(
        functools.partial(_fcum_body, groups=groups),
        out_shape=jax.ShapeDtypeStruct(ff_t.shape, F32),
        name="fox_forget_cumsum",
    )(ff_t, bias)
    return c.reshape(B, H, S)


def _gla_body(q_ref, k_ref, v_ref, r_ref, lr_ref, wlr_ref, blr_ref, g_ref,
              o_ref, st_ref, *, C, DK, DV):
    c = pl.program_id(2)

    @pl.when(c == 0)
    def _():
        st_ref[...] = jnp.zeros_like(st_ref)

    q = q_ref[...] * (DK ** -0.5)
    k = k_ref[...]
    v = v_ref[...]
    z = jnp.dot(lr_ref[...].astype(BF16), wlr_ref[...],
                preferred_element_type=F32) + blr_ref[...]
    la = _log_sigmoid(z) * (1.0 / GLA_TAU)

    row = lax.broadcasted_iota(jnp.int32, (C, DK), 0)
    b = la
    s = 1
    while s < C:
        b = b + jnp.where(row >= s, pltpu.roll(b, s, axis=0), 0.0)
        s *= 2

    st = st_ref[...]
    qe = (q * jnp.exp(b)).astype(BF16)
    inter = lax.dot_general(qe, st.astype(BF16), (((1,), (1,)), ((), ())),
                            preferred_element_type=F32)
    blast = b[C - 1:C, :]
    kd = (k * jnp.exp(blast - b)).astype(BF16)

    ti = lax.broadcasted_iota(jnp.int32, (C, C), 0)
    si = lax.broadcasted_iota(jnp.int32, (C, C), 1)
    attn = None
    m = C // 2
    while m >= SUBLANES:
        blk = 2 * m
        nblk = C // blk
        r = jnp.broadcast_to(b.reshape(nblk, blk, DK)[:, m - 1:m, :],
                             (nblk, blk, DK)).reshape(C, DK)
        e = jnp.exp(-jnp.abs(b - r))
        in_b = (row & (blk - 1)) >= m
        ql = jnp.where(in_b, q * e, 0.0).astype(BF16)
        kl = jnp.where(in_b, 0.0, k * e).astype(BF16)
        a = lax.dot_general(ql, kl, (((1,), (1,)), ((), ())),
                            preferred_element_type=F32)
        if nblk > 1:
            a = jnp.where((ti & -blk) == (si & -blk), a, 0.0)
        attn = a if attn is None else attn + a
        m //= 2

    NB = C // SUBLANES
    q3 = q.reshape(NB, SUBLANES, DK)
    k3 = k.reshape(NB, SUBLANES, DK)
    b3 = b.reshape(NB, SUBLANES, DK)
    sub = lax.broadcasted_iota(jnp.int32, (NB, SUBLANES, C), 1)
    rel = (lax.broadcasted_iota(jnp.int32, (NB, SUBLANES, C), 2)
           - SUBLANES * lax.broadcasted_iota(jnp.int32, (NB, SUBLANES, C), 0))
    diag = jnp.zeros((NB, SUBLANES, C), F32)
    for j in range(SUBLANES):
        bj = b3[:, j:j + 1, :]
        kj = k3[:, j:j + 1, :]
        w = q3 * jnp.exp(jnp.minimum(b3 - bj, 0.0)) * kj
        rs = jnp.sum(w, axis=-1, keepdims=True)
        diag = jnp.where((rel == j) & (sub >= j), rs, diag)
    attn = attn + diag.reshape(C, C)

    o = inter + jnp.dot(attn.astype(BF16), v, preferred_element_type=F32)

    upd = lax.dot_general(v, kd, (((0,), (0,)), ((), ())),
                          preferred_element_type=F32)
    st_ref[...] = st * jnp.exp(blast) + upd

    ms = jnp.mean(o * o, axis=-1, keepdims=True)
    on = o * lax.rsqrt(ms + RMS_EPS) * g_ref[...]
    gate = r_ref[...]
    o_ref[...] = (on * (gate * jax.nn.sigmoid(gate))).astype(o_ref.dtype)


def _gla(proj_f, proj_b, proj_s, w_lr, b_lr, norm_g, B, S, cols):
    H = GLA_HEADS
    DK = w_lr.shape[1] // H
    DV = norm_g.shape[1]
    C = min(GLA_CHUNK, S)
    NC = S // C
    assert S % C == 0 and C % (2 * SUBLANES) == 0
    qo, ko, ro, vo = (cols["gq"] // DK, cols["gk"] // DK, cols["gr"] // DV, cols["gv"] // DV)
    rows = lambda b, h, c: b * NC + c
    blocks = (2 * _nbytes((C, DK), F32) + _nbytes((C, DV), BF16) + _nbytes((C, DV), F32)
              + _nbytes((C, LANES), F32) + _nbytes((LANES, DK), BF16) + _nbytes((C, DV), BF16))
    return pl.pallas_call(
        functools.partial(_gla_body, C=C, DK=DK, DV=DV),
        grid=(B, H, NC),
        in_specs=[
            pl.BlockSpec((C, DK), lambda b, h, c: (rows(b, h, c), qo + h)),
            pl.BlockSpec((C, DK), lambda b, h, c: (rows(b, h, c), ko + h)),
            pl.BlockSpec((C, DV), lambda b, h, c: (rows(b, h, c), vo + h)),
            pl.BlockSpec((C, DV), lambda b, h, c: (rows(b, h, c), ro + h)),
            pl.BlockSpec((C, LANES), lambda b, h, c: (rows(b, h, c), 0)),
            pl.BlockSpec((LANES, DK), lambda b, h, c: (0, h)),
            pl.BlockSpec((1, DK), lambda b, h, c: (0, h)),
            pl.BlockSpec((1, DV), lambda b, h, c: (0, 0)),
        ],
        out_specs=pl.BlockSpec((C, DV), lambda b, h, c: (rows(b, h, c), h)),
        out_shape=jax.ShapeDtypeStruct((B * S, H * DV), BF16),
        scratch_shapes=[pltpu.VMEM((DV, DK), F32)],
        compiler_params=pltpu.CompilerParams(
            dimension_semantics=("parallel", "parallel", "arbitrary"),
            vmem_limit_bytes=_vmem_limit(blocks, 16 << 20)),
        name="gla_chunked",
    )(proj_f, proj_f, proj_b, proj_f, proj_s, w_lr, b_lr, norm_g)


def _fox_body(q_ref, k_ref, v_ref, cq_ref, ck_ref, o_ref, m_ref, l_ref, acc_ref,
              *, T, scale):
    qi = pl.program_id(2)
    ki = pl.program_id(3)

    @pl.when(ki == 0)
    def _():
        m_ref[...] = jnp.full_like(m_ref, -jnp.inf)
        l_ref[...] = jnp.zeros_like(l_ref)
        acc_ref[...] = jnp.zeros_like(acc_ref)

    def step(on_diagonal):
        s = lax.dot_general(q_ref[...], k_ref[...], (((1,), (1,)), ((), ())),
                            preferred_element_type=F32) * scale
        s = s + cq_ref[...] - ck_ref[...]
        if on_diagonal:
            ti = lax.broadcasted_iota(jnp.int32, (T, T), 0)
            si = lax.broadcasted_iota(jnp.int32, (T, T), 1)
            s = jnp.where(si <= ti, s, -jnp.inf)
        m_prev = m_ref[...]
        m_new = jnp.maximum(m_prev, jnp.max(s, axis=-1, keepdims=True))
        p = jnp.exp(s - m_new)
        alpha = jnp.exp(m_prev - m_new)
        l_ref[...] = alpha * l_ref[...] + jnp.sum(p, axis=-1, keepdims=True)
        acc_ref[...] = alpha * acc_ref[...] + jnp.dot(
            p.astype(BF16), v_ref[...], preferred_element_type=F32)
        m_ref[...] = m_new

    @pl.when(ki < qi)
    def _():
        step(False)

    @pl.when(ki == qi)
    def _():
        step(True)
        o_ref[...] = (acc_ref[...] / l_ref[...]).astype(o_ref.dtype)


def _fox(proj_b, c, B, S, cols, dh):
    H = FOX_HEADS
    T = min(FOX_TILE, S)
    NT = S // T
    assert S % T == 0
    qo, ko, vo = cols["fq"] // dh, cols["fk"] // dh, cols["fv"] // dh
    c_col = c.reshape(B, H, S, 1)
    c_row = c.reshape(B, H, 1, S)
    blocks = 3 * _nbytes((T, dh), BF16) + _nbytes((T, LANES), F32) + _nbytes((8, T), F32) \
        + _nbytes((T, dh), BF16)
    return pl.pallas_call(
        functools.partial(_fox_body, T=T, scale=dh ** -0.5),
        grid=(B, H, NT, NT),
        in_specs=[
            pl.BlockSpec((T, dh), lambda b, h, qi, ki: (b * NT + qi, qo + h)),
            pl.BlockSpec((T, dh), lambda b, h, qi, ki: (b * NT + jnp.minimum(ki, qi), ko + h)),
            pl.BlockSpec((T, dh), lambda b, h, qi, ki: (b * NT + jnp.minimum(ki, qi), vo + h)),
            pl.BlockSpec((None, None, T, 1), lambda b, h, qi, ki: (b, h, qi, 0)),
            pl.BlockSpec((None, None, 1, T), lambda b, h, qi, ki: (b, h, 0, jnp.minimum(ki, qi))),
        ],
        out_specs=pl.BlockSpec((T, dh), lambda b, h, qi, ki: (b * NT + qi, h)),
        out_shape=jax.ShapeDtypeStruct((B * S, H * dh), BF16),
        scratch_shapes=[pltpu.VMEM((T, 1), F32), pltpu.VMEM((T, 1), F32),
                        pltpu.VMEM((T, dh), F32)],
        compiler_params=pltpu.CompilerParams(
            dimension_semantics=("parallel", "parallel", "parallel", "arbitrary"),
            vmem_limit_bytes=_vmem_limit(blocks, 6 * _nbytes((T, T), F32))),
        name="fox_attention",
    )(proj_b, proj_b, proj_b, c_col, c_row)


def _merge_body(og_ref, of_ref, wg_ref, wf_ref, ma_ref, mb_ref, o_ref):
    yg = jnp.dot(og_ref[...], wg_ref[...], preferred_element_type=F32)
    yf = jnp.dot(of_ref[...], wf_ref[...], preferred_element_type=F32)
    o_ref[...] = (jax.nn.sigmoid(ma_ref[...]) * yg
                  + jax.nn.sigmoid(mb_ref[...]) * yf).astype(o_ref.dtype)


def _merge(og, of, w_bg, w_bf, proj_f, cols, tm=1024, tn=1024):
    M, KG = og.shape
    KF = of.shape[1]
    N = w_bg.shape[1]
    tm, tn = min(tm, M), min(tn, N)
    ao, bo = cols["ma"] // tn, cols["mb"] // tn
    assert cols["ma"] % tn == 0 and cols["mb"] % tn == 0
    blocks = (_nbytes((tm, KG + KF), BF16) + _nbytes((KG + KF, tn), BF16)
              + 2 * _nbytes((tm, tn), F32) + _nbytes((tm, tn), BF16))
    return pl.pallas_call(
        _merge_body,
        grid=(N // tn, M // tm),
        in_specs=[
            pl.BlockSpec((tm, KG), lambda j, i: (i, 0)),
            pl.BlockSpec((tm, KF), lambda j, i: (i, 0)),
            pl.BlockSpec((KG, tn), lambda j, i: (0, j)),
            pl.BlockSpec((KF, tn), lambda j, i: (0, j)),
            pl.BlockSpec((tm, tn), lambda j, i: (i, ao + j)),
            pl.BlockSpec((tm, tn), lambda j, i: (i, bo + j)),
        ],
        out_specs=pl.BlockSpec((tm, tn), lambda j, i: (i, j)),
        out_shape=jax.ShapeDtypeStruct((M, N), BF16),
        compiler_params=pltpu.CompilerParams(
            dimension_semantics=("parallel", "parallel"),
            vmem_limit_bytes=_vmem_limit(blocks, 3 * _nbytes((tm, tn), F32))),
        name="branch_merge",
    )(og, of, w_bg, w_bf, proj_f, proj_f)


def _proj_ln_body(a_ref, w_ref, x_ref, g_ref, b_ref, o_ref, ob_ref, *, alpha):
    y = alpha * x_ref[...] + jnp.dot(a_ref[...], w_ref[...], preferred_element_type=F32)
    out = _layer_norm(y, g_ref[...], b_ref[...])
    o_ref[...] = out
    ob_ref[...] = out.astype(BF16)


def _proj_ln(a, w, resid, g, b, alpha, tm=512):
    M, K = a.shape
    N = w.shape[1]
    tm = min(tm, M)
    blocks = (_nbytes((tm, K), BF16) + _nbytes((K, N), BF16) + 2 * _nbytes((tm, N), F32)
              + _nbytes((tm, N), BF16))
    return pl.pallas_call(
        functools.partial(_proj_ln_body, alpha=alpha),
        grid=(M // tm,),
        in_specs=[
            pl.BlockSpec((tm, K), lambda i: (i, 0)),
            pl.BlockSpec((K, N), lambda i: (0, 0)),
            pl.BlockSpec((tm, N), lambda i: (i, 0)),
            pl.BlockSpec((1, N), lambda i: (0, 0)),
            pl.BlockSpec((1, N), lambda i: (0, 0)),
        ],
        out_specs=[pl.BlockSpec((tm, N), lambda i: (i, 0)),
                   pl.BlockSpec((tm, N), lambda i: (i, 0))],
        out_shape=[jax.ShapeDtypeStruct((M, N), F32), jax.ShapeDtypeStruct((M, N), BF16)],
        compiler_params=pltpu.CompilerParams(
            dimension_semantics=("parallel",),
            vmem_limit_bytes=_vmem_limit(blocks, 3 * _nbytes((tm, N), F32))),
        name="out_proj_layernorm",
    )(a, w, resid, g, b)


def _ffn_up_body(x_ref, wg_ref, wu_ref, cw_ref, cb_ref, h_ref, gbuf_ref, *, tm, tiles_per_seq):
    i = pl.program_id(1)
    HALO = SUBLANES

    @pl.when(i % tiles_per_seq == 0)
    def _():
        gbuf_ref[0:HALO, :] = jnp.zeros((HALO, gbuf_ref.shape[1]), F32)

    x = x_ref[...]
    g = jnp.dot(x, wg_ref[...], preferred_element_type=F32)
    u = jnp.dot(x, wu_ref[...], preferred_element_type=F32)
    gbuf_ref[HALO:HALO + tm, :] = g
    cw = cw_ref[...]
    y = cb_ref[...] + cw[CONV_W - 1:CONV_W, :] * g
    for j in range(CONV_W - 1):
        off = HALO - (CONV_W - 1) + j
        y = y + cw[j:j + 1, :] * gbuf_ref[off:off + tm, :]
    h_ref[...] = (jax.nn.gelu(y, approximate=True) * u).astype(h_ref.dtype)
    gbuf_ref[0:HALO, :] = gbuf_ref[tm:tm + HALO, :]


def _ffn_up(xb, w_gate, w_up, conv_w, conv_b, S, tm=512, tn=512):
    M, K = xb.shape
    N = w_gate.shape[1]
    tm, tn = min(tm, S), min(tn, N)
    assert S % tm == 0 and N % tn == 0
    blocks = (_nbytes((tm, K), BF16) + 2 * _nbytes((K, tn), BF16) + _nbytes((tm, tn), BF16)
              + _nbytes((8, tn), F32) * 2)
    return pl.pallas_call(
        functools.partial(_ffn_up_body, tm=tm, tiles_per_seq=S // tm),
        grid=(N // tn, M // tm),
        in_specs=[
            pl.BlockSpec((tm, K), lambda j, i: (i, 0)),
            pl.BlockSpec((K, tn), lambda j, i: (0, j)),
            pl.BlockSpec((K, tn), lambda j, i: (0, j)),
            pl.BlockSpec((CONV_W, tn), lambda j, i: (0, j)),
            pl.BlockSpec((1, tn), lambda j, i: (0, j)),
        ],
        out_specs=pl.BlockSpec((tm, tn), lambda j, i: (i, j)),
        out_shape=jax.ShapeDtypeStruct((M, N), BF16),
        scratch_shapes=[pltpu.VMEM((tm + SUBLANES, tn), F32)],
        compiler_params=pltpu.CompilerParams(
            dimension_semantics=("parallel", "arbitrary"),
            vmem_limit_bytes=_vmem_limit(blocks, 6 * _nbytes((tm, tn), F32))),
        name="ffn_gate_up",
    )(xb, w_gate, w_up, conv_w, conv_b)


def _ffn_down_body(h_ref, w_ref, x_ref, g_ref, b_ref, o_ref, ob_ref, acc_ref, *, alpha):
    kk = pl.program_id(1)
    part = jnp.dot(h_ref[...], w_ref[...], preferred_element_type=F32)

    @pl.when(kk == 0)
    def _():
        acc_ref[...] = part

    @pl.when(kk > 0)
    def _():
        acc_ref[...] += part

    @pl.when(kk == pl.num_programs(1) - 1)
    def _():
        out = _layer_norm(alpha * x_ref[...] + acc_ref[...], g_ref[...], b_ref[...])
        o_ref[...] = out
        ob_ref[...] = out.astype(BF16)


def _ffn_down(h, w_down, resid, g, b, alpha, tm=512, tk=512):
    M, K = h.shape
    N = w_down.shape[1]
    tm, tk = min(tm, M), min(tk, K)
    assert M % tm == 0 and K % tk == 0
    blocks = (_nbytes((tm, tk), BF16) + _nbytes((tk, N), BF16) + 2 * _nbytes((tm, N), F32)
              + _nbytes((tm, N), BF16))
    return pl.pallas_call(
        functools.partial(_ffn_down_body, alpha=alpha),
        grid=(M // tm, K // tk),
        in_specs=[
            pl.BlockSpec((tm, tk), lambda i, kk: (i, kk)),
            pl.BlockSpec((tk, N), lambda i, kk: (kk, 0)),
            pl.BlockSpec((tm, N), lambda i, kk: (i, 0)),
            pl.BlockSpec((1, N), lambda i, kk: (0, 0)),
            pl.BlockSpec((1, N), lambda i, kk: (0, 0)),
        ],
        out_specs=[pl.BlockSpec((tm, N), lambda i, kk: (i, 0)),
                   pl.BlockSpec((tm, N), lambda i, kk: (i, 0))],
        out_shape=[jax.ShapeDtypeStruct((M, N), F32), jax.ShapeDtypeStruct((M, N), BF16)],
        scratch_shapes=[pltpu.VMEM((tm, N), F32)],
        compiler_params=pltpu.CompilerParams(
            dimension_semantics=("parallel", "arbitrary"),
            vmem_limit_bytes=_vmem_limit(blocks, 3 * _nbytes((tm, N), F32))),
        name="ffn_down_layernorm",
    )(h, w_down, resid, g, b)


def _ple_body(xb_ref, wg_ref, p_ref, wp_ref, x_ref, o_ref):
    gate = jax.nn.sigmoid(jnp.dot(xb_ref[...], wg_ref[...], preferred_element_type=F32))
    emb = jnp.dot(p_ref[...].astype(BF16), wp_ref[...], preferred_element_type=F32)
    o_ref[...] = x_ref[...] + gate * emb


def _ple(xb, x, p, w_gate, w_proj, tm=1024, tn=1024):
    M, K = xb.shape
    N = w_gate.shape[1]
    P = p.shape[1]
    tm, tn = min(tm, M), min(tn, N)
    blocks = (_nbytes((tm, K), BF16) + _nbytes((K, tn), BF16) + _nbytes((tm, P), F32)
              + _nbytes((P, tn), BF16) + 2 * _nbytes((tm, tn), F32))
    return pl.pallas_call(
        _ple_body,
        grid=(N // tn, M // tm),
        in_specs=[
            pl.BlockSpec((tm, K), lambda j, i: (i, 0)),
            pl.BlockSpec((K, tn), lambda j, i: (0, j)),
            pl.BlockSpec((tm, P), lambda j, i: (i, 0)),
            pl.BlockSpec((P, tn), lambda j, i: (0, j)),
            pl.BlockSpec((tm, tn), lambda j, i: (i, j)),
        ],
        out_specs=pl.BlockSpec((tm, tn), lambda j, i: (i, j)),
        out_shape=jax.ShapeDtypeStruct((M, N), F32),
        compiler_params=pltpu.CompilerParams(
            dimension_semantics=("parallel", "parallel"),
            vmem_limit_bytes=_vmem_limit(blocks, 3 * _nbytes((tm, tn), F32))),
        name="ple_gate",
    )(xb, w_gate, p, w_proj, x)


def _split_w_in(w_in, D, rank):
    gla_qk = D // 2
    gla_v = D
    fox_w = FOX_HEADS * LANES
    names = ("gq", "gk", "gv", "gr", "glr", "fq", "fk", "fv", "ff", "ma", "mb")
    widths = (gla_qk, gla_qk, gla_v, gla_v, rank, fox_w, fox_w, fox_w, FOX_HEADS, D, D)
    assert sum(widths) == w_in.shape[1]
    seg, off = {}, 0
    for n, wd in zip(names, widths):
        seg[n] = w_in[:, off:off + wd]
        off += wd

    def group(order, pad_to=None):
        parts, cols, o = [], {}, 0
        for n in order:
            wseg = seg[n]
            if pad_to is not None:
                wseg = jnp.pad(wseg, ((0, 0), (0, pad_to - wseg.shape[1])))
            cols[n] = o
            o += wseg.shape[1]
            parts.append(wseg)
        return jnp.concatenate(parts, axis=1).astype(BF16), cols

    w_b, cols_b = group(("gv", "fq", "fk", "fv"))
    w_f, cols_f = group(("gq", "gk", "gr", "ma", "mb"))
    w_s, cols_s = group(("glr", "ff"), pad_to=LANES)
    return (w_b, cols_b), (w_f, cols_f), (w_s, cols_s)


def kernel(x, p, w_in, w_gla_lr, b_gla_lr, gla_norm_g, b_forget, w_branch_gla, w_branch_fox,
           w_out, ln1_g, ln1_b, w_gate, w_up, conv_w, conv_b, w_down, ln2_g, ln2_b,
           w_ple_gate, w_ple_proj):
    B, S, D = x.shape
    depth = w_in.shape[0]
    alpha = (2 * depth) ** 0.25
    M = B * S
    xf = x.reshape(M, D)
    for i in range(depth):
        rank = w_gla_lr.shape[1]
        (w_b, cols_b), (w_f, cols_f), (w_s, cols_s) = _split_w_in(w_in[i], D, rank)
        xb = xf.astype(BF16)
        proj_b = _matmul(xb, w_b, BF16, 1024, 1024, "in_proj_bf16")
        proj_f = _matmul(xb, w_f, F32, 1024, 1024, "in_proj_f32")
        proj_s = _matmul(xb, w_s, F32, 1024, 256, "in_proj_small")

        ff = proj_s[:, cols_s["ff"]:cols_s["ff"] + FOX_HEADS]
        c = _forget_cumsum(ff, b_forget[i], B, S)
        of = _fox(proj_b, c, B, S, cols_b, LANES)

        w_lr = jnp.pad(w_gla_lr[i], ((0, LANES - rank), (0, 0))).astype(BF16)
        og = _gla(proj_f, proj_b, proj_s, w_lr, b_gla_lr[i].reshape(1, -1),
                  gla_norm_g[i].reshape(1, -1), B, S, {**cols_f, **cols_b})

        merged = _merge(og, of, w_branch_gla[i].astype(BF16), w_branch_fox[i].astype(BF16),
                        proj_f, cols_f)
        x1, x1b = _proj_ln(merged, w_out[i].astype(BF16), xf, ln1_g[i].reshape(1, -1),
                           ln1_b[i].reshape(1, -1), alpha)
        h = _ffn_up(x1b, w_gate[i].astype(BF16), w_up[i].astype(BF16), conv_w[i],
                    conv_b[i].reshape(1, -1), S)
        x2, x2b = _ffn_down(h, w_down[i].astype(BF16), x1, ln2_g[i].reshape(1, -1),
                            ln2_b[i].reshape(1, -1), alpha)
        xf = _ple(x2b, x2, p[i].reshape(M, -1), w_ple_gate[i].astype(BF16),
                  w_ple_proj[i].astype(BF16))
    return xf.reshape(B, S, D)
```

```python
import functools

import jax
import jax.numpy as jnp
from jax import lax
from jax.experimental import pallas as pl
from jax.experimental.pallas import tpu as pltpu

F32 = jnp.float32
BF16 = jnp.bfloat16

GLA_HEADS = 4
GLA_TAU = 16.0
FOX_HEADS = 8
CONV_W = 3
LN_EPS = 1e-5
RMS_EPS = 1e-6

LANES = 128
SUBLANES = 8
VMEM_BYTES_V7X = 64 * 1024 * 1024

GLA_CHUNK = 128
FOX_TILE = 512
FOX_GROUP = 4


def _vmem_limit(block_bytes, extra_bytes=0):
    est = 2 * block_bytes + extra_bytes + (4 << 20)
    return int(min(max(est, 16 << 20), VMEM_BYTES_V7X - (8 << 20)))


def _nbytes(shape, dtype):
    n = 1
    for s in shape:
        n *= s
    return n * jnp.dtype(dtype).itemsize


def _log_sigmoid(z):
    return jnp.minimum(z, 0.0) - jnp.log1p(jnp.exp(-jnp.abs(z)))


def _layer_norm(y, g, b):
    mu = jnp.mean(y, axis=-1, keepdims=True)
    d = y - mu
    var = jnp.mean(d * d, axis=-1, keepdims=True)
    return d * lax.rsqrt(var + LN_EPS) * g + b


def _mm_body(a_ref, w_ref, cs_ref, o_ref):
    acc = jnp.dot(a_ref[...], w_ref[...], preferred_element_type=F32)
    o_ref[...] = (acc * cs_ref[...]).astype(o_ref.dtype)


def _matmul(a, w, col_scale, window, out_dtype, tm, tn, name):
    M, K = a.shape
    start, N = window
    tm = min(tm, M)
    tn = min(tn, N)
    assert M % tm == 0 and N % tn == 0 and start % tn == 0
    j0 = start // tn
    blocks = (_nbytes((tm, K), a.dtype) + _nbytes((K, tn), w.dtype)
              + _nbytes((tm, tn), out_dtype))
    return pl.pallas_call(
        _mm_body,
        grid=(N // tn, M // tm),
        in_specs=[pl.BlockSpec((tm, K), lambda j, i: (i, 0)),
                  pl.BlockSpec((K, tn), lambda j, i: (0, j0 + j)),
                  pl.BlockSpec((1, tn), lambda j, i: (0, j0 + j))],
        out_specs=pl.BlockSpec((tm, tn), lambda j, i: (i, j)),
        out_shape=jax.ShapeDtypeStruct((M, N), out_dtype),
        compiler_params=pltpu.CompilerParams(
            dimension_semantics=("parallel", "parallel"),
            vmem_limit_bytes=_vmem_limit(blocks, _nbytes((tm, tn), F32))),
        name=name,
    )(a, w, col_scale)


def _fcum_body(ff_ref, bias_ref, c_ref, *, groups):
    z = ff_ref[...] + bias_ref[...]
    x = _log_sigmoid(z)
    rows, lanes = x.shape
    lane = lax.broadcasted_iota(jnp.int32, x.shape, 1)
    s = 1
    while s < lanes:
        x = x + jnp.where(lane >= s, pltpu.roll(x, s, axis=1), 0.0)
        s *= 2
    tot = jnp.broadcast_to(x[:, lanes - 1:lanes], x.shape)
    grp = lax.broadcasted_iota(jnp.int32, x.shape, 0) % groups
    inc = tot
    s = 1
    while s < groups:
        inc = inc + jnp.where(grp >= s, pltpu.roll(inc, s, axis=0), 0.0)
        s *= 2
    c_ref[...] = x + (inc - tot)


def _forget_cumsum(ff, b_forget, B, S):
    H = ff.shape[1]
    groups = S // LANES
    ff_t = ff.reshape(B, S, H).transpose(0, 2, 1).reshape(B * H * groups, LANES)
    bias = jnp.broadcast_to(b_forget.reshape(1, H, 1, 1),
                            (B, H, groups, LANES)).reshape(B * H * groups, LANES)
    c = pl.pallas_call(
        functools.partial(_fcum_body, groups=groups),
        out_shape=jax.ShapeDtypeStruct(ff_t.shape, F32),
        name="fox_forget_cumsum",
    )(ff_t, bias)
    return c.reshape(B, H, S)


def _gla_body(q_ref, k_ref, v_ref, r_ref, lr_ref, wlr_ref, blr_ref, g_ref,
              o_ref, st_ref, *, C, DK, DV):
    c = pl.program_id(2)

    @pl.when(c == 0)
    def _():
        st_ref[...] = jnp.zeros_like(st_ref)

    q = q_ref[...]
    k = k_ref[...]
    v = v_ref[...]
    z = jnp.dot(lr_ref[...].astype(BF16), wlr_ref[...],
                preferred_element_type=F32) + blr_ref[...]
    la = _log_sigmoid(z) * (1.0 / GLA_TAU)

    row = lax.broadcasted_iota(jnp.int32, (C, DK), 0)
    b = la
    s = 1
    while s < C:
        b = b + jnp.where(row >= s, pltpu.roll(b, s, axis=0), 0.0)
        s *= 2

    st = st_ref[...]
    qe = (q * jnp.exp(b)).astype(BF16)
    inter = lax.dot_general(qe, st.astype(BF16), (((1,), (1,)), ((), ())),
                            preferred_element_type=F32)
    blast = b[C - 1:C, :]
    kd = (k * jnp.exp(blast - b)).astype(BF16)

    ti = lax.broadcasted_iota(jnp.int32, (C, C), 0)
    si = lax.broadcasted_iota(jnp.int32, (C, C), 1)
    attn = None
    m = C // 2
    while m >= SUBLANES:
        blk = 2 * m
        nblk = C // blk
        r = jnp.broadcast_to(b.reshape(nblk, blk, DK)[:, m - 1:m, :],
                             (nblk, blk, DK)).reshape(C, DK)
        e = jnp.exp(-jnp.abs(b - r))
        in_b = (row & (blk - 1)) >= m
        ql = jnp.where(in_b, q * e, 0.0).astype(BF16)
        kl = jnp.where(in_b, 0.0, k * e).astype(BF16)
        a = lax.dot_general(ql, kl, (((1,), (1,)), ((), ())),
                            preferred_element_type=F32)
        if nblk > 1:
            a = jnp.where((ti & -blk) == (si & -blk), a, 0.0)
        attn = a if attn is None else attn + a
        m //= 2

    NB = C // SUBLANES
    q3 = q.reshape(NB, SUBLANES, DK)
    k3 = k.reshape(NB, SUBLANES, DK)
    b3 = b.reshape(NB, SUBLANES, DK)
    sub = lax.broadcasted_iota(jnp.int32, (NB, SUBLANES, C), 1)
    rel = (lax.broadcasted_iota(jnp.int32, (NB, SUBLANES, C), 2)
           - SUBLANES * lax.broadcasted_iota(jnp.int32, (NB, SUBLANES, C), 0))
    diag = jnp.zeros((NB, SUBLANES, C), F32)
    for j in range(SUBLANES):
        bj = b3[:, j:j + 1, :]
        kj = k3[:, j:j + 1, :]
        w = q3 * jnp.exp(jnp.minimum(b3 - bj, 0.0)) * kj
        rs = jnp.sum(w, axis=-1, keepdims=True)
        diag = jnp.where((rel == j) & (sub >= j), rs, diag)
    attn = attn + diag.reshape(C, C)

    o = inter + jnp.dot(attn.astype(BF16), v, preferred_element_type=F32)

    upd = lax.dot_general(v, kd, (((0,), (0,)), ((), ())),
                          preferred_element_type=F32)
    st_ref[...] = st * jnp.exp(blast) + upd

    ms = jnp.mean(o * o, axis=-1, keepdims=True)
    on = o * lax.rsqrt(ms + RMS_EPS) * g_ref[...]
    gate = r_ref[...]
    o_ref[...] = (on * (gate * jax.nn.sigmoid(gate))).astype(o_ref.dtype)


def _gla(proj_f, proj_b, proj_s, w_lr, b_lr, norm_g, B, S, cols):
    H = GLA_HEADS
    DK = w_lr.shape[1] // H
    DV = norm_g.shape[1]
    C = min(GLA_CHUNK, S)
    NC = S // C
    assert S % C == 0 and C % (2 * SUBLANES) == 0
    qo, ko, ro, vo = (cols["gq"] // DK, cols["gk"] // DK, cols["gr"] // DV, cols["gv"] // DV)
    rows = lambda b, h, c: b * NC + c
    blocks = (2 * _nbytes((C, DK), F32) + _nbytes((C, DV), BF16) + _nbytes((C, DV), F32)
              + _nbytes((C, LANES), F32) + _nbytes((LANES, DK), BF16) + _nbytes((C, DV), BF16))
    return pl.pallas_call(
        functools.partial(_gla_body, C=C, DK=DK, DV=DV),
        grid=(B, H, NC),
        in_specs=[
            pl.BlockSpec((C, DK), lambda b, h, c: (rows(b, h, c), qo + h)),
            pl.BlockSpec((C, DK), lambda b, h, c: (rows(b, h, c), ko + h)),
            pl.BlockSpec((C, DV), lambda b, h, c: (rows(b, h, c), vo + h)),
            pl.BlockSpec((C, DV), lambda b, h, c: (rows(b, h, c), ro + h)),
            pl.BlockSpec((C, LANES), lambda b, h, c: (rows(b, h, c), 0)),
            pl.BlockSpec((LANES, DK), lambda b, h, c: (0, h)),
            pl.BlockSpec((1, DK), lambda b, h, c: (0, h)),
            pl.BlockSpec((1, DV), lambda b, h, c: (0, 0)),
        ],
        out_specs=pl.BlockSpec((C, DV), lambda b, h, c: (rows(b, h, c), h)),
        out_shape=jax.ShapeDtypeStruct((B * S, H * DV), BF16),
        scratch_shapes=[pltpu.VMEM((DV, DK), F32)],
        compiler_params=pltpu.CompilerParams(
            dimension_semantics=("parallel", "parallel", "arbitrary"),
            vmem_limit_bytes=_vmem_limit(blocks, 16 << 20)),
        name="gla_chunked",
    )(proj_f, proj_f, proj_b, proj_f, proj_s, w_lr, b_lr, norm_g)


def _fox_body(q_ref, k_ref, v_ref, cq_ref, ck_ref, o_ref, m_ref, l_ref, acc_ref, cqb_ref,
              *, T, G):
    qi = pl.program_id(2)
    lane_tiles = T // LANES
    cqb_ref[...] = jnp.broadcast_to(cq_ref[...], cqb_ref.shape)
    m_ref[...] = jnp.full_like(m_ref, -jnp.inf)
    l_ref[...] = jnp.zeros_like(l_ref)
    acc_ref[...] = jnp.zeros_like(acc_ref)

    def block(ki, on_diagonal):
        start = pl.multiple_of(ki * T, T)
        scores = []
        for g in range(G):
            head = slice(g * LANES, (g + 1) * LANES)
            scores.append(lax.dot_general(
                q_ref[:, head], k_ref[pl.ds(start, T), head], (((1,), (1,)), ((), ())),
                preferred_element_type=F32))
        for g in range(G):
            head = slice(g * LANES, (g + 1) * LANES)
            t = scores[g] - ck_ref[g, ki]
            if on_diagonal:
                ti = lax.broadcasted_iota(jnp.int32, (T, T), 0)
                si = lax.broadcasted_iota(jnp.int32, (T, T), 1)
                t = jnp.where(si <= ti, t, -jnp.inf)
            cq = cqb_ref[g]
            m_prev = m_ref[g]
            m_new = jnp.maximum(m_prev, cq + jnp.max(t, axis=-1, keepdims=True))
            p = jnp.exp(t + pltpu.repeat(cq - m_new, lane_tiles, axis=1))
            alpha = jnp.exp(m_prev - m_new)
            l_ref[g] = alpha * l_ref[g] + jnp.sum(p, axis=-1, keepdims=True)
            acc_ref[g] = alpha * acc_ref[g] + jnp.dot(
                p.astype(BF16), v_ref[pl.ds(start, T), head], preferred_element_type=F32)
            m_ref[g] = m_new

    def body(ki, carry):
        block(ki, False)
        return carry

    lax.fori_loop(0, qi, body, 0)
    block(qi, True)
    for g in range(G):
        o_ref[:, g * LANES:(g + 1) * LANES] = (acc_ref[g] / l_ref[g]).astype(o_ref.dtype)


def _fox(proj_b, c, B, S, cols, dh):
    H = FOX_HEADS
    G = FOX_GROUP
    T = min(FOX_TILE, S)
    NT = S // T
    W = G * dh
    assert S % T == 0 and H % G == 0 and dh == LANES
    qo, ko, vo = cols["fq"] // W, cols["fk"] // W, cols["fv"] // W
    assert all(cols[n] % W == 0 for n in ("fq", "fk", "fv"))
    c_col = c.reshape(B, H, S, 1)
    c_row = c.reshape(B, H, NT, 1, T)
    blocks = (2 * _nbytes((T, W), BF16) + 2 * _nbytes((S, W), BF16)
              + G * (_nbytes((T, LANES), F32) + _nbytes((NT, SUBLANES, T), F32)))
    return pl.pallas_call(
        functools.partial(_fox_body, T=T, G=G),
        grid=(B, H // G, NT),
        in_specs=[
            pl.BlockSpec((T, W), lambda b, h, qi: (b * NT + qi, qo + h)),
            pl.BlockSpec((S, W), lambda b, h, qi: (b, ko + h)),
            pl.BlockSpec((S, W), lambda b, h, qi: (b, vo + h)),
            pl.BlockSpec((None, G, T, 1), lambda b, h, qi: (b, h, qi, 0)),
            pl.BlockSpec((None, G, NT, 1, T), lambda b, h, qi: (b, h, 0, 0, 0)),
        ],
        out_specs=pl.BlockSpec((T, W), lambda b, h, qi: (b * NT + qi, h)),
        out_shape=jax.ShapeDtypeStruct((B * S, H * dh), BF16),
        scratch_shapes=[pltpu.VMEM((G, T, LANES), F32), pltpu.VMEM((G, T, LANES), F32),
                        pltpu.VMEM((G, T, dh), F32), pltpu.VMEM((G, T, LANES), F32)],
        compiler_params=pltpu.CompilerParams(
            dimension_semantics=("parallel", "parallel", "arbitrary"),
            vmem_limit_bytes=_vmem_limit(blocks, (2 * G + 4) * _nbytes((T, T), F32))),
        name="fox_attention",
    )(proj_b, proj_b, proj_b, c_col, c_row)


def _merge_body(og_ref, of_ref, wg_ref, wf_ref, ma_ref, mb_ref, o_ref):
    yg = jnp.dot(og_ref[...], wg_ref[...], preferred_element_type=F32)
    yf = jnp.dot(of_ref[...], wf_ref[...], preferred_element_type=F32)
    o_ref[...] = (jax.nn.sigmoid(ma_ref[...]) * yg
                  + jax.nn.sigmoid(mb_ref[...]) * yf).astype(o_ref.dtype)


def _merge(og, of, w_bg, w_bf, proj_f, cols, tm=1024, tn=1024):
    M, KG = og.shape
    KF = of.shape[1]
    N = w_bg.shape[1]
    tm, tn = min(tm, M), min(tn, N)
    ao, bo = cols["ma"] // tn, cols["mb"] // tn
    assert cols["ma"] % tn == 0 and cols["mb"] % tn == 0
    blocks = (_nbytes((tm, KG + KF), BF16) + _nbytes((KG + KF, tn), BF16)
              + 2 * _nbytes((tm, tn), F32) + _nbytes((tm, tn), BF16))
    return pl.pallas_call(
        _merge_body,
        grid=(N // tn, M // tm),
        in_specs=[
            pl.BlockSpec((tm, KG), lambda j, i: (i, 0)),
            pl.BlockSpec((tm, KF), lambda j, i: (i, 0)),
            pl.BlockSpec((KG, tn), lambda j, i: (0, j)),
            pl.BlockSpec((KF, tn), lambda j, i: (0, j)),
            pl.BlockSpec((tm, tn), lambda j, i: (i, ao + j)),
            pl.BlockSpec((tm, tn), lambda j, i: (i, bo + j)),
        ],
        out_specs=pl.BlockSpec((tm, tn), lambda j, i: (i, j)),
        out_shape=jax.ShapeDtypeStruct((M, N), BF16),
        compiler_params=pltpu.CompilerParams(
            dimension_semantics=("parallel", "parallel"),
            vmem_limit_bytes=_vmem_limit(blocks, 3 * _nbytes((tm, tn), F32))),
        name="branch_merge",
    )(og, of, w_bg, w_bf, proj_f, proj_f)


def _proj_ln_body(a_ref, w_ref, x_ref, g_ref, b_ref, o_ref, ob_ref, *, alpha):
    y = alpha * x_ref[...] + jnp.dot(a_ref[...], w_ref[...], preferred_element_type=F32)
    out = _layer_norm(y, g_ref[...], b_ref[...])
    o_ref[...] = out
    ob_ref[...] = out.astype(BF16)


def _proj_ln(a, w, resid, g, b, alpha, tm=512):
    M, K = a.shape
    N = w.shape[1]
    tm = min(tm, M)
    blocks = (_nbytes((tm, K), BF16) + _nbytes((K, N), BF16) + 2 * _nbytes((tm, N), F32)
              + _nbytes((tm, N), BF16))
    return pl.pallas_call(
        functools.partial(_proj_ln_body, alpha=alpha),
        grid=(M // tm,),
        in_specs=[
            pl.BlockSpec((tm, K), lambda i: (i, 0)),
            pl.BlockSpec((K, N), lambda i: (0, 0)),
            pl.BlockSpec((tm, N), lambda i: (i, 0)),
            pl.BlockSpec((1, N), lambda i: (0, 0)),
            pl.BlockSpec((1, N), lambda i: (0, 0)),
        ],
        out_specs=[pl.BlockSpec((tm, N), lambda i: (i, 0)),
                   pl.BlockSpec((tm, N), lambda i: (i, 0))],
        out_shape=[jax.ShapeDtypeStruct((M, N), F32), jax.ShapeDtypeStruct((M, N), BF16)],
        compiler_params=pltpu.CompilerParams(
            dimension_semantics=("parallel",),
            vmem_limit_bytes=_vmem_limit(blocks, 3 * _nbytes((tm, N), F32))),
        name="out_proj_layernorm",
    )(a, w, resid, g, b)


def _ffn_up_body(x_ref, wg_ref, wu_ref, cw_ref, cb_ref, h_ref, wgb_ref, wub_ref, gbuf_ref,
                 *, tm, tiles_per_seq, pr, pc):
    i = pl.program_id(1)
    HALO = SUBLANES
    tn = h_ref.shape[1]

    @pl.when(i == 0)
    def _():
        wgb_ref[...] = wg_ref[...].astype(BF16)
        wub_ref[...] = wu_ref[...].astype(BF16)

    @pl.when(i % tiles_per_seq == 0)
    def _():
        gbuf_ref[0:HALO, :] = jnp.zeros((HALO, tn), F32)

    cw = cw_ref[...]
    cb = cb_ref[...]
    parts = [(r, c) for c in range(tn // pc) for r in range(tm // pr)]

    def matmuls(r, c):
        x = x_ref[r * pr:(r + 1) * pr, :]
        cols = slice(c * pc, (c + 1) * pc)
        g = jnp.dot(x, wgb_ref[:, cols], preferred_element_type=F32)
        u = jnp.dot(x, wub_ref[:, cols], preferred_element_type=F32)
        gbuf_ref[HALO + r * pr:HALO + (r + 1) * pr, cols] = g
        return g, u

    def finish(r, c, g, u):
        cols = slice(c * pc, (c + 1) * pc)
        y = cb[:, cols] + cw[CONV_W - 1:CONV_W, cols] * g
        for j in range(CONV_W - 1):
            off = HALO - (CONV_W - 1) + j + r * pr
            y = y + cw[j:j + 1, cols] * gbuf_ref[off:off + pr, cols]
        h_ref[r * pr:(r + 1) * pr, cols] = (jax.nn.gelu(y, approximate=True) * u).astype(h_ref.dtype)

    pending = None
    for r, c in parts:
        gu = matmuls(r, c)
        if pending is not None:
            finish(*pending)
        pending = (r, c) + gu
    finish(*pending)
    gbuf_ref[0:HALO, :] = gbuf_ref[tm:tm + HALO, :]


def _ffn_up(xb, w_gate, w_up, conv_w, conv_b, S, tm=1024, tn=512, pr=512, pc=256):
    M, K = xb.shape
    N = w_gate.shape[1]
    tm, tn = min(tm, S), min(tn, N)
    pr, pc = min(pr, tm), min(pc, tn)
    assert S % tm == 0 and N % tn == 0 and tm % pr == 0 and tn % pc == 0
    blocks = (_nbytes((tm, K), BF16) + 2 * _nbytes((K, tn), F32) + _nbytes((tm, tn), BF16)
              + _nbytes((8, tn), F32) * 2)
    scratch = 2 * _nbytes((K, tn), BF16) + _nbytes((tm + SUBLANES, tn), F32)
    return pl.pallas_call(
        functools.partial(_ffn_up_body, tm=tm, tiles_per_seq=S // tm, pr=pr, pc=pc),
        grid=(N // tn, M // tm),
        in_specs=[
            pl.BlockSpec((tm, K), lambda j, i: (i, 0)),
            pl.BlockSpec((K, tn), lambda j, i: (0, j)),
            pl.BlockSpec((K, tn), lambda j, i: (0, j)),
            pl.BlockSpec((CONV_W, tn), lambda j, i: (0, j)),
            pl.BlockSpec((1, tn), lambda j, i: (0, j)),
        ],
        out_specs=pl.BlockSpec((tm, tn), lambda j, i: (i, j)),
        out_shape=jax.ShapeDtypeStruct((M, N), BF16),
        scratch_shapes=[pltpu.VMEM((K, tn), BF16), pltpu.VMEM((K, tn), BF16),
                        pltpu.VMEM((tm + SUBLANES, tn), F32)],
        compiler_params=pltpu.CompilerParams(
            dimension_semantics=("parallel", "arbitrary"),
            vmem_limit_bytes=_vmem_limit(blocks, scratch + 8 * _nbytes((pr, pc), F32))),
        name="ffn_gate_up",
    )(xb, w_gate, w_up, conv_w, conv_b)


def _ffn_down_body(h_ref, w_ref, x_ref, g_ref, b_ref, o_ref, ob_ref, acc_ref, *, alpha):
    kk = pl.program_id(1)
    part = jnp.dot(h_ref[...], w_ref[...], preferred_element_type=F32)

    @pl.when(kk == 0)
    def _():
        acc_ref[...] = part

    @pl.when(kk > 0)
    def _():
        acc_ref[...] += part

    @pl.when(kk == pl.num_programs(1) - 1)
    def _():
        out = _layer_norm(alpha * x_ref[...] + acc_ref[...], g_ref[...], b_ref[...])
        o_ref[...] = out
        ob_ref[...] = out.astype(BF16)


def _ffn_down(h, w_down, resid, g, b, alpha, tm=512, tk=512):
    M, K = h.shape
    N = w_down.shape[1]
    tm, tk = min(tm, M), min(tk, K)
    assert M % tm == 0 and K % tk == 0
    blocks = (_nbytes((tm, tk), BF16) + _nbytes((tk, N), BF16) + 2 * _nbytes((tm, N), F32)
              + _nbytes((tm, N), BF16))
    return pl.pallas_call(
        functools.partial(_ffn_down_body, alpha=alpha),
        grid=(M // tm, K // tk),
        in_specs=[
            pl.BlockSpec((tm, tk), lambda i, kk: (i, kk)),
            pl.BlockSpec((tk, N), lambda i, kk: (kk, 0)),
            pl.BlockSpec((tm, N), lambda i, kk: (i, 0)),
            pl.BlockSpec((1, N), lambda i, kk: (0, 0)),
            pl.BlockSpec((1, N), lambda i, kk: (0, 0)),
        ],
        out_specs=[pl.BlockSpec((tm, N), lambda i, kk: (i, 0)),
                   pl.BlockSpec((tm, N), lambda i, kk: (i, 0))],
        out_shape=[jax.ShapeDtypeStruct((M, N), F32), jax.ShapeDtypeStruct((M, N), BF16)],
        scratch_shapes=[pltpu.VMEM((tm, N), F32)],
        compiler_params=pltpu.CompilerParams(
            dimension_semantics=("parallel", "arbitrary"),
            vmem_limit_bytes=_vmem_limit(blocks, 3 * _nbytes((tm, N), F32))),
        name="ffn_down_layernorm",
    )(h, w_down, resid, g, b)


def _ple_body(xb_ref, wg_ref, p_ref, wp_ref, x_ref, o_ref):
    gate = jax.nn.sigmoid(jnp.dot(xb_ref[...], wg_ref[...], preferred_element_type=F32))
    emb = jnp.dot(p_ref[...].astype(BF16), wp_ref[...], preferred_element_type=F32)
    o_ref[...] = x_ref[...] + gate * emb


def _ple(xb, x, p, w_gate, w_proj, tm=1024, tn=1024):
    M, K = xb.shape
    N = w_gate.shape[1]
    P = p.shape[1]
    tm, tn = min(tm, M), min(tn, N)
    blocks = (_nbytes((tm, K), BF16) + _nbytes((K, tn), BF16) + _nbytes((tm, P), F32)
              + _nbytes((P, tn), BF16) + 2 * _nbytes((tm, tn), F32))
    return pl.pallas_call(
        _ple_body,
        grid=(N // tn, M // tm),
        in_specs=[
            pl.BlockSpec((tm, K), lambda j, i: (i, 0)),
            pl.BlockSpec((K, tn), lambda j, i: (0, j)),
            pl.BlockSpec((tm, P), lambda j, i: (i, 0)),
            pl.BlockSpec((P, tn), lambda j, i: (0, j)),
            pl.BlockSpec((tm, tn), lambda j, i: (i, j)),
        ],
        out_specs=pl.BlockSpec((tm, tn), lambda j, i: (i, j)),
        out_shape=jax.ShapeDtypeStruct((M, N), F32),
        compiler_params=pltpu.CompilerParams(
            dimension_semantics=("parallel", "parallel"),
            vmem_limit_bytes=_vmem_limit(blocks, 3 * _nbytes((tm, tn), F32))),
        name="ple_gate",
    )(xb, w_gate, p, w_proj, x)


def _split_w_in(w_in, D, rank):
    gla_qk = D // 2
    gla_v = D
    fox_w = FOX_HEADS * LANES
    names = ("gq", "gk", "gv", "gr", "glr", "fq", "fk", "fv", "ff", "ma", "mb")
    widths = (gla_qk, gla_qk, gla_v, gla_v, rank, fox_w, fox_w, fox_w, FOX_HEADS, D, D)
    assert sum(widths) == w_in.shape[1]
    seg, off = {}, 0
    for n, wd in zip(names, widths):
        seg[n] = w_in[:, off:off + wd]
        off += wd

    out_scale = {"gq": (gla_qk // GLA_HEADS) ** -0.5, "fq": LANES ** -0.5}
    groups = (("gv", "fq", "fk", "fv"),
              ("gq", "gk", "gr", "ma", "mb"),
              ("glr", "ff"))
    parts, scales, layout, o = [], [], [], 0
    for gi, order in enumerate(groups):
        cols, start = {}, o
        for n in order:
            wseg = seg[n]
            if wseg.shape[1] % LANES:
                wseg = jnp.pad(wseg, ((0, 0), (0, LANES - wseg.shape[1] % LANES)))
            cols[n] = o - start
            o += wseg.shape[1]
            parts.append(wseg)
            scales.append(jnp.full((1, wseg.shape[1]), out_scale.get(n, 1.0), F32))
        layout.append((start, o - start, cols))
    return (jnp.concatenate(parts, axis=1).astype(BF16), jnp.concatenate(scales, axis=1), layout)


def kernel(x, p, w_in, w_gla_lr, b_gla_lr, gla_norm_g, b_forget, w_branch_gla, w_branch_fox,
           w_out, ln1_g, ln1_b, w_gate, w_up, conv_w, conv_b, w_down, ln2_g, ln2_b,
           w_ple_gate, w_ple_proj):
    B, S, D = x.shape
    depth = w_in.shape[0]
    alpha = (2 * depth) ** 0.25
    M = B * S
    xf = x.reshape(M, D)
    for i in range(depth):
        rank = w_gla_lr.shape[1]
        w_all, cs_all, layout = _split_w_in(w_in[i], D, rank)
        (win_b, cols_b), (win_f, cols_f), (win_s, cols_s) = [(l[:2], l[2]) for l in layout]
        xb = xf.astype(BF16)
        proj_b = _matmul(xb, w_all, cs_all, win_b, BF16, 1024, 1024, "in_proj_bf16")
        proj_f = _matmul(xb, w_all, cs_all, win_f, F32, 1024, 1024, "in_proj_f32")
        proj_s = _matmul(xb, w_all, cs_all, win_s, F32, 1024, 256, "in_proj_small")

        ff = proj_s[:, cols_s["ff"]:cols_s["ff"] + FOX_HEADS]
        c = _forget_cumsum(ff, b_forget[i], B, S)
        of = _fox(proj_b, c, B, S, cols_b, LANES)

        w_lr = jnp.pad(w_gla_lr[i], ((0, LANES - rank), (0, 0))).astype(BF16)
        og = _gla(proj_f, proj_b, proj_s, w_lr, b_gla_lr[i].reshape(1, -1),
                  gla_norm_g[i].reshape(1, -1), B, S, {**cols_f, **cols_b})

        merged = _merge(og, of, w_branch_gla[i].astype(BF16), w_branch_fox[i].astype(BF16),
                        proj_f, cols_f)
        x1, x1b = _proj_ln(merged, w_out[i].astype(BF16), xf, ln1_g[i].reshape(1, -1),
                           ln1_b[i].reshape(1, -1), alpha)
        h = _ffn_up(x1b, w_gate[i], w_up[i], conv_w[i], conv_b[i].reshape(1, -1), S)
        x2, x2b = _ffn_down(h, w_down[i].astype(BF16), x1, ln2_g[i].reshape(1, -1),
                            ln2_b[i].reshape(1, -1), alpha)
        xf = _ple(x2b, x2, p[i].reshape(M, -1), w_ple_gate[i].astype(BF16),
                  w_ple_proj[i].astype(BF16))
    return xf.reshape(B, S, D)
```

```python
import functools

import jax
import jax.numpy as jnp
from jax import lax
from jax.experimental import pallas as pl
from jax.experimental.pallas import tpu as pltpu

F32 = jnp.float32
BF16 = jnp.bfloat16

GLA_HEADS = 4
GLA_TAU = 16.0
FOX_HEADS = 8
CONV_W = 3
LN_EPS = 1e-5
RMS_EPS = 1e-6
LOG2E = 1.4426950408889634

LANES = 128
SUBLANES = 8
VMEM_BYTES_V7X = 64 * 1024 * 1024

GLA_CHUNK = 128
FOX_TILE = 512
FOX_GROUP = 4


def _vmem_limit(block_bytes, extra_bytes=0):
    est = 2 * block_bytes + extra_bytes + (4 << 20)
    return int(min(max(est, 16 << 20), VMEM_BYTES_V7X - (8 << 20)))


def _nbytes(shape, dtype):
    n = 1
    for s in shape:
        n *= s
    return n * jnp.dtype(dtype).itemsize


def _log_sigmoid(z):
    return jnp.minimum(z, 0.0) - jnp.log(1.0 + jnp.exp(-jnp.abs(z)))


def _layer_norm(y, g, b):
    mu = jnp.mean(y, axis=-1, keepdims=True)
    d = y - mu
    var = jnp.mean(d * d, axis=-1, keepdims=True)
    return d * lax.rsqrt(var + LN_EPS) * g + b


def _mm_body(a_ref, w_ref, cs_ref, o_ref, ab_ref):
    @pl.when(pl.program_id(1) == 0)
    def _():
        ab_ref[...] = a_ref[...].astype(BF16)

    acc = jnp.dot(ab_ref[...], w_ref[...], preferred_element_type=F32)
    o_ref[...] = (acc * cs_ref[...]).astype(o_ref.dtype)


def _matmul(a, w, col_scale, window, out_dtype, tm, tn, name):
    M, K = a.shape
    start, N = window
    tm = min(tm, M)
    tn = min(tn, N)
    assert M % tm == 0 and N % tn == 0 and start % tn == 0
    j0 = start // tn
    blocks = (_nbytes((tm, K), a.dtype) + _nbytes((K, tn), w.dtype)
              + _nbytes((tm, tn), out_dtype))
    return pl.pallas_call(
        _mm_body,
        grid=(M // tm, N // tn),
        in_specs=[pl.BlockSpec((tm, K), lambda i, j: (i, 0)),
                  pl.BlockSpec((K, tn), lambda i, j: (0, j0 + j)),
                  pl.BlockSpec((1, tn), lambda i, j: (0, j0 + j))],
        out_specs=pl.BlockSpec((tm, tn), lambda i, j: (i, j)),
        out_shape=jax.ShapeDtypeStruct((M, N), out_dtype),
        scratch_shapes=[pltpu.VMEM((tm, K), BF16)],
        compiler_params=pltpu.CompilerParams(
            dimension_semantics=("parallel", "arbitrary"),
            vmem_limit_bytes=_vmem_limit(blocks, _nbytes((tm, K), BF16) + _nbytes((tm, tn), F32))),
        name=name,
    )(a, w, col_scale)


def _fcum_body(ff_ref, bias_ref, c_ref, *, groups):
    z = ff_ref[...] + bias_ref[...]
    x = _log_sigmoid(z)
    rows, lanes = x.shape
    lane = lax.broadcasted_iota(jnp.int32, x.shape, 1)
    s = 1
    while s < lanes:
        x = x + jnp.where(lane >= s, pltpu.roll(x, s, axis=1), 0.0)
        s *= 2
    tot = jnp.broadcast_to(x[:, lanes - 1:lanes], x.shape)
    grp = lax.broadcasted_iota(jnp.int32, x.shape, 0) % groups
    inc = tot
    s = 1
    while s < groups:
        inc = inc + jnp.where(grp >= s, pltpu.roll(inc, s, axis=0), 0.0)
        s *= 2
    c_ref[...] = x + (inc - tot)


def _forget_cumsum(ff, b_forget, B, S):
    H = ff.shape[1]
    groups = S // LANES
    ff_t = ff.reshape(B, S, H).transpose(0, 2, 1).reshape(B * H * groups, LANES)
    bias = jnp.broadcast_to(b_forget.reshape(1, H, 1, 1),
                            (B, H, groups, LANES)).reshape(B * H * groups, LANES)
    c = pl.pallas_call(
        functools.partial(_fcum_body, groups=groups),
        out_shape=jax.ShapeDtypeStruct(ff_t.shape, F32),
        name="fox_forget_cumsum",
    )(ff_t, bias)
    return c.reshape(B, H, S)


def _gla_body(q_ref, k_ref, v_ref, r_ref, lr_ref, wlr_ref, blr_ref, g_ref,
              o_ref, st_ref, *, C, DK, DV):
    c = pl.program_id(2)

    @pl.when(c == 0)
    def _():
        st_ref[...] = jnp.zeros_like(st_ref)

    q = q_ref[...]
    k = k_ref[...]
    v = v_ref[...]
    z = jnp.dot(lr_ref[...].astype(BF16), wlr_ref[...],
                preferred_element_type=F32) + blr_ref[...]
    la = _log_sigmoid(z) * (1.0 / GLA_TAU)

    row = lax.broadcasted_iota(jnp.int32, (C, DK), 0)
    b = la
    s = 1
    while s < C:
        b = b + jnp.where(row >= s, pltpu.roll(b, s, axis=0), 0.0)
        s *= 2

    st = st_ref[...]
    qe = (q * jnp.exp(b)).astype(BF16)
    inter = lax.dot_general(qe, st.astype(BF16), (((1,), (1,)), ((), ())),
                            preferred_element_type=F32)
    blast = b[C - 1:C, :]
    kd = (k * jnp.exp(blast - b)).astype(BF16)

    ti = lax.broadcasted_iota(jnp.int32, (C, C), 0)
    si = lax.broadcasted_iota(jnp.int32, (C, C), 1)
    attn = None
    m = C // 2
    while m >= SUBLANES:
        blk = 2 * m
        nblk = C // blk
        r = jnp.broadcast_to(b.reshape(nblk, blk, DK)[:, m - 1:m, :],
                             (nblk, blk, DK)).reshape(C, DK)
        e = jnp.exp(-jnp.abs(b - r))
        in_b = (row & (blk - 1)) >= m
        ql = jnp.where(in_b, q * e, 0.0).astype(BF16)
        kl = jnp.where(in_b, 0.0, k * e).astype(BF16)
        a = lax.dot_general(ql, kl, (((1,), (1,)), ((), ())),
                            preferred_element_type=F32)
        if nblk > 1:
            a = jnp.where((ti & -blk) == (si & -blk), a, 0.0)
        attn = a if attn is None else attn + a
        m //= 2

    NB = C // SUBLANES
    q3 = q.reshape(NB, SUBLANES, DK)
    k3 = k.reshape(NB, SUBLANES, DK)
    b3 = b.reshape(NB, SUBLANES, DK)
    sub = lax.broadcasted_iota(jnp.int32, (NB, SUBLANES, C), 1)
    rel = (lax.broadcasted_iota(jnp.int32, (NB, SUBLANES, C), 2)
           - SUBLANES * lax.broadcasted_iota(jnp.int32, (NB, SUBLANES, C), 0))
    diag = jnp.zeros((NB, SUBLANES, C), F32)
    for j in range(SUBLANES):
        bj = b3[:, j:j + 1, :]
        kj = k3[:, j:j + 1, :]
        w = q3 * jnp.exp(jnp.minimum(b3 - bj, 0.0)) * kj
        rs = jnp.sum(w, axis=-1, keepdims=True)
        diag = jnp.where((rel == j) & (sub >= j), rs, diag)
    attn = attn + diag.reshape(C, C)

    o = inter + jnp.dot(attn.astype(BF16), v, preferred_element_type=F32)

    upd = lax.dot_general(v, kd, (((0,), (0,)), ((), ())),
                          preferred_element_type=F32)
    st_ref[...] = st * jnp.exp(blast) + upd

    ms = jnp.mean(o * o, axis=-1, keepdims=True)
    on = o * lax.rsqrt(ms + RMS_EPS) * g_ref[...]
    gate = r_ref[...]
    o_ref[...] = (on * (gate * jax.nn.sigmoid(gate))).astype(o_ref.dtype)


def _gla(proj_f, proj_b, proj_s, w_lr, b_lr, norm_g, B, S, cols):
    H = GLA_HEADS
    DK = w_lr.shape[1] // H
    DV = norm_g.shape[1]
    C = min(GLA_CHUNK, S)
    NC = S // C
    assert S % C == 0 and C % (2 * SUBLANES) == 0
    qo, ko, ro, vo = (cols["gq"] // DK, cols["gk"] // DK, cols["gr"] // DV, cols["gv"] // DV)
    rows = lambda b, h, c: b * NC + c
    blocks = (2 * _nbytes((C, DK), F32) + _nbytes((C, DV), BF16) + _nbytes((C, DV), F32)
              + _nbytes((C, LANES), F32) + _nbytes((LANES, DK), BF16) + _nbytes((C, DV), BF16))
    return pl.pallas_call(
        functools.partial(_gla_body, C=C, DK=DK, DV=DV),
        grid=(B, H, NC),
        in_specs=[
            pl.BlockSpec((C, DK), lambda b, h, c: (rows(b, h, c), qo + h)),
            pl.BlockSpec((C, DK), lambda b, h, c: (rows(b, h, c), ko + h)),
            pl.BlockSpec((C, DV), lambda b, h, c: (rows(b, h, c), vo + h)),
            pl.BlockSpec((C, DV), lambda b, h, c: (rows(b, h, c), ro + h)),
            pl.BlockSpec((C, LANES), lambda b, h, c: (rows(b, h, c), 0)),
            pl.BlockSpec((LANES, DK), lambda b, h, c: (0, h)),
            pl.BlockSpec((1, DK), lambda b, h, c: (0, h)),
            pl.BlockSpec((1, DV), lambda b, h, c: (0, 0)),
        ],
        out_specs=pl.BlockSpec((C, DV), lambda b, h, c: (rows(b, h, c), h)),
        out_shape=jax.ShapeDtypeStruct((B * S, H * DV), BF16),
        scratch_shapes=[pltpu.VMEM((DV, DK), F32)],
        compiler_params=pltpu.CompilerParams(
            dimension_semantics=("parallel", "parallel", "arbitrary"),
            vmem_limit_bytes=_vmem_limit(blocks, 16 << 20)),
        name="gla_chunked",
    )(proj_f, proj_f, proj_b, proj_f, proj_s, w_lr, b_lr, norm_g)


def _fox_body(q_ref, k_ref, v_ref, c_ref, o_ref, m_ref, l_ref, acc_ref, cqb_ref, *, T, G):
    qi = pl.program_id(2)
    lane_tiles = T // LANES
    for g in range(G):
        cqb_ref[g] = jnp.broadcast_to(c_ref[g, qi] * LOG2E, (LANES, T)).T
    m_ref[...] = jnp.full_like(m_ref, -jnp.inf)
    l_ref[...] = jnp.zeros_like(l_ref)
    acc_ref[...] = jnp.zeros_like(acc_ref)

    def block(ki, on_diagonal):
        start = pl.multiple_of(ki * T, T)
        scores = []
        for g in range(G):
            head = slice(g * LANES, (g + 1) * LANES)
            scores.append(lax.dot_general(
                q_ref[:, head], k_ref[pl.ds(start, T), head], (((1,), (1,)), ((), ())),
                preferred_element_type=F32))
        for g in range(G):
            head = slice(g * LANES, (g + 1) * LANES)
            t = scores[g] - c_ref[g, ki] * LOG2E
            if on_diagonal:
                ti = lax.broadcasted_iota(jnp.int32, (T, T), 0)
                si = lax.broadcasted_iota(jnp.int32, (T, T), 1)
                t = jnp.where(si <= ti, t, -jnp.inf)
            cq = cqb_ref[g]
            m_prev = m_ref[g]
            m_new = jnp.maximum(m_prev, cq + jnp.max(t, axis=-1, keepdims=True))
            p = jnp.exp2(t + jnp.concatenate([cq - m_new] * lane_tiles, axis=1))
            alpha = jnp.exp2(m_prev - m_new)
            l_ref[g] = alpha * l_ref[g] + jnp.sum(p, axis=-1, keepdims=True)
            acc_ref[g] = alpha * acc_ref[g] + jnp.dot(
                p.astype(BF16), v_ref[pl.ds(start, T), head], preferred_element_type=F32)
            m_ref[g] = m_new

    def body(ki, carry):
        block(ki, False)
        return carry

    lax.fori_loop(0, qi, body, 0)
    block(qi, True)
    for g in range(G):
        o_ref[:, g * LANES:(g + 1) * LANES] = (acc_ref[g] / l_ref[g]).astype(o_ref.dtype)


def _fox(proj_b, c, B, S, cols, dh):
    H = FOX_HEADS
    G = FOX_GROUP
    T = min(FOX_TILE, S)
    NT = S // T
    W = G * dh
    assert S % T == 0 and H % G == 0 and dh == LANES
    qo, ko, vo = cols["fq"] // W, cols["fk"] // W, cols["fv"] // W
    assert all(cols[n] % W == 0 for n in ("fq", "fk", "fv"))
    c_row = c.reshape(B, H, NT, 1, T)
    blocks = (2 * _nbytes((T, W), BF16) + 2 * _nbytes((S, W), BF16)
              + G * _nbytes((NT, SUBLANES, T), F32))
    return pl.pallas_call(
        functools.partial(_fox_body, T=T, G=G),
        grid=(B, H // G, NT),
        in_specs=[
            pl.BlockSpec((T, W), lambda b, h, qi: (b * NT + qi, qo + h)),
            pl.BlockSpec((S, W), lambda b, h, qi: (b, ko + h)),
            pl.BlockSpec((S, W), lambda b, h, qi: (b, vo + h)),
            pl.BlockSpec((None, G, NT, 1, T), lambda b, h, qi: (b, h, 0, 0, 0)),
        ],
        out_specs=pl.BlockSpec((T, W), lambda b, h, qi: (b * NT + qi, h)),
        out_shape=jax.ShapeDtypeStruct((B * S, H * dh), BF16),
        scratch_shapes=[pltpu.VMEM((G, T, LANES), F32), pltpu.VMEM((G, T, LANES), F32),
                        pltpu.VMEM((G, T, dh), F32), pltpu.VMEM((G, T, LANES), F32)],
        compiler_params=pltpu.CompilerParams(
            dimension_semantics=("parallel", "parallel", "arbitrary"),
            vmem_limit_bytes=_vmem_limit(blocks, (2 * G + 4) * _nbytes((T, T), F32))),
        name="fox_attention",
    )(proj_b, proj_b, proj_b, c_row)


def _merge_body(og_ref, of_ref, wg_ref, wf_ref, ma_ref, mb_ref, o_ref):
    yg = jnp.dot(og_ref[...], wg_ref[...], preferred_element_type=F32)
    yf = jnp.dot(of_ref[...], wf_ref[...], preferred_element_type=F32)
    o_ref[...] = (jax.nn.sigmoid(ma_ref[...]) * yg
                  + jax.nn.sigmoid(mb_ref[...]) * yf).astype(o_ref.dtype)


def _merge(og, of, w_bg, w_bf, proj_f, cols, tm=1024, tn=1024):
    M, KG = og.shape
    KF = of.shape[1]
    N = w_bg.shape[1]
    tm, tn = min(tm, M), min(tn, N)
    ao, bo = cols["ma"] // tn, cols["mb"] // tn
    assert cols["ma"] % tn == 0 and cols["mb"] % tn == 0
    blocks = (_nbytes((tm, KG + KF), BF16) + _nbytes((KG + KF, tn), BF16)
              + 2 * _nbytes((tm, tn), F32) + _nbytes((tm, tn), BF16))
    return pl.pallas_call(
        _merge_body,
        grid=(N // tn, M // tm),
        in_specs=[
            pl.BlockSpec((tm, KG), lambda j, i: (i, 0)),
            pl.BlockSpec((tm, KF), lambda j, i: (i, 0)),
            pl.BlockSpec((KG, tn), lambda j, i: (0, j)),
            pl.BlockSpec((KF, tn), lambda j, i: (0, j)),
            pl.BlockSpec((tm, tn), lambda j, i: (i, ao + j)),
            pl.BlockSpec((tm, tn), lambda j, i: (i, bo + j)),
        ],
        out_specs=pl.BlockSpec((tm, tn), lambda j, i: (i, j)),
        out_shape=jax.ShapeDtypeStruct((M, N), BF16),
        compiler_params=pltpu.CompilerParams(
            dimension_semantics=("parallel", "parallel"),
            vmem_limit_bytes=_vmem_limit(blocks, 3 * _nbytes((tm, tn), F32))),
        name="branch_merge",
    )(og, of, w_bg, w_bf, proj_f, proj_f)


def _proj_ln_body(a_ref, w_ref, x_ref, g_ref, b_ref, o_ref, ob_ref, *, alpha):
    y = alpha * x_ref[...] + jnp.dot(a_ref[...], w_ref[...], preferred_element_type=F32)
    out = _layer_norm(y, g_ref[...], b_ref[...])
    o_ref[...] = out
    ob_ref[...] = out.astype(BF16)


def _proj_ln(a, w, resid, g, b, alpha, tm=512):
    M, K = a.shape
    N = w.shape[1]
    tm = min(tm, M)
    blocks = (_nbytes((tm, K), BF16) + _nbytes((K, N), BF16) + 2 * _nbytes((tm, N), F32)
              + _nbytes((tm, N), BF16))
    return pl.pallas_call(
        functools.partial(_proj_ln_body, alpha=alpha),
        grid=(M // tm,),
        in_specs=[
            pl.BlockSpec((tm, K), lambda i: (i, 0)),
            pl.BlockSpec((K, N), lambda i: (0, 0)),
            pl.BlockSpec((tm, N), lambda i: (i, 0)),
            pl.BlockSpec((1, N), lambda i: (0, 0)),
            pl.BlockSpec((1, N), lambda i: (0, 0)),
        ],
        out_specs=[pl.BlockSpec((tm, N), lambda i: (i, 0)),
                   pl.BlockSpec((tm, N), lambda i: (i, 0))],
        out_shape=[jax.ShapeDtypeStruct((M, N), F32), jax.ShapeDtypeStruct((M, N), BF16)],
        compiler_params=pltpu.CompilerParams(
            dimension_semantics=("parallel",),
            vmem_limit_bytes=_vmem_limit(blocks, 3 * _nbytes((tm, N), F32))),
        name="out_proj_layernorm",
    )(a, w, resid, g, b)


def _ffn_up_body(x_ref, wg_ref, wu_ref, cw_ref, cb_ref, h_ref, wgb_ref, wub_ref, gbuf_ref,
                 *, tm, tiles_per_seq, pr, pc):
    i = pl.program_id(1)
    HALO = SUBLANES
    tn = h_ref.shape[1]

    @pl.when(i == 0)
    def _():
        wgb_ref[...] = wg_ref[...].astype(BF16)
        wub_ref[...] = wu_ref[...].astype(BF16)

    @pl.when(i % tiles_per_seq == 0)
    def _():
        gbuf_ref[0:HALO, :] = jnp.zeros((HALO, tn), F32)

    cw = cw_ref[...]
    cb = cb_ref[...]
    parts = [(r, c) for c in range(tn // pc) for r in range(tm // pr)]

    def matmuls(r, c):
        x = x_ref[r * pr:(r + 1) * pr, :]
        cols = slice(c * pc, (c + 1) * pc)
        g = jnp.dot(x, wgb_ref[:, cols], preferred_element_type=F32)
        u = jnp.dot(x, wub_ref[:, cols], preferred_element_type=F32)
        gbuf_ref[HALO + r * pr:HALO + (r + 1) * pr, cols] = g
        return g, u

    def finish(r, c, g, u):
        cols = slice(c * pc, (c + 1) * pc)
        y = cb[:, cols] + cw[CONV_W - 1:CONV_W, cols] * g
        for j in range(CONV_W - 1):
            off = HALO - (CONV_W - 1) + j + r * pr
            y = y + cw[j:j + 1, cols] * gbuf_ref[off:off + pr, cols]
        h_ref[r * pr:(r + 1) * pr, cols] = (jax.nn.gelu(y, approximate=True) * u).astype(h_ref.dtype)

    pending = None
    for r, c in parts:
        gu = matmuls(r, c)
        if pending is not None:
            finish(*pending)
        pending = (r, c) + gu
    finish(*pending)
    gbuf_ref[0:HALO, :] = gbuf_ref[tm:tm + HALO, :]


def _ffn_up(xb, w_gate, w_up, conv_w, conv_b, S, tm=1024, tn=512, pr=512, pc=256):
    M, K = xb.shape
    N = w_gate.shape[1]
    tm, tn = min(tm, S), min(tn, N)
    pr, pc = min(pr, tm), min(pc, tn)
    assert S % tm == 0 and N % tn == 0 and tm % pr == 0 and tn % pc == 0
    blocks = (_nbytes((tm, K), BF16) + 2 * _nbytes((K, tn), F32) + _nbytes((tm, tn), BF16)
              + _nbytes((8, tn), F32) * 2)
    scratch = 2 * _nbytes((K, tn), BF16) + _nbytes((tm + SUBLANES, tn), F32)
    return pl.pallas_call(
        functools.partial(_ffn_up_body, tm=tm, tiles_per_seq=S // tm, pr=pr, pc=pc),
        grid=(N // tn, M // tm),
        in_specs=[
            pl.BlockSpec((tm, K), lambda j, i: (i, 0)),
            pl.BlockSpec((K, tn), lambda j, i: (0, j)),
            pl.BlockSpec((K, tn), lambda j, i: (0, j)),
            pl.BlockSpec((CONV_W, tn), lambda j, i: (0, j)),
            pl.BlockSpec((1, tn), lambda j, i: (0, j)),
        ],
        out_specs=pl.BlockSpec((tm, tn), lambda j, i: (i, j)),
        out_shape=jax.ShapeDtypeStruct((M, N), BF16),
        scratch_shapes=[pltpu.VMEM((K, tn), BF16), pltpu.VMEM((K, tn), BF16),
                        pltpu.VMEM((tm + SUBLANES, tn), F32)],
        compiler_params=pltpu.CompilerParams(
            dimension_semantics=("parallel", "arbitrary"),
            vmem_limit_bytes=_vmem_limit(blocks, scratch + 8 * _nbytes((pr, pc), F32))),
        name="ffn_gate_up",
    )(xb, w_gate, w_up, conv_w, conv_b)


def _ffn_down_body(h_ref, w_ref, x_ref, g_ref, b_ref, o_ref, ob_ref, acc_ref, *, alpha):
    kk = pl.program_id(1)

    @pl.when(kk == 0)
    def _():
        acc_ref[...] = jnp.zeros_like(acc_ref)

    acc_ref[...] += jnp.dot(h_ref[...], w_ref[...], preferred_element_type=F32)

    @pl.when(kk == pl.num_programs(1) - 1)
    def _():
        out = _layer_norm(alpha * x_ref[...] + acc_ref[...], g_ref[...], b_ref[...])
        o_ref[...] = out
        ob_ref[...] = out.astype(BF16)


def _ffn_down(h, w_down, resid, g, b, alpha, tm=512, tk=512):
    M, K = h.shape
    N = w_down.shape[1]
    tm, tk = min(tm, M), min(tk, K)
    assert M % tm == 0 and K % tk == 0
    blocks = (_nbytes((tm, tk), BF16) + _nbytes((tk, N), BF16) + 2 * _nbytes((tm, N), F32)
              + _nbytes((tm, N), BF16))
    return pl.pallas_call(
        functools.partial(_ffn_down_body, alpha=alpha),
        grid=(M // tm, K // tk),
        in_specs=[
            pl.BlockSpec((tm, tk), lambda i, kk: (i, kk)),
            pl.BlockSpec((tk, N), lambda i, kk: (kk, 0)),
            pl.BlockSpec((tm, N), lambda i, kk: (i, 0)),
            pl.BlockSpec((1, N), lambda i, kk: (0, 0)),
            pl.BlockSpec((1, N), lambda i, kk: (0, 0)),
        ],
        out_specs=[pl.BlockSpec((tm, N), lambda i, kk: (i, 0)),
                   pl.BlockSpec((tm, N), lambda i, kk: (i, 0))],
        out_shape=[jax.ShapeDtypeStruct((M, N), F32), jax.ShapeDtypeStruct((M, N), BF16)],
        scratch_shapes=[pltpu.VMEM((tm, N), F32)],
        compiler_params=pltpu.CompilerParams(
            dimension_semantics=("parallel", "arbitrary"),
            vmem_limit_bytes=_vmem_limit(blocks, 3 * _nbytes((tm, N), F32))),
        name="ffn_down_layernorm",
    )(h, w_down, resid, g, b)


def _ple_body(xb_ref, wg_ref, p_ref, wp_ref, x_ref, o_ref):
    gate = jax.nn.sigmoid(jnp.dot(xb_ref[...], wg_ref[...], preferred_element_type=F32))
    emb = jnp.dot(p_ref[...].astype(BF16), wp_ref[...], preferred_element_type=F32)
    o_ref[...] = x_ref[...] + gate * emb


def _ple(xb, x, p, w_gate, w_proj, tm=1024, tn=1024):
    M, K = xb.shape
    N = w_gate.shape[1]
    P = p.shape[1]
    tm, tn = min(tm, M), min(tn, N)
    blocks = (_nbytes((tm, K), BF16) + _nbytes((K, tn), BF16) + _nbytes((tm, P), F32)
              + _nbytes((P, tn), BF16) + 2 * _nbytes((tm, tn), F32))
    return pl.pallas_call(
        _ple_body,
        grid=(N // tn, M // tm),
        in_specs=[
            pl.BlockSpec((tm, K), lambda j, i: (i, 0)),
            pl.BlockSpec((K, tn), lambda j, i: (0, j)),
            pl.BlockSpec((tm, P), lambda j, i: (i, 0)),
            pl.BlockSpec((P, tn), lambda j, i: (0, j)),
            pl.BlockSpec((tm, tn), lambda j, i: (i, j)),
        ],
        out_specs=pl.BlockSpec((tm, tn), lambda j, i: (i, j)),
        out_shape=jax.ShapeDtypeStruct((M, N), F32),
        compiler_params=pltpu.CompilerParams(
            dimension_semantics=("parallel", "parallel"),
            vmem_limit_bytes=_vmem_limit(blocks, 3 * _nbytes((tm, tn), F32))),
        name="ple_gate",
    )(xb, w_gate, p, w_proj, x)


def _split_w_in(w_in, D, rank):
    gla_qk = D // 2
    gla_v = D
    fox_w = FOX_HEADS * LANES
    names = ("gq", "gk", "gv", "gr", "glr", "fq", "fk", "fv", "ff", "ma", "mb")
    widths = (gla_qk, gla_qk, gla_v, gla_v, rank, fox_w, fox_w, fox_w, FOX_HEADS, D, D)
    assert sum(widths) == w_in.shape[1]
    seg, off = {}, 0
    for n, wd in zip(names, widths):
        seg[n] = w_in[:, off:off + wd]
        off += wd

    out_scale = {"gq": (gla_qk // GLA_HEADS) ** -0.5, "fq": LANES ** -0.5 * LOG2E}
    groups = (("gv", "fq", "fk", "fv"),
              ("gq", "gk", "gr", "ma", "mb"),
              ("glr", "ff"))
    parts, scales, layout, o = [], [], [], 0
    for gi, order in enumerate(groups):
        cols, start = {}, o
        for n in order:
            wseg = seg[n]
            if wseg.shape[1] % LANES:
                wseg = jnp.pad(wseg, ((0, 0), (0, LANES - wseg.shape[1] % LANES)))
            cols[n] = o - start
            o += wseg.shape[1]
            parts.append(wseg)
            scales.append(jnp.full((1, wseg.shape[1]), out_scale.get(n, 1.0), F32))
        layout.append((start, o - start, cols))
    return (jnp.concatenate(parts, axis=1).astype(BF16), jnp.concatenate(scales, axis=1), layout)


def kernel(x, p, w_in, w_gla_lr, b_gla_lr, gla_norm_g, b_forget, w_branch_gla, w_branch_fox,
           w_out, ln1_g, ln1_b, w_gate, w_up, conv_w, conv_b, w_down, ln2_g, ln2_b,
           w_ple_gate, w_ple_proj):
    B, S, D = x.shape
    depth = w_in.shape[0]
    alpha = (2 * depth) ** 0.25
    M = B * S
    xf = x.reshape(M, D)
    for i in range(depth):
        rank = w_gla_lr.shape[1]
        w_all, cs_all, layout = _split_w_in(w_in[i], D, rank)
        (win_b, cols_b), (win_f, cols_f), (win_s, cols_s) = [(l[:2], l[2]) for l in layout]
        proj_b = _matmul(xf, w_all, cs_all, win_b, BF16, 1024, 1024, "in_proj_bf16")
        proj_f = _matmul(xf, w_all, cs_all, win_f, F32, 1024, 1024, "in_proj_f32")
        proj_s = _matmul(xf, w_all, cs_all, win_s, F32, 1024, 256, "in_proj_small")

        ff = proj_s[:, cols_s["ff"]:cols_s["ff"] + FOX_HEADS]
        c = _forget_cumsum(ff, b_forget[i], B, S)
        of = _fox(proj_b, c, B, S, cols_b, LANES)

        w_lr = jnp.pad(w_gla_lr[i], ((0, LANES - rank), (0, 0))).astype(BF16)
        og = _gla(proj_f, proj_b, proj_s, w_lr, b_gla_lr[i].reshape(1, -1),
                  gla_norm_g[i].reshape(1, -1), B, S, {**cols_f, **cols_b})

        merged = _merge(og, of, w_branch_gla[i].astype(BF16), w_branch_fox[i].astype(BF16),
                        proj_f, cols_f)
        x1, x1b = _proj_ln(merged, w_out[i].astype(BF16), xf, ln1_g[i].reshape(1, -1),
                           ln1_b[i].reshape(1, -1), alpha)
        h = _ffn_up(x1b, w_gate[i], w_up[i], conv_w[i], conv_b[i].reshape(1, -1), S)
        x2, x2b = _ffn_down(h, w_down[i].astype(BF16), x1, ln2_g[i].reshape(1, -1),
                            ln2_b[i].reshape(1, -1), alpha)
        xf = _ple(x2b, x2, p[i].reshape(M, -1), w_ple_gate[i].astype(BF16),
                  w_ple_proj[i].astype(BF16))
    return xf.reshape(B, S, D)
```

```python
import functools

import jax
import jax.numpy as jnp
from jax import lax
from jax.experimental import pallas as pl
from jax.experimental.pallas import tpu as pltpu

F32 = jnp.float32
BF16 = jnp.bfloat16

GLA_HEADS = 4
GLA_TAU = 16.0
FOX_HEADS = 8
CONV_W = 3
LN_EPS = 1e-5
RMS_EPS = 1e-6
LOG2E = 1.4426950408889634

LANES = 128
SUBLANES = 8
VMEM_BYTES_V7X = 64 * 1024 * 1024

IN_PROJ_TILE = 1024
GLA_CHUNK = 128
GLA_GROUP = 4
FOX_TILE = 512
FOX_GROUP = 4


def _vmem_limit(block_bytes, extra_bytes=0):
    est = 2 * block_bytes + extra_bytes + (4 << 20)
    return int(min(max(est, 16 << 20), VMEM_BYTES_V7X - (8 << 20)))


def _nbytes(shape, dtype):
    n = 1
    for s in shape:
        n *= s
    return n * jnp.dtype(dtype).itemsize


def _log_sigmoid(z):
    return jnp.minimum(z, 0.0) - jnp.log(1.0 + jnp.exp(-jnp.abs(z)))


def _layer_norm(y, g, b):
    mu = jnp.mean(y, axis=-1, keepdims=True)
    d = y - mu
    var = jnp.mean(d * d, axis=-1, keepdims=True)
    return d * lax.rsqrt(var + LN_EPS) * g + b


def _mm_nt_body(rows_ref, a_ref, wt_ref, cs_ref, o_ref, wb_ref):
    del rows_ref

    @pl.when(pl.program_id(1) == 0)
    def _():
        wb_ref[...] = wt_ref[0].astype(BF16)

    acc = lax.dot_general(a_ref[...], wb_ref[...], (((1,), (1,)), ((), ())),
                          preferred_element_type=F32)
    o_ref[...] = (acc * cs_ref[...]).astype(o_ref.dtype)


def _matmul_nt(a, w_t, layer, src_rows, col_scale, out_dtype, tm, tn, name):
    M, K = a.shape
    nt = len(src_rows)
    tm = min(tm, M)
    assert M % tm == 0 and col_scale.shape == (1, nt * tn)
    assert all(r % SUBLANES == 0 and r + tn <= w_t.shape[1] for r in src_rows)
    grid_spec = pltpu.PrefetchScalarGridSpec(
        num_scalar_prefetch=1,
        grid=(nt, M // tm),
        in_specs=[pl.BlockSpec((tm, K), lambda j, i, rows: (i, 0)),
                  pl.BlockSpec((pl.Element(1), pl.Element(tn), pl.Element(K)),
                               lambda j, i, rows: (layer, pl.multiple_of(rows[j], SUBLANES), 0)),
                  pl.BlockSpec((1, tn), lambda j, i, rows: (0, j))],
        out_specs=pl.BlockSpec((tm, tn), lambda j, i, rows: (i, j)),
        scratch_shapes=[pltpu.VMEM((tn, K), BF16)],
    )
    blocks = (_nbytes((tm, K), a.dtype) + _nbytes((tn, K), w_t.dtype)
              + _nbytes((tm, tn), out_dtype))
    return pl.pallas_call(
        _mm_nt_body,
        grid_spec=grid_spec,
        out_shape=jax.ShapeDtypeStruct((M, nt * tn), out_dtype),
        compiler_params=pltpu.CompilerParams(
            dimension_semantics=("parallel", "arbitrary"),
            vmem_limit_bytes=_vmem_limit(blocks, _nbytes((tn, K), BF16) + _nbytes((tm, tn), F32))),
        name=name,
    )(jnp.asarray(src_rows, jnp.int32), a, w_t, col_scale)


def _fcum_body(ff_ref, bias_ref, c_ref, *, groups):
    z = ff_ref[...] + bias_ref[...]
    x = _log_sigmoid(z)
    rows, lanes = x.shape
    lane = lax.broadcasted_iota(jnp.int32, x.shape, 1)
    s = 1
    while s < lanes:
        x = x + jnp.where(lane >= s, pltpu.roll(x, s, axis=1), 0.0)
        s *= 2
    tot = jnp.broadcast_to(x[:, lanes - 1:lanes], x.shape)
    grp = lax.broadcasted_iota(jnp.int32, x.shape, 0) % groups
    inc = tot
    s = 1
    while s < groups:
        inc = inc + jnp.where(grp >= s, pltpu.roll(inc, s, axis=0), 0.0)
        s *= 2
    c_ref[...] = x + (inc - tot)


def _forget_cumsum(ff, b_forget, B, S):
    H = ff.shape[1]
    groups = S // LANES
    ff_t = ff.reshape(B, S, H).transpose(0, 2, 1).reshape(B * H * groups, LANES)
    bias = jnp.broadcast_to(b_forget.reshape(1, H, 1, 1),
                            (B, H, groups, LANES)).reshape(B * H * groups, LANES)
    c = pl.pallas_call(
        functools.partial(_fcum_body, groups=groups),
        out_shape=jax.ShapeDtypeStruct(ff_t.shape, F32),
        name="fox_forget_cumsum",
    )(ff_t, bias)
    return c.reshape(B, H, S)


def _gla_body(q_ref, k_ref, v_ref, r_ref, lr_ref, wlr_ref, blr_ref, g_ref,
              o_ref, st_ref, *, C, DK, DV, G):
    c = pl.program_id(2)

    @pl.when(c == 0)
    def _():
        st_ref[...] = jnp.zeros_like(st_ref)

    lr = lr_ref[...].astype(BF16)
    heads = []
    for g in range(G):
        kcols = slice(g * DK, (g + 1) * DK)
        vcols = slice(g * DV, (g + 1) * DV)
        heads.append(_gla_head(
            q_ref[:, kcols], k_ref[:, kcols], v_ref[:, vcols], r_ref[:, vcols], lr,
            wlr_ref[:, kcols], blr_ref[:, kcols], g_ref[...], st_ref.at[g],
            C=C, DK=DK, DV=DV))
    live = list(range(G))
    while live:
        for g in list(live):
            try:
                next(heads[g])
            except StopIteration as done:
                o_ref[:, g * DV:(g + 1) * DV] = done.value.astype(o_ref.dtype)
                live.remove(g)


def _gla_head(q, k, v, gate, lr, wlr, blr, norm_g, st_ref, *, C, DK, DV):
    z = jnp.dot(lr, wlr, preferred_element_type=F32) + blr
    yield
    la = _log_sigmoid(z) * (LOG2E / GLA_TAU)

    row = lax.broadcasted_iota(jnp.int32, (C, DK), 0)
    b = la
    s = 1
    while s < C:
        b = b + jnp.where(row >= s, pltpu.roll(b, s, axis=0), 0.0)
        s *= 2
    yield

    st = st_ref[...]
    qe = (q * jnp.exp2(b)).astype(BF16)
    inter = lax.dot_general(qe, st.astype(BF16), (((1,), (1,)), ((), ())),
                            preferred_element_type=F32)
    blast = b[C - 1:C, :]
    kd = (k * jnp.exp2(blast - b)).astype(BF16)
    yield

    ti = lax.broadcasted_iota(jnp.int32, (C, C), 0)
    si = lax.broadcasted_iota(jnp.int32, (C, C), 1)
    pair = jnp.where(ti > si, ti ^ si, 0)
    attn = jnp.where(ti == si, jnp.sum(q * k, axis=-1, keepdims=True), 0.0)
    NB = C // SUBLANES
    b3 = b.reshape(NB, SUBLANES, DK)
    sub3 = lax.broadcasted_iota(jnp.int32, (NB, SUBLANES, DK), 1)
    m = C // 2
    while m >= 1:
        blk = 2 * m
        in_b = (row & (blk - 1)) >= m
        if m == 1:
            e = jnp.exp2(la)
        else:
            if blk >= SUBLANES:
                nblk = C // blk
                r = jnp.broadcast_to(b.reshape(nblk, blk, DK)[:, m - 1:m, :],
                                     (nblk, blk, DK)).reshape(C, DK)
            else:
                r3 = None
                for start in range(0, SUBLANES, blk):
                    cand = jnp.broadcast_to(b3[:, start + m - 1:start + m, :], b3.shape)
                    r3 = cand if r3 is None else jnp.where(sub3 >= start, cand, r3)
                r = r3.reshape(C, DK)
            e = jnp.exp2(-jnp.abs(b - r))
        x = jnp.where(in_b, q, k)
        x = (jnp.where(in_b, x * e, x) if m == 1 else x * e).astype(BF16)
        a = lax.dot_general(x, x, (((1,), (1,)), ((), ())),
                            preferred_element_type=F32)
        attn = jnp.where((pair >> (m.bit_length() - 1)) == 1, a, attn)
        m //= 2
        yield

    o = inter + jnp.dot(attn.astype(BF16), v, preferred_element_type=F32)
    yield

    upd = lax.dot_general(v, kd, (((0,), (0,)), ((), ())),
                          preferred_element_type=F32)
    st_ref[...] = st * jnp.exp2(blast) + upd
    yield

    ms = jnp.mean(o * o, axis=-1, keepdims=True)
    on = o * lax.rsqrt(ms + RMS_EPS) * norm_g
    return on * (gate * jax.nn.sigmoid(gate))


def _gla(proj_f, proj_b, proj_s, w_lr, b_lr, norm_g, B, S, cols):
    H = GLA_HEADS
    G = GLA_GROUP
    DK = w_lr.shape[1] // H
    DV = norm_g.shape[1]
    C = min(GLA_CHUNK, S)
    NC = S // C
    WK, WV = G * DK, G * DV
    assert S % C == 0 and C % (2 * SUBLANES) == 0 and H % G == 0
    assert all(cols[n] % WK == 0 for n in ("gq", "gk")) and all(cols[n] % WV == 0 for n in ("gr", "gv"))
    qo, ko, ro, vo = (cols["gq"] // WK, cols["gk"] // WK, cols["gr"] // WV, cols["gv"] // WV)
    rows = lambda b, h, c: b * NC + c
    blocks = (2 * _nbytes((C, WK), F32) + _nbytes((C, WV), BF16) + _nbytes((C, WV), F32)
              + _nbytes((C, LANES), F32) + _nbytes((LANES, WK), BF16) + _nbytes((C, WV), BF16))
    return pl.pallas_call(
        functools.partial(_gla_body, C=C, DK=DK, DV=DV, G=G),
        grid=(B, H // G, NC),
        in_specs=[
            pl.BlockSpec((C, WK), lambda b, h, c: (rows(b, h, c), qo + h)),
            pl.BlockSpec((C, WK), lambda b, h, c: (rows(b, h, c), ko + h)),
            pl.BlockSpec((C, WV), lambda b, h, c: (rows(b, h, c), vo + h)),
            pl.BlockSpec((C, WV), lambda b, h, c: (rows(b, h, c), ro + h)),
            pl.BlockSpec((C, LANES), lambda b, h, c: (rows(b, h, c), 0)),
            pl.BlockSpec((LANES, WK), lambda b, h, c: (0, h)),
            pl.BlockSpec((1, WK), lambda b, h, c: (0, h)),
            pl.BlockSpec((1, DV), lambda b, h, c: (0, 0)),
        ],
        out_specs=pl.BlockSpec((C, WV), lambda b, h, c: (rows(b, h, c), h)),
        out_shape=jax.ShapeDtypeStruct((B * S, H * DV), BF16),
        scratch_shapes=[pltpu.VMEM((G, DV, DK), F32)],
        compiler_params=pltpu.CompilerParams(
            dimension_semantics=("parallel", "parallel", "arbitrary"),
            vmem_limit_bytes=_vmem_limit(blocks, G * (12 << 20))),
        name="gla_chunked",
    )(proj_f, proj_f, proj_b, proj_f, proj_s, w_lr, b_lr, norm_g)


def _fox_body(q_ref, k_ref, v_ref, c_ref, o_ref, m_ref, l_ref, acc_ref, cqb_ref, *, T, G):
    qi = pl.program_id(2)
    lane_tiles = T // LANES
    for g in range(G):
        cqb_ref[g] = jnp.broadcast_to(c_ref[g, qi] * LOG2E, (LANES, T)).T
    m_ref[...] = jnp.full_like(m_ref, -jnp.inf)
    l_ref[...] = jnp.zeros_like(l_ref)
    acc_ref[...] = jnp.zeros_like(acc_ref)

    def block(ki, on_diagonal):
        start = pl.multiple_of(ki * T, T)
        scores = []
        for g in range(G):
            head = slice(g * LANES, (g + 1) * LANES)
            scores.append(lax.dot_general(
                q_ref[:, head], k_ref[pl.ds(start, T), head], (((1,), (1,)), ((), ())),
                preferred_element_type=F32))
        for g in range(G):
            head = slice(g * LANES, (g + 1) * LANES)
            t = scores[g] - c_ref[g, ki] * LOG2E
            if on_diagonal:
                ti = lax.broadcasted_iota(jnp.int32, (T, T), 0)
                si = lax.broadcasted_iota(jnp.int32, (T, T), 1)
                t = jnp.where(si <= ti, t, -jnp.inf)
            cq = cqb_ref[g]
            m_prev = m_ref[g]
            m_new = jnp.maximum(m_prev, cq + jnp.max(t, axis=-1, keepdims=True))
            p = jnp.exp2(t + jnp.concatenate([cq - m_new] * lane_tiles, axis=1))
            alpha = jnp.exp2(m_prev - m_new)
            l_ref[g] = alpha * l_ref[g] + jnp.sum(p, axis=-1, keepdims=True)
            acc_ref[g] = alpha * acc_ref[g] + jnp.dot(
                p.astype(BF16), v_ref[pl.ds(start, T), head], preferred_element_type=F32)
            m_ref[g] = m_new

    def body(ki, carry):
        block(ki, False)
        return carry

    lax.fori_loop(0, qi, body, 0)
    block(qi, True)
    for g in range(G):
        o_ref[:, g * LANES:(g + 1) * LANES] = (acc_ref[g] / l_ref[g]).astype(o_ref.dtype)


def _fox(proj_b, c, B, S, cols, dh):
    H = FOX_HEADS
    G = FOX_GROUP
    T = min(FOX_TILE, S)
    NT = S // T
    W = G * dh
    assert S % T == 0 and H % G == 0 and dh == LANES
    qo, ko, vo = cols["fq"] // W, cols["fk"] // W, cols["fv"] // W
    assert all(cols[n] % W == 0 for n in ("fq", "fk", "fv"))
    c_row = c.reshape(B, H, NT, 1, T)
    blocks = (2 * _nbytes((T, W), BF16) + 2 * _nbytes((S, W), BF16)
              + G * _nbytes((NT, SUBLANES, T), F32))
    return pl.pallas_call(
        functools.partial(_fox_body, T=T, G=G),
        grid=(B, H // G, NT),
        in_specs=[
            pl.BlockSpec((T, W), lambda b, h, qi: (b * NT + qi, qo + h)),
            pl.BlockSpec((S, W), lambda b, h, qi: (b, ko + h)),
            pl.BlockSpec((S, W), lambda b, h, qi: (b, vo + h)),
            pl.BlockSpec((None, G, NT, 1, T), lambda b, h, qi: (b, h, 0, 0, 0)),
        ],
        out_specs=pl.BlockSpec((T, W), lambda b, h, qi: (b * NT + qi, h)),
        out_shape=jax.ShapeDtypeStruct((B * S, H * dh), BF16),
        scratch_shapes=[pltpu.VMEM((G, T, LANES), F32), pltpu.VMEM((G, T, LANES), F32),
                        pltpu.VMEM((G, T, dh), F32), pltpu.VMEM((G, T, LANES), F32)],
        compiler_params=pltpu.CompilerParams(
            dimension_semantics=("parallel", "parallel", "arbitrary"),
            vmem_limit_bytes=_vmem_limit(blocks, (2 * G + 4) * _nbytes((T, T), F32))),
        name="fox_attention",
    )(proj_b, proj_b, proj_b, c_row)


def _merge_body(og_ref, of_ref, wg_ref, wf_ref, ma_ref, mb_ref, o_ref):
    yg = jnp.dot(og_ref[...], wg_ref[...], preferred_element_type=F32)
    yf = jnp.dot(of_ref[...], wf_ref[...], preferred_element_type=F32)
    o_ref[...] = (jax.nn.sigmoid(ma_ref[...]) * yg
                  + jax.nn.sigmoid(mb_ref[...]) * yf).astype(o_ref.dtype)


def _merge(og, of, w_bg, w_bf, proj_f, cols, tm=1024, tn=1024):
    M, KG = og.shape
    KF = of.shape[1]
    N = w_bg.shape[1]
    tm, tn = min(tm, M), min(tn, N)
    ao, bo = cols["ma"] // tn, cols["mb"] // tn
    assert cols["ma"] % tn == 0 and cols["mb"] % tn == 0
    blocks = (_nbytes((tm, KG + KF), BF16) + _nbytes((KG + KF, tn), BF16)
              + 2 * _nbytes((tm, tn), F32) + _nbytes((tm, tn), BF16))
    return pl.pallas_call(
        _merge_body,
        grid=(N // tn, M // tm),
        in_specs=[
            pl.BlockSpec((tm, KG), lambda j, i: (i, 0)),
            pl.BlockSpec((tm, KF), lambda j, i: (i, 0)),
            pl.BlockSpec((KG, tn), lambda j, i: (0, j)),
            pl.BlockSpec((KF, tn), lambda j, i: (0, j)),
            pl.BlockSpec((tm, tn), lambda j, i: (i, ao + j)),
            pl.BlockSpec((tm, tn), lambda j, i: (i, bo + j)),
        ],
        out_specs=pl.BlockSpec((tm, tn), lambda j, i: (i, j)),
        out_shape=jax.ShapeDtypeStruct((M, N), BF16),
        compiler_params=pltpu.CompilerParams(
            dimension_semantics=("parallel", "parallel"),
            vmem_limit_bytes=_vmem_limit(blocks, 3 * _nbytes((tm, tn), F32))),
        name="branch_merge",
    )(og, of, w_bg, w_bf, proj_f, proj_f)


def _proj_ln_body(a_ref, w_ref, x_ref, g_ref, b_ref, o_ref, ob_ref, *, alpha):
    y = alpha * x_ref[...] + jnp.dot(a_ref[...], w_ref[...], preferred_element_type=F32)
    out = _layer_norm(y, g_ref[...], b_ref[...])
    o_ref[...] = out
    ob_ref[...] = out.astype(BF16)


def _proj_ln(a, w, resid, g, b, alpha, tm=512):
    M, K = a.shape
    N = w.shape[1]
    tm = min(tm, M)
    blocks = (_nbytes((tm, K), BF16) + _nbytes((K, N), BF16) + 2 * _nbytes((tm, N), F32)
              + _nbytes((tm, N), BF16))
    return pl.pallas_call(
        functools.partial(_proj_ln_body, alpha=alpha),
        grid=(M // tm,),
        in_specs=[
            pl.BlockSpec((tm, K), lambda i: (i, 0)),
            pl.BlockSpec((K, N), lambda i: (0, 0)),
            pl.BlockSpec((tm, N), lambda i: (i, 0)),
            pl.BlockSpec((1, N), lambda i: (0, 0)),
            pl.BlockSpec((1, N), lambda i: (0, 0)),
        ],
        out_specs=[pl.BlockSpec((tm, N), lambda i: (i, 0)),
                   pl.BlockSpec((tm, N), lambda i: (i, 0))],
        out_shape=[jax.ShapeDtypeStruct((M, N), F32), jax.ShapeDtypeStruct((M, N), BF16)],
        compiler_params=pltpu.CompilerParams(
            dimension_semantics=("parallel",),
            vmem_limit_bytes=_vmem_limit(blocks, 3 * _nbytes((tm, N), F32))),
        name="out_proj_layernorm",
    )(a, w, resid, g, b)


def _ffn_up_body(x_ref, wg_ref, wu_ref, cw_ref, cb_ref, h_ref, wgb_ref, wub_ref, gbuf_ref,
                 *, tm, tiles_per_seq, pr, pc):
    i = pl.program_id(1)
    HALO = SUBLANES
    tn = h_ref.shape[1]

    @pl.when(i == 0)
    def _():
        wgb_ref[...] = wg_ref[...].astype(BF16)
        wub_ref[...] = wu_ref[...].astype(BF16)

    @pl.when(i % tiles_per_seq == 0)
    def _():
        gbuf_ref[0:HALO, :] = jnp.zeros((HALO, tn), F32)

    cw = cw_ref[...]
    cb = cb_ref[...]
    parts = [(r, c) for c in range(tn // pc) for r in range(tm // pr)]

    def matmuls(r, c):
        x = x_ref[r * pr:(r + 1) * pr, :]
        cols = slice(c * pc, (c + 1) * pc)
        g = jnp.dot(x, wgb_ref[:, cols], preferred_element_type=F32)
        u = jnp.dot(x, wub_ref[:, cols], preferred_element_type=F32)
        gbuf_ref[HALO + r * pr:HALO + (r + 1) * pr, cols] = g
        return g, u

    def finish(r, c, g, u):
        cols = slice(c * pc, (c + 1) * pc)
        y = cb[:, cols] + cw[CONV_W - 1:CONV_W, cols] * g
        for j in range(CONV_W - 1):
            off = HALO - (CONV_W - 1) + j + r * pr
            y = y + cw[j:j + 1, cols] * gbuf_ref[off:off + pr, cols]
        h_ref[r * pr:(r + 1) * pr, cols] = (jax.nn.gelu(y, approximate=True) * u).astype(h_ref.dtype)

    pending = None
    for r, c in parts:
        gu = matmuls(r, c)
        if pending is not None:
            finish(*pending)
        pending = (r, c) + gu
    finish(*pending)
    gbuf_ref[0:HALO, :] = gbuf_ref[tm:tm + HALO, :]


def _ffn_up(xb, w_gate, w_up, conv_w, conv_b, S, tm=1024, tn=512, pr=512, pc=256):
    M, K = xb.shape
    N = w_gate.shape[1]
    tm, tn = min(tm, S), min(tn, N)
    pr, pc = min(pr, tm), min(pc, tn)
    assert S % tm == 0 and N % tn == 0 and tm % pr == 0 and tn % pc == 0
    blocks = (_nbytes((tm, K), BF16) + 2 * _nbytes((K, tn), F32) + _nbytes((tm, tn), BF16)
              + _nbytes((8, tn), F32) * 2)
    scratch = 2 * _nbytes((K, tn), BF16) + _nbytes((tm + SUBLANES, tn), F32)
    return pl.pallas_call(
        functools.partial(_ffn_up_body, tm=tm, tiles_per_seq=S // tm, pr=pr, pc=pc),
        grid=(N // tn, M // tm),
        in_specs=[
            pl.BlockSpec((tm, K), lambda j, i: (i, 0)),
            pl.BlockSpec((K, tn), lambda j, i: (0, j)),
            pl.BlockSpec((K, tn), lambda j, i: (0, j)),
            pl.BlockSpec((CONV_W, tn), lambda j, i: (0, j)),
            pl.BlockSpec((1, tn), lambda j, i: (0, j)),
        ],
        out_specs=pl.BlockSpec((tm, tn), lambda j, i: (i, j)),
        out_shape=jax.ShapeDtypeStruct((M, N), BF16),
        scratch_shapes=[pltpu.VMEM((K, tn), BF16), pltpu.VMEM((K, tn), BF16),
                        pltpu.VMEM((tm + SUBLANES, tn), F32)],
        compiler_params=pltpu.CompilerParams(
            dimension_semantics=("parallel", "arbitrary"),
            vmem_limit_bytes=_vmem_limit(blocks, scratch + 8 * _nbytes((pr, pc), F32))),
        name="ffn_gate_up",
    )(xb, w_gate, w_up, conv_w, conv_b)


def _ffn_down_body(h_ref, w_ref, x_ref, g_ref, b_ref, o_ref, ob_ref, acc_ref, *, alpha):
    kk = pl.program_id(1)

    @pl.when(kk == 0)
    def _():
        acc_ref[...] = jnp.zeros_like(acc_ref)

    acc_ref[...] += jnp.dot(h_ref[...], w_ref[...], preferred_element_type=F32)

    @pl.when(kk == pl.num_programs(1) - 1)
    def _():
        out = _layer_norm(alpha * x_ref[...] + acc_ref[...], g_ref[...], b_ref[...])
        o_ref[...] = out
        ob_ref[...] = out.astype(BF16)


def _ffn_down(h, w_down, resid, g, b, alpha, tm=512, tk=512):
    M, K = h.shape
    N = w_down.shape[1]
    tm, tk = min(tm, M), min(tk, K)
    assert M % tm == 0 and K % tk == 0
    blocks = (_nbytes((tm, tk), BF16) + _nbytes((tk, N), BF16) + 2 * _nbytes((tm, N), F32)
              + _nbytes((tm, N), BF16))
    return pl.pallas_call(
        functools.partial(_ffn_down_body, alpha=alpha),
        grid=(M // tm, K // tk),
        in_specs=[
            pl.BlockSpec((tm, tk), lambda i, kk: (i, kk)),
            pl.BlockSpec((tk, N), lambda i, kk: (kk, 0)),
            pl.BlockSpec((tm, N), lambda i, kk: (i, 0)),
            pl.BlockSpec((1, N), lambda i, kk: (0, 0)),
            pl.BlockSpec((1, N), lambda i, kk: (0, 0)),
        ],
        out_specs=[pl.BlockSpec((tm, N), lambda i, kk: (i, 0)),
                   pl.BlockSpec((tm, N), lambda i, kk: (i, 0))],
        out_shape=[jax.ShapeDtypeStruct((M, N), F32), jax.ShapeDtypeStruct((M, N), BF16)],
        scratch_shapes=[pltpu.VMEM((tm, N), F32)],
        compiler_params=pltpu.CompilerParams(
            dimension_semantics=("parallel", "arbitrary"),
            vmem_limit_bytes=_vmem_limit(blocks, 3 * _nbytes((tm, N), F32))),
        name="ffn_down_layernorm",
    )(h, w_down, resid, g, b)


def _ple_body(xb_ref, wg_ref, p_ref, wp_ref, x_ref, o_ref):
    gate = jax.nn.sigmoid(jnp.dot(xb_ref[...], wg_ref[...], preferred_element_type=F32))
    emb = jnp.dot(p_ref[...].astype(BF16), wp_ref[...], preferred_element_type=F32)
    o_ref[...] = x_ref[...] + gate * emb


def _ple(xb, x, p, w_gate, w_proj, tm=1024, tn=1024):
    M, K = xb.shape
    N = w_gate.shape[1]
    P = p.shape[1]
    tm, tn = min(tm, M), min(tn, N)
    blocks = (_nbytes((tm, K), BF16) + _nbytes((K, tn), BF16) + _nbytes((tm, P), F32)
              + _nbytes((P, tn), BF16) + 2 * _nbytes((tm, tn), F32))
    return pl.pallas_call(
        _ple_body,
        grid=(N // tn, M // tm),
        in_specs=[
            pl.BlockSpec((tm, K), lambda j, i: (i, 0)),
            pl.BlockSpec((K, tn), lambda j, i: (0, j)),
            pl.BlockSpec((tm, P), lambda j, i: (i, 0)),
            pl.BlockSpec((P, tn), lambda j, i: (0, j)),
            pl.BlockSpec((tm, tn), lambda j, i: (i, j)),
        ],
        out_specs=pl.BlockSpec((tm, tn), lambda j, i: (i, j)),
        out_shape=jax.ShapeDtypeStruct((M, N), F32),
        compiler_params=pltpu.CompilerParams(
            dimension_semantics=("parallel", "parallel"),
            vmem_limit_bytes=_vmem_limit(blocks, 3 * _nbytes((tm, tn), F32))),
        name="ple_gate",
    )(xb, w_gate, p, w_proj, x)


def _in_proj(xb, w_in, layer, rank):
    M, D = xb.shape
    gla_qk = D // 2
    fox_w = FOX_HEADS * LANES
    names = ("gq", "gk", "gv", "gr", "glr", "fq", "fk", "fv", "ff", "ma", "mb")
    widths = (gla_qk, gla_qk, D, D, rank, fox_w, fox_w, fox_w, FOX_HEADS, D, D)
    assert sum(widths) == w_in.shape[2]
    width, offset, off = {}, {}, 0
    for n, wd in zip(names, widths):
        width[n], offset[n] = wd, off
        off += wd
    w_t = jnp.swapaxes(w_in, 1, 2)

    out_scale = {"gq": (gla_qk // GLA_HEADS) ** -0.5, "fq": LANES ** -0.5 * LOG2E}
    T = IN_PROJ_TILE

    def project(order, out_dtype, name):
        src_rows, scales, cols, o = [], [], {}, 0
        for n in order:
            assert width[n] % T == 0
            cols[n] = o
            o += width[n]
            src_rows += [offset[n] + t for t in range(0, width[n], T)]
            scales.append(jnp.full((1, width[n]), out_scale.get(n, 1.0), F32))
        out = _matmul_nt(xb, w_t, layer, src_rows, jnp.concatenate(scales, axis=1),
                         out_dtype, 1024, T, name)
        return out, cols

    proj_b = project(("gv", "fq", "fk", "fv"), BF16, "in_proj_bf16")
    proj_f = project(("gq", "gk", "gr", "ma", "mb"), F32, "in_proj_f32")

    small, cols_s = [], {}
    for t, n in enumerate(("glr", "ff")):
        wseg = w_t[layer, offset[n]:offset[n] + width[n], :]
        small.append(jnp.pad(wseg, ((0, LANES - width[n]), (0, 0))))
        cols_s[n] = t * LANES
    w_small = jnp.concatenate(small, axis=0)[None]
    ns = w_small.shape[1]
    proj_s = _matmul_nt(xb, w_small, 0, [0], jnp.ones((1, ns), F32), F32, 1024, ns,
                        "in_proj_small")
    return proj_b, proj_f, (proj_s, cols_s)


def kernel(x, p, w_in, w_gla_lr, b_gla_lr, gla_norm_g, b_forget, w_branch_gla, w_branch_fox,
           w_out, ln1_g, ln1_b, w_gate, w_up, conv_w, conv_b, w_down, ln2_g, ln2_b,
           w_ple_gate, w_ple_proj):
    B, S, D = x.shape
    depth = w_in.shape[0]
    alpha = (2 * depth) ** 0.25
    M = B * S
    xf = x.reshape(M, D)
    for i in range(depth):
        rank = w_gla_lr.shape[1]
        (proj_b, cols_b), (proj_f, cols_f), (proj_s, cols_s) = _in_proj(
            xf.astype(BF16), w_in, i, rank)

        ff = proj_s[:, cols_s["ff"]:cols_s["ff"] + FOX_HEADS]
        c = _forget_cumsum(ff, b_forget[i], B, S)
        of = _fox(proj_b, c, B, S, cols_b, LANES)

        w_lr = jnp.pad(w_gla_lr[i], ((0, LANES - rank), (0, 0))).astype(BF16)
        og = _gla(proj_f, proj_b, proj_s, w_lr, b_gla_lr[i].reshape(1, -1),
                  gla_norm_g[i].reshape(1, -1), B, S, {**cols_f, **cols_b})

        merged = _merge(og, of, w_branch_gla[i].astype(BF16), w_branch_fox[i].astype(BF16),
                        proj_f, cols_f)
        x1, x1b = _proj_ln(merged, w_out[i].astype(BF16), xf, ln1_g[i].reshape(1, -1),
                           ln1_b[i].reshape(1, -1), alpha)
        h = _ffn_up(x1b, w_gate[i], w_up[i], conv_w[i], conv_b[i].reshape(1, -1), S)
        x2, x2b = _ffn_down(h, w_down[i].astype(BF16), x1, ln2_g[i].reshape(1, -1),
                            ln2_b[i].reshape(1, -1), alpha)
        xf = _ple(x2b, x2, p[i].reshape(M, -1), w_ple_gate[i].astype(BF16),
                  w_ple_proj[i].astype(BF16))
    return xf.reshape(B, S, D)
```

```python
import functools

import jax
import jax.numpy as jnp
from jax import lax
from jax.experimental import pallas as pl
from jax.experimental.pallas import tpu as pltpu

F32 = jnp.float32
BF16 = jnp.bfloat16

GLA_HEADS = 4
GLA_TAU = 16.0
FOX_HEADS = 8
CONV_W = 3
LN_EPS = 1e-5
RMS_EPS = 1e-6
LOG2E = 1.4426950408889634

LANES = 128
SUBLANES = 8
MXU_COLS = 256
VMEM_BYTES_V7X = 64 * 1024 * 1024

IN_PROJ_TILE = 1024
GLA_CHUNK = 128
GLA_GROUP = 4
FFN_ROWS = 64
FOX_TILE = 512
FOX_GROUP = 4


def _vmem_limit(block_bytes, extra_bytes=0):
    est = 2 * block_bytes + extra_bytes + (4 << 20)
    return int(min(max(est, 16 << 20), VMEM_BYTES_V7X - (8 << 20)))


def _nbytes(shape, dtype):
    n = 1
    for s in shape:
        n *= s
    return n * jnp.dtype(dtype).itemsize


def _log_sigmoid(z):
    return jnp.minimum(z, 0.0) - jnp.log(1.0 + jnp.exp(-jnp.abs(z)))


def _layer_norm(y, g, b):
    mu = jnp.mean(y, axis=-1, keepdims=True)
    d = y - mu
    var = jnp.mean(d * d, axis=-1, keepdims=True)
    return d * lax.rsqrt(var + LN_EPS) * g + b


def _mm_nt_body(rows_ref, a_ref, wt_ref, cs_ref, o_ref, wb_ref):
    del rows_ref

    @pl.when(pl.program_id(1) == 0)
    def _():
        wb_ref[...] = wt_ref[0].astype(BF16)

    a = a_ref[...]
    for c in range(0, o_ref.shape[1], MXU_COLS):
        cols = slice(c, c + MXU_COLS)
        acc = lax.dot_general(a, wb_ref[cols, :], (((1,), (1,)), ((), ())),
                              preferred_element_type=F32)
        o_ref[:, cols] = (acc * cs_ref[:, cols]).astype(o_ref.dtype)


def _matmul_nt(a, w_t, layer, src_rows, col_scale, out_dtype, tm, tn, name):
    M, K = a.shape
    nt = len(src_rows)
    tm = min(tm, M)
    assert M % tm == 0 and col_scale.shape == (1, nt * tn)
    assert all(r % SUBLANES == 0 and r + tn <= w_t.shape[1] for r in src_rows)
    grid_spec = pltpu.PrefetchScalarGridSpec(
        num_scalar_prefetch=1,
        grid=(nt, M // tm),
        in_specs=[pl.BlockSpec((tm, K), lambda j, i, rows: (i, 0)),
                  pl.BlockSpec((pl.Element(1), pl.Element(tn), pl.Element(K)),
                               lambda j, i, rows: (layer, pl.multiple_of(rows[j], SUBLANES), 0)),
                  pl.BlockSpec((1, tn), lambda j, i, rows: (0, j))],
        out_specs=pl.BlockSpec((tm, tn), lambda j, i, rows: (i, j)),
        scratch_shapes=[pltpu.VMEM((tn, K), BF16)],
    )
    blocks = (_nbytes((tm, K), a.dtype) + _nbytes((tn, K), w_t.dtype)
              + _nbytes((tm, tn), out_dtype))
    return pl.pallas_call(
        _mm_nt_body,
        grid_spec=grid_spec,
        out_shape=jax.ShapeDtypeStruct((M, nt * tn), out_dtype),
        compiler_params=pltpu.CompilerParams(
            dimension_semantics=("parallel", "arbitrary"),
            vmem_limit_bytes=_vmem_limit(blocks, _nbytes((tn, K), BF16) + _nbytes((tm, tn), F32))),
        name=name,
    )(jnp.asarray(src_rows, jnp.int32), a, w_t, col_scale)


def _fcum_body(ff_ref, bias_ref, c_ref, *, groups):
    z = ff_ref[...] + bias_ref[...]
    x = _log_sigmoid(z)
    rows, lanes = x.shape
    lane = lax.broadcasted_iota(jnp.int32, x.shape, 1)
    s = 1
    while s < lanes:
        x = x + jnp.where(lane >= s, pltpu.roll(x, s, axis=1), 0.0)
        s *= 2
    tot = jnp.broadcast_to(x[:, lanes - 1:lanes], x.shape)
    grp = lax.broadcasted_iota(jnp.int32, x.shape, 0) % groups
    inc = tot
    s = 1
    while s < groups:
        inc = inc + jnp.where(grp >= s, pltpu.roll(inc, s, axis=0), 0.0)
        s *= 2
    c_ref[...] = x + (inc - tot)


def _forget_cumsum(ff, b_forget, B, S):
    H = ff.shape[1]
    groups = S // LANES
    ff_t = ff.reshape(B, S, H).transpose(0, 2, 1).reshape(B * H * groups, LANES)
    bias = jnp.broadcast_to(b_forget.reshape(1, H, 1, 1),
                            (B, H, groups, LANES)).reshape(B * H * groups, LANES)
    c = pl.pallas_call(
        functools.partial(_fcum_body, groups=groups),
        out_shape=jax.ShapeDtypeStruct(ff_t.shape, F32),
        name="fox_forget_cumsum",
    )(ff_t, bias)
    return c.reshape(B, H, S)


def _gla_body(q_ref, k_ref, v_ref, r_ref, lr_ref, wlr_ref, blr_ref, g_ref,
              o_ref, st_ref, *, C, DK, DV, G):
    c = pl.program_id(2)

    @pl.when(c == 0)
    def _():
        st_ref[...] = jnp.zeros_like(st_ref)

    lr = lr_ref[...].astype(BF16)
    heads = []
    for g in range(G):
        kcols = slice(g * DK, (g + 1) * DK)
        vcols = slice(g * DV, (g + 1) * DV)
        heads.append(_gla_head(
            q_ref[:, kcols], k_ref[:, kcols], v_ref[:, vcols], r_ref[:, vcols], lr,
            wlr_ref[:, kcols], blr_ref[:, kcols], g_ref[...], st_ref.at[g],
            C=C, DK=DK, DV=DV))
    live = list(range(G))
    while live:
        for g in list(live):
            try:
                next(heads[g])
            except StopIteration as done:
                o_ref[:, g * DV:(g + 1) * DV] = done.value.astype(o_ref.dtype)
                live.remove(g)


def _gla_head(q, k, v, gate, lr, wlr, blr, norm_g, st_ref, *, C, DK, DV):
    z = jnp.dot(lr, wlr, preferred_element_type=F32) + blr
    yield
    la = _log_sigmoid(z) * (LOG2E / GLA_TAU)

    row = lax.broadcasted_iota(jnp.int32, (C, DK), 0)
    b = la
    s = 1
    while s < C:
        b = b + jnp.where(row >= s, pltpu.roll(b, s, axis=0), 0.0)
        s *= 2
    yield

    st = st_ref[...]
    qe = (q * jnp.exp2(b)).astype(BF16)
    inter = lax.dot_general(qe, st.astype(BF16), (((1,), (1,)), ((), ())),
                            preferred_element_type=F32)
    blast = b[C - 1:C, :]
    kd = (k * jnp.exp2(blast - b)).astype(BF16)
    yield

    ti = lax.broadcasted_iota(jnp.int32, (C, C), 0)
    si = lax.broadcasted_iota(jnp.int32, (C, C), 1)
    pair = jnp.where(ti > si, ti ^ si, 0)
    attn = jnp.where(ti == si, jnp.sum(q * k, axis=-1, keepdims=True), 0.0)
    NB = C // SUBLANES
    b3 = b.reshape(NB, SUBLANES, DK)
    sub3 = lax.broadcasted_iota(jnp.int32, (NB, SUBLANES, DK), 1)
    m = C // 2
    while m >= 1:
        blk = 2 * m
        in_b = (row & (blk - 1)) >= m
        if m == 1:
            e = jnp.exp2(la)
        else:
            if blk >= SUBLANES:
                nblk = C // blk
                r = jnp.broadcast_to(b.reshape(nblk, blk, DK)[:, m - 1:m, :],
                                     (nblk, blk, DK)).reshape(C, DK)
            else:
                r3 = None
                for start in range(0, SUBLANES, blk):
                    cand = jnp.broadcast_to(b3[:, start + m - 1:start + m, :], b3.shape)
                    r3 = cand if r3 is None else jnp.where(sub3 >= start, cand, r3)
                r = r3.reshape(C, DK)
            e = jnp.exp2(-jnp.abs(b - r))
        x = jnp.where(in_b, q, k)
        x = (jnp.where(in_b, x * e, x) if m == 1 else x * e).astype(BF16)
        a = lax.dot_general(x, x, (((1,), (1,)), ((), ())),
                            preferred_element_type=F32)
        attn = jnp.where((pair >> (m.bit_length() - 1)) == 1, a, attn)
        m //= 2
        yield

    o = inter + jnp.dot(attn.astype(BF16), v, preferred_element_type=F32)
    yield

    upd = lax.dot_general(v, kd, (((0,), (0,)), ((), ())),
                          preferred_element_type=F32)
    st_ref[...] = st * jnp.exp2(blast) + upd
    yield

    ms = jnp.mean(o * o, axis=-1, keepdims=True)
    on = o * lax.rsqrt(ms + RMS_EPS) * norm_g
    return on * (gate * jax.nn.sigmoid(gate))


def _gla(proj_f, proj_b, proj_s, w_lr, b_lr, norm_g, B, S, cols):
    H = GLA_HEADS
    G = GLA_GROUP
    DK = w_lr.shape[1] // H
    DV = norm_g.shape[1]
    C = min(GLA_CHUNK, S)
    NC = S // C
    WK, WV = G * DK, G * DV
    assert S % C == 0 and C % (2 * SUBLANES) == 0 and H % G == 0
    assert all(cols[n] % WK == 0 for n in ("gq", "gk")) and all(cols[n] % WV == 0 for n in ("gr", "gv"))
    qo, ko, ro, vo = (cols["gq"] // WK, cols["gk"] // WK, cols["gr"] // WV, cols["gv"] // WV)
    rows = lambda b, h, c: b * NC + c
    blocks = (2 * _nbytes((C, WK), F32) + _nbytes((C, WV), BF16) + _nbytes((C, WV), F32)
              + _nbytes((C, LANES), F32) + _nbytes((LANES, WK), BF16) + _nbytes((C, WV), BF16))
    return pl.pallas_call(
        functools.partial(_gla_body, C=C, DK=DK, DV=DV, G=G),
        grid=(B, H // G, NC),
        in_specs=[
            pl.BlockSpec((C, WK), lambda b, h, c: (rows(b, h, c), qo + h)),
            pl.BlockSpec((C, WK), lambda b, h, c: (rows(b, h, c), ko + h)),
            pl.BlockSpec((C, WV), lambda b, h, c: (rows(b, h, c), vo + h)),
            pl.BlockSpec((C, WV), lambda b, h, c: (rows(b, h, c), ro + h)),
            pl.BlockSpec((C, LANES), lambda b, h, c: (rows(b, h, c), 0)),
            pl.BlockSpec((LANES, WK), lambda b, h, c: (0, h)),
            pl.BlockSpec((1, WK), lambda b, h, c: (0, h)),
            pl.BlockSpec((1, DV), lambda b, h, c: (0, 0)),
        ],
        out_specs=pl.BlockSpec((C, WV), lambda b, h, c: (rows(b, h, c), h)),
        out_shape=jax.ShapeDtypeStruct((B * S, H * DV), BF16),
        scratch_shapes=[pltpu.VMEM((G, DV, DK), F32)],
        compiler_params=pltpu.CompilerParams(
            dimension_semantics=("parallel", "parallel", "arbitrary"),
            vmem_limit_bytes=_vmem_limit(blocks, G * (12 << 20))),
        name="gla_chunked",
    )(proj_f, proj_f, proj_b, proj_f, proj_s, w_lr, b_lr, norm_g)


def _fox_body(q_ref, k_ref, v_ref, c_ref, o_ref, m_ref, l_ref, acc_ref, cqb_ref, *, T, G):
    qi = pl.program_id(2)
    lane_tiles = T // LANES
    for g in range(G):
        cqb_ref[g] = jnp.broadcast_to(c_ref[g, qi] * LOG2E, (LANES, T)).T
    m_ref[...] = jnp.full_like(m_ref, -jnp.inf)
    l_ref[...] = jnp.zeros_like(l_ref)
    acc_ref[...] = jnp.zeros_like(acc_ref)

    def qk(ki):
        start = pl.multiple_of(ki * T, T)
        return [lax.dot_general(
            q_ref[:, g * LANES:(g + 1) * LANES], k_ref[pl.ds(start, T), g * LANES:(g + 1) * LANES],
            (((1,), (1,)), ((), ())), preferred_element_type=F32) for g in range(G)]

    def softmax_pv(ki, scores, on_diagonal):
        start = pl.multiple_of(ki * T, T)
        for g in range(G):
            head = slice(g * LANES, (g + 1) * LANES)
            t = scores[g] - c_ref[g, ki] * LOG2E
            if on_diagonal:
                ti = lax.broadcasted_iota(jnp.int32, (T, T), 0)
                si = lax.broadcasted_iota(jnp.int32, (T, T), 1)
                t = jnp.where(si <= ti, t, -jnp.inf)
            cq = cqb_ref[g]
            m_prev = m_ref[g]
            m_new = jnp.maximum(m_prev, cq + jnp.max(t, axis=-1, keepdims=True))
            p = jnp.exp2(t + jnp.concatenate([cq - m_new] * lane_tiles, axis=1))
            alpha = jnp.exp2(m_prev - m_new)
            l_ref[g] = alpha * l_ref[g] + jnp.sum(p, axis=-1, keepdims=True)
            acc_ref[g] = alpha * acc_ref[g] + jnp.dot(
                p.astype(BF16), v_ref[pl.ds(start, T), head], preferred_element_type=F32)
            m_ref[g] = m_new

    def pair(kp, carry):
        sa = qk(2 * kp)
        sb = qk(2 * kp + 1)
        softmax_pv(2 * kp, sa, False)
        softmax_pv(2 * kp + 1, sb, False)
        return carry

    lax.fori_loop(0, qi // 2, pair, 0)

    @pl.when(qi % 2 == 1)
    def _():
        softmax_pv(qi - 1, qk(qi - 1), False)

    softmax_pv(qi, qk(qi), True)
    for g in range(G):
        o_ref[:, g * LANES:(g + 1) * LANES] = (acc_ref[g] / l_ref[g]).astype(o_ref.dtype)


def _fox(proj_b, c, B, S, cols, dh):
    H = FOX_HEADS
    G = FOX_GROUP
    T = min(FOX_TILE, S)
    NT = S // T
    W = G * dh
    assert S % T == 0 and H % G == 0 and dh == LANES
    qo, ko, vo = cols["fq"] // W, cols["fk"] // W, cols["fv"] // W
    assert all(cols[n] % W == 0 for n in ("fq", "fk", "fv"))
    c_row = c.reshape(B, H, NT, 1, T)
    blocks = (2 * _nbytes((T, W), BF16) + 2 * _nbytes((S, W), BF16)
              + G * _nbytes((NT, SUBLANES, T), F32))
    return pl.pallas_call(
        functools.partial(_fox_body, T=T, G=G),
        grid=(B, H // G, NT),
        in_specs=[
            pl.BlockSpec((T, W), lambda b, h, qi: (b * NT + qi, qo + h)),
            pl.BlockSpec((S, W), lambda b, h, qi: (b, ko + h)),
            pl.BlockSpec((S, W), lambda b, h, qi: (b, vo + h)),
            pl.BlockSpec((None, G, NT, 1, T), lambda b, h, qi: (b, h, 0, 0, 0)),
        ],
        out_specs=pl.BlockSpec((T, W), lambda b, h, qi: (b * NT + qi, h)),
        out_shape=jax.ShapeDtypeStruct((B * S, H * dh), BF16),
        scratch_shapes=[pltpu.VMEM((G, T, LANES), F32), pltpu.VMEM((G, T, LANES), F32),
                        pltpu.VMEM((G, T, dh), F32), pltpu.VMEM((G, T, LANES), F32)],
        compiler_params=pltpu.CompilerParams(
            dimension_semantics=("parallel", "parallel", "arbitrary"),
            vmem_limit_bytes=_vmem_limit(blocks, (4 * G + 8) * _nbytes((T, T), F32))),
        name="fox_attention",
    )(proj_b, proj_b, proj_b, c_row)


def _merge_body(og_ref, of_ref, wg_ref, wf_ref, ma_ref, mb_ref, o_ref):
    yg = jnp.dot(og_ref[...], wg_ref[...], preferred_element_type=F32)
    yf = jnp.dot(of_ref[...], wf_ref[...], preferred_element_type=F32)
    o_ref[...] = (jax.nn.sigmoid(ma_ref[...]) * yg
                  + jax.nn.sigmoid(mb_ref[...]) * yf).astype(o_ref.dtype)


def _merge(og, of, w_bg, w_bf, proj_f, cols, tm=1024, tn=1024):
    M, KG = og.shape
    KF = of.shape[1]
    N = w_bg.shape[1]
    tm, tn = min(tm, M), min(tn, N)
    ao, bo = cols["ma"] // tn, cols["mb"] // tn
    assert cols["ma"] % tn == 0 and cols["mb"] % tn == 0
    blocks = (_nbytes((tm, KG + KF), BF16) + _nbytes((KG + KF, tn), BF16)
              + 2 * _nbytes((tm, tn), F32) + _nbytes((tm, tn), BF16))
    return pl.pallas_call(
        _merge_body,
        grid=(N // tn, M // tm),
        in_specs=[
            pl.BlockSpec((tm, KG), lambda j, i: (i, 0)),
            pl.BlockSpec((tm, KF), lambda j, i: (i, 0)),
            pl.BlockSpec((KG, tn), lambda j, i: (0, j)),
            pl.BlockSpec((KF, tn), lambda j, i: (0, j)),
            pl.BlockSpec((tm, tn), lambda j, i: (i, ao + j)),
            pl.BlockSpec((tm, tn), lambda j, i: (i, bo + j)),
        ],
        out_specs=pl.BlockSpec((tm, tn), lambda j, i: (i, j)),
        out_shape=jax.ShapeDtypeStruct((M, N), BF16),
        compiler_params=pltpu.CompilerParams(
            dimension_semantics=("parallel", "parallel"),
            vmem_limit_bytes=_vmem_limit(blocks, 3 * _nbytes((tm, tn), F32))),
        name="branch_merge",
    )(og, of, w_bg, w_bf, proj_f, proj_f)


def _proj_ln_body(a_ref, w_ref, x_ref, g_ref, b_ref, o_ref, ob_ref, *, alpha):
    y = alpha * x_ref[...] + jnp.dot(a_ref[...], w_ref[...], preferred_element_type=F32)
    out = _layer_norm(y, g_ref[...], b_ref[...])
    o_ref[...] = out
    ob_ref[...] = out.astype(BF16)


def _proj_ln(a, w, resid, g, b, alpha, tm=512):
    M, K = a.shape
    N = w.shape[1]
    tm = min(tm, M)
    blocks = (_nbytes((tm, K), BF16) + _nbytes((K, N), BF16) + 2 * _nbytes((tm, N), F32)
              + _nbytes((tm, N), BF16))
    return pl.pallas_call(
        functools.partial(_proj_ln_body, alpha=alpha),
        grid=(M // tm,),
        in_specs=[
            pl.BlockSpec((tm, K), lambda i: (i, 0)),
            pl.BlockSpec((K, N), lambda i: (0, 0)),
            pl.BlockSpec((tm, N), lambda i: (i, 0)),
            pl.BlockSpec((1, N), lambda i: (0, 0)),
            pl.BlockSpec((1, N), lambda i: (0, 0)),
        ],
        out_specs=[pl.BlockSpec((tm, N), lambda i: (i, 0)),
                   pl.BlockSpec((tm, N), lambda i: (i, 0))],
        out_shape=[jax.ShapeDtypeStruct((M, N), F32), jax.ShapeDtypeStruct((M, N), BF16)],
        compiler_params=pltpu.CompilerParams(
            dimension_semantics=("parallel",),
            vmem_limit_bytes=_vmem_limit(blocks, 3 * _nbytes((tm, N), F32))),
        name="out_proj_layernorm",
    )(a, w, resid, g, b)


def _ffn_up_body(x_ref, wg_ref, wu_ref, cw_ref, cb_ref, h_ref, wgb_ref, wub_ref, gbuf_ref,
                 *, tm, tiles_per_seq, pr, pc):
    i = pl.program_id(1)
    HALO = SUBLANES
    tn = h_ref.shape[1]

    @pl.when(i == 0)
    def _():
        wgb_ref[...] = wg_ref[...].astype(BF16)
        wub_ref[...] = wu_ref[...].astype(BF16)

    @pl.when(i % tiles_per_seq == 0)
    def _():
        gbuf_ref[0:HALO, :] = jnp.zeros((HALO, tn), F32)

    cw = cw_ref[...]
    cb = cb_ref[...]
    parts = [(r, c) for c in range(tn // pc) for r in range(tm // pr)]

    def matmuls(r, c):
        x = x_ref[r * pr:(r + 1) * pr, :]
        cols = slice(c * pc, (c + 1) * pc)
        g = jnp.dot(x, wgb_ref[:, cols], preferred_element_type=F32)
        u = jnp.dot(x, wub_ref[:, cols], preferred_element_type=F32)
        gbuf_ref[HALO + r * pr:HALO + (r + 1) * pr, cols] = g
        return g, u

    def finish(r, c, g, u):
        cols = slice(c * pc, (c + 1) * pc)
        for rr in range(0, pr, FFN_ROWS):
            y = cb[:, cols] + cw[CONV_W - 1:CONV_W, cols] * g[rr:rr + FFN_ROWS]
            for j in range(CONV_W - 1):
                off = HALO - (CONV_W - 1) + j + r * pr + rr
                y = y + cw[j:j + 1, cols] * gbuf_ref[off:off + FFN_ROWS, cols]
            h_ref[r * pr + rr:r * pr + rr + FFN_ROWS, cols] = (
                jax.nn.gelu(y, approximate=True) * u[rr:rr + FFN_ROWS]).astype(h_ref.dtype)

    pending = None
    for r, c in parts:
        gu = matmuls(r, c)
        if pending is not None:
            finish(*pending)
        pending = (r, c) + gu
    finish(*pending)
    gbuf_ref[0:HALO, :] = gbuf_ref[tm:tm + HALO, :]


def _ffn_up(xb, w_gate, w_up, conv_w, conv_b, S, tm=1024, tn=512, pr=256, pc=256):
    M, K = xb.shape
    N = w_gate.shape[1]
    tm, tn = min(tm, S), min(tn, N)
    pr, pc = min(pr, tm), min(pc, tn)
    assert S % tm == 0 and N % tn == 0 and tm % pr == 0 and tn % pc == 0
    blocks = (_nbytes((tm, K), BF16) + 2 * _nbytes((K, tn), F32) + _nbytes((tm, tn), BF16)
              + _nbytes((8, tn), F32) * 2)
    scratch = 2 * _nbytes((K, tn), BF16) + _nbytes((tm + SUBLANES, tn), F32)
    return pl.pallas_call(
        functools.partial(_ffn_up_body, tm=tm, tiles_per_seq=S // tm, pr=pr, pc=pc),
        grid=(N // tn, M // tm),
        in_specs=[
            pl.BlockSpec((tm, K), lambda j, i: (i, 0)),
            pl.BlockSpec((K, tn), lambda j, i: (0, j)),
            pl.BlockSpec((K, tn), lambda j, i: (0, j)),
            pl.BlockSpec((CONV_W, tn), lambda j, i: (0, j)),
            pl.BlockSpec((1, tn), lambda j, i: (0, j)),
        ],
        out_specs=pl.BlockSpec((tm, tn), lambda j, i: (i, j)),
        out_shape=jax.ShapeDtypeStruct((M, N), BF16),
        scratch_shapes=[pltpu.VMEM((K, tn), BF16), pltpu.VMEM((K, tn), BF16),
                        pltpu.VMEM((tm + SUBLANES, tn), F32)],
        compiler_params=pltpu.CompilerParams(
            dimension_semantics=("parallel", "arbitrary"),
            vmem_limit_bytes=_vmem_limit(blocks, scratch + 8 * _nbytes((pr, pc), F32))),
        name="ffn_gate_up",
    )(xb, w_gate, w_up, conv_w, conv_b)


def _ffn_down_body(h_ref, w_ref, x_ref, g_ref, b_ref, o_ref, ob_ref, acc_ref, *, alpha):
    kk = pl.program_id(1)

    @pl.when(kk == 0)
    def _():
        acc_ref[...] = jnp.zeros_like(acc_ref)

    acc_ref[...] += jnp.dot(h_ref[...], w_ref[...], preferred_element_type=F32)

    @pl.when(kk == pl.num_programs(1) - 1)
    def _():
        out = _layer_norm(alpha * x_ref[...] + acc_ref[...], g_ref[...], b_ref[...])
        o_ref[...] = out
        ob_ref[...] = out.astype(BF16)


def _ffn_down(h, w_down, resid, g, b, alpha, tm=512, tk=1408):
    M, K = h.shape
    N = w_down.shape[1]
    tm, tk = min(tm, M), min(tk, K)
    assert M % tm == 0 and K % tk == 0
    blocks = (_nbytes((tm, tk), BF16) + _nbytes((tk, N), BF16) + 2 * _nbytes((tm, N), F32)
              + _nbytes((tm, N), BF16))
    return pl.pallas_call(
        functools.partial(_ffn_down_body, alpha=alpha),
        grid=(M // tm, K // tk),
        in_specs=[
            pl.BlockSpec((tm, tk), lambda i, kk: (i, kk)),
            pl.BlockSpec((tk, N), lambda i, kk: (kk, 0)),
            pl.BlockSpec((tm, N), lambda i, kk: (i, 0)),
            pl.BlockSpec((1, N), lambda i, kk: (0, 0)),
            pl.BlockSpec((1, N), lambda i, kk: (0, 0)),
        ],
        out_specs=[pl.BlockSpec((tm, N), lambda i, kk: (i, 0)),
                   pl.BlockSpec((tm, N), lambda i, kk: (i, 0))],
        out_shape=[jax.ShapeDtypeStruct((M, N), F32), jax.ShapeDtypeStruct((M, N), BF16)],
        scratch_shapes=[pltpu.VMEM((tm, N), F32)],
        compiler_params=pltpu.CompilerParams(
            dimension_semantics=("parallel", "arbitrary"),
            vmem_limit_bytes=_vmem_limit(blocks, 3 * _nbytes((tm, N), F32))),
        name="ffn_down_layernorm",
    )(h, w_down, resid, g, b)


def _ple_body(xb_ref, wg_ref, p_ref, wp_ref, x_ref, o_ref):
    gate = jax.nn.sigmoid(jnp.dot(xb_ref[...], wg_ref[...], preferred_element_type=F32))
    emb = jnp.dot(p_ref[...].astype(BF16), wp_ref[...], preferred_element_type=F32)
    o_ref[...] = x_ref[...] + gate * emb


def _ple(xb, x, p, w_gate, w_proj, tm=1024, tn=1024):
    M, K = xb.shape
    N = w_gate.shape[1]
    P = p.shape[1]
    tm, tn = min(tm, M), min(tn, N)
    blocks = (_nbytes((tm, K), BF16) + _nbytes((K, tn), BF16) + _nbytes((tm, P), F32)
              + _nbytes((P, tn), BF16) + 2 * _nbytes((tm, tn), F32))
    return pl.pallas_call(
        _ple_body,
        grid=(N // tn, M // tm),
        in_specs=[
            pl.BlockSpec((tm, K), lambda j, i: (i, 0)),
            pl.BlockSpec((K, tn), lambda j, i: (0, j)),
            pl.BlockSpec((tm, P), lambda j, i: (i, 0)),
            pl.BlockSpec((P, tn), lambda j, i: (0, j)),
            pl.BlockSpec((tm, tn), lambda j, i: (i, j)),
        ],
        out_specs=pl.BlockSpec((tm, tn), lambda j, i: (i, j)),
        out_shape=jax.ShapeDtypeStruct((M, N), F32),
        compiler_params=pltpu.CompilerParams(
            dimension_semantics=("parallel", "parallel"),
            vmem_limit_bytes=_vmem_limit(blocks, 3 * _nbytes((tm, tn), F32))),
        name="ple_gate",
    )(xb, w_gate, p, w_proj, x)


def _in_proj(xb, w_in, layer, rank):
    M, D = xb.shape
    gla_qk = D // 2
    fox_w = FOX_HEADS * LANES
    names = ("gq", "gk", "gv", "gr", "glr", "fq", "fk", "fv", "ff", "ma", "mb")
    widths = (gla_qk, gla_qk, D, D, rank, fox_w, fox_w, fox_w, FOX_HEADS, D, D)
    assert sum(widths) == w_in.shape[2]
    width, offset, off = {}, {}, 0
    for n, wd in zip(names, widths):
        width[n], offset[n] = wd, off
        off += wd
    w_t = jnp.swapaxes(w_in, 1, 2)

    out_scale = {"gq": (gla_qk // GLA_HEADS) ** -0.5, "fq": LANES ** -0.5 * LOG2E}
    T = IN_PROJ_TILE

    def project(order, out_dtype, name):
        src_rows, scales, cols, o = [], [], {}, 0
        for n in order:
            assert width[n] % T == 0
            cols[n] = o
            o += width[n]
            src_rows += [offset[n] + t for t in range(0, width[n], T)]
            scales.append(jnp.full((1, width[n]), out_scale.get(n, 1.0), F32))
        out = _matmul_nt(xb, w_t, layer, src_rows, jnp.concatenate(scales, axis=1),
                         out_dtype, 1024, T, name)
        return out, cols

    proj_b = project(("gv", "fq", "fk", "fv"), BF16, "in_proj_bf16")
    proj_f = project(("gq", "gk", "gr", "ma", "mb"), F32, "in_proj_f32")

    small, cols_s = [], {}
    for t, n in enumerate(("glr", "ff")):
        wseg = w_t[layer, offset[n]:offset[n] + width[n], :]
        small.append(jnp.pad(wseg, ((0, LANES - width[n]), (0, 0))))
        cols_s[n] = t * LANES
    w_small = jnp.concatenate(small, axis=0)[None]
    ns = w_small.shape[1]
    proj_s = _matmul_nt(xb, w_small, 0, [0], jnp.ones((1, ns), F32), F32, 1024, ns,
                        "in_proj_small")
    return proj_b, proj_f, (proj_s, cols_s)


def kernel(x, p, w_in, w_gla_lr, b_gla_lr, gla_norm_g, b_forget, w_branch_gla, w_branch_fox,
           w_out, ln1_g, ln1_b, w_gate, w_up, conv_w, conv_b, w_down, ln2_g, ln2_b,
           w_ple_gate, w_ple_proj):
    B, S, D = x.shape
    depth = w_in.shape[0]
    alpha = (2 * depth) ** 0.25
    M = B * S
    xf = x.reshape(M, D)
    for i in range(depth):
        rank = w_gla_lr.shape[1]
        (proj_b, cols_b), (proj_f, cols_f), (proj_s, cols_s) = _in_proj(
            xf.astype(BF16), w_in, i, rank)

        ff = proj_s[:, cols_s["ff"]:cols_s["ff"] + FOX_HEADS]
        c = _forget_cumsum(ff, b_forget[i], B, S)
        of = _fox(proj_b, c, B, S, cols_b, LANES)

        w_lr = jnp.pad(w_gla_lr[i], ((0, LANES - rank), (0, 0))).astype(BF16)
        og = _gla(proj_f, proj_b, proj_s, w_lr, b_gla_lr[i].reshape(1, -1),
                  gla_norm_g[i].reshape(1, -1), B, S, {**cols_f, **cols_b})

        merged = _merge(og, of, w_branch_gla[i].astype(BF16), w_branch_fox[i].astype(BF16),
                        proj_f, cols_f)
        x1, x1b = _proj_ln(merged, w_out[i].astype(BF16), xf, ln1_g[i].reshape(1, -1),
                           ln1_b[i].reshape(1, -1), alpha)
        h = _ffn_up(x1b, w_gate[i], w_up[i], conv_w[i], conv_b[i].reshape(1, -1), S)
        x2, x2b = _ffn_down(h, w_down[i].astype(BF16), x1, ln2_g[i].reshape(1, -1),
                            ln2_b[i].reshape(1, -1), alpha)
        xf = _ple(x2b, x2, p[i].reshape(M, -1), w_ple_gate[i].astype(BF16),
                  w_ple_proj[i].astype(BF16))
    return xf.reshape(B, S, D)
```

```python
import functools

import jax
import jax.numpy as jnp
from jax import lax
from jax.experimental import pallas as pl
from jax.experimental.pallas import tpu as pltpu

F32 = jnp.float32
BF16 = jnp.bfloat16

GLA_HEADS = 4
GLA_TAU = 16.0
FOX_HEADS = 8
CONV_W = 3
LN_EPS = 1e-5
RMS_EPS = 1e-6
LOG2E = 1.4426950408889634

LANES = 128
SUBLANES = 8
MXU_COLS = 256
VMEM_BYTES_V7X = 64 * 1024 * 1024

IN_PROJ_TILE = 1024
GLA_CHUNK = 128
GLA_GROUP = 4
FOX_TILE = 512
FOX_GROUP = 4


def _vmem_limit(block_bytes, extra_bytes=0):
    est = 2 * block_bytes + extra_bytes + (4 << 20)
    return int(min(max(est, 16 << 20), VMEM_BYTES_V7X - (8 << 20)))


def _nbytes(shape, dtype):
    n = 1
    for s in shape:
        n *= s
    return n * jnp.dtype(dtype).itemsize


def _log_sigmoid(z):
    return jnp.minimum(z, 0.0) - jnp.log(1.0 + jnp.exp(-jnp.abs(z)))


def _layer_norm(y, g, b):
    mu = jnp.mean(y, axis=-1, keepdims=True)
    d = y - mu
    var = jnp.mean(d * d, axis=-1, keepdims=True)
    return d * lax.rsqrt(var + LN_EPS) * g + b


def _mm_nt_body(rows_ref, a_ref, wt_ref, cs_ref, o_ref, wb_ref):
    del rows_ref

    @pl.when(pl.program_id(1) == 0)
    def _():
        wb_ref[...] = wt_ref[0].astype(BF16)

    a = a_ref[...]
    for c in range(0, o_ref.shape[1], MXU_COLS):
        cols = slice(c, c + MXU_COLS)
        acc = lax.dot_general(a, wb_ref[cols, :], (((1,), (1,)), ((), ())),
                              preferred_element_type=F32)
        o_ref[:, cols] = (acc * cs_ref[:, cols]).astype(o_ref.dtype)


def _matmul_nt(a, w_t, layer, src_rows, col_scale, out_dtype, tm, tn, name):
    M, K = a.shape
    nt = len(src_rows)
    tm = min(tm, M)
    assert M % tm == 0 and col_scale.shape == (1, nt * tn)
    assert all(r % SUBLANES == 0 and r + tn <= w_t.shape[1] for r in src_rows)
    grid_spec = pltpu.PrefetchScalarGridSpec(
        num_scalar_prefetch=1,
        grid=(nt, M // tm),
        in_specs=[pl.BlockSpec((tm, K), lambda j, i, rows: (i, 0)),
                  pl.BlockSpec((pl.Element(1), pl.Element(tn), pl.Element(K)),
                               lambda j, i, rows: (layer, pl.multiple_of(rows[j], SUBLANES), 0)),
                  pl.BlockSpec((1, tn), lambda j, i, rows: (0, j))],
        out_specs=pl.BlockSpec((tm, tn), lambda j, i, rows: (i, j)),
        scratch_shapes=[pltpu.VMEM((tn, K), BF16)],
    )
    blocks = (_nbytes((tm, K), a.dtype) + _nbytes((tn, K), w_t.dtype)
              + _nbytes((tm, tn), out_dtype))
    return pl.pallas_call(
        _mm_nt_body,
        grid_spec=grid_spec,
        out_shape=jax.ShapeDtypeStruct((M, nt * tn), out_dtype),
        compiler_params=pltpu.CompilerParams(
            dimension_semantics=("parallel", "arbitrary"),
            vmem_limit_bytes=_vmem_limit(blocks, _nbytes((tn, K), BF16) + _nbytes((tm, tn), F32))),
        name=name,
    )(jnp.asarray(src_rows, jnp.int32), a, w_t, col_scale)


def _fcum_body(ff_ref, bias_ref, c_ref, *, groups):
    z = ff_ref[...] + bias_ref[...]
    x = _log_sigmoid(z)
    rows, lanes = x.shape
    lane = lax.broadcasted_iota(jnp.int32, x.shape, 1)
    s = 1
    while s < lanes:
        x = x + jnp.where(lane >= s, pltpu.roll(x, s, axis=1), 0.0)
        s *= 2
    tot = jnp.broadcast_to(x[:, lanes - 1:lanes], x.shape)
    grp = lax.broadcasted_iota(jnp.int32, x.shape, 0) % groups
    inc = tot
    s = 1
    while s < groups:
        inc = inc + jnp.where(grp >= s, pltpu.roll(inc, s, axis=0), 0.0)
        s *= 2
    c_ref[...] = x + (inc - tot)


def _forget_cumsum(ff, b_forget, B, S):
    H = ff.shape[1]
    groups = S // LANES
    ff_t = ff.reshape(B, S, H).transpose(0, 2, 1).reshape(B * H * groups, LANES)
    bias = jnp.broadcast_to(b_forget.reshape(1, H, 1, 1),
                            (B, H, groups, LANES)).reshape(B * H * groups, LANES)
    c = pl.pallas_call(
        functools.partial(_fcum_body, groups=groups),
        out_shape=jax.ShapeDtypeStruct(ff_t.shape, F32),
        name="fox_forget_cumsum",
    )(ff_t, bias)
    return c.reshape(B, H, S)


def _gla_body(q_ref, k_ref, v_ref, r_ref, lr_ref, wlr_ref, blr_ref, g_ref,
              o_ref, st_ref, *, C, DK, DV, G):
    c = pl.program_id(2)

    @pl.when(c == 0)
    def _():
        st_ref[...] = jnp.zeros_like(st_ref)

    lr = lr_ref[...].astype(BF16)
    heads = []
    for g in range(G):
        kcols = slice(g * DK, (g + 1) * DK)
        vcols = slice(g * DV, (g + 1) * DV)
        heads.append(_gla_head(
            q_ref[:, kcols], k_ref[:, kcols], v_ref[:, vcols], r_ref[:, vcols], lr,
            wlr_ref[:, kcols], blr_ref[:, kcols], g_ref[...], st_ref.at[g],
            C=C, DK=DK, DV=DV))
    live = list(range(G))
    while live:
        for g in list(live):
            try:
                next(heads[g])
            except StopIteration as done:
                o_ref[:, g * DV:(g + 1) * DV] = done.value.astype(o_ref.dtype)
                live.remove(g)


def _gla_head(q, k, v, gate, lr, wlr, blr, norm_g, st_ref, *, C, DK, DV):
    z = jnp.dot(lr, wlr, preferred_element_type=F32) + blr
    yield
    la = _log_sigmoid(z) * (LOG2E / GLA_TAU)

    row = lax.broadcasted_iota(jnp.int32, (C, DK), 0)
    b = la
    s = 1
    while s < C:
        b = b + jnp.where(row >= s, pltpu.roll(b, s, axis=0), 0.0)
        s *= 2
    yield

    st = st_ref[...]
    qe = (q * jnp.exp2(b)).astype(BF16)
    inter = lax.dot_general(qe, st.astype(BF16), (((1,), (1,)), ((), ())),
                            preferred_element_type=F32)
    blast = b[C - 1:C, :]
    kd = (k * jnp.exp2(blast - b)).astype(BF16)
    yield

    ti = lax.broadcasted_iota(jnp.int32, (C, C), 0)
    si = lax.broadcasted_iota(jnp.int32, (C, C), 1)
    pair = jnp.where(ti > si, ti ^ si, 0)
    attn = jnp.where(ti == si, jnp.sum(q * k, axis=-1, keepdims=True), 0.0)
    NB = C // SUBLANES
    b3 = b.reshape(NB, SUBLANES, DK)
    sub3 = lax.broadcasted_iota(jnp.int32, (NB, SUBLANES, DK), 1)
    m = C // 2
    while m >= 1:
        blk = 2 * m
        in_b = (row & (blk - 1)) >= m
        if m == 1:
            e = jnp.exp2(la)
        else:
            if blk >= SUBLANES:
                nblk = C // blk
                r = jnp.broadcast_to(b.reshape(nblk, blk, DK)[:, m - 1:m, :],
                                     (nblk, blk, DK)).reshape(C, DK)
            else:
                r3 = None
                for start in range(0, SUBLANES, blk):
                    cand = jnp.broadcast_to(b3[:, start + m - 1:start + m, :], b3.shape)
                    r3 = cand if r3 is None else jnp.where(sub3 >= start, cand, r3)
                r = r3.reshape(C, DK)
            e = jnp.exp2(-jnp.abs(b - r))
        x = jnp.where(in_b, q, k)
        x = (jnp.where(in_b, x * e, x) if m == 1 else x * e).astype(BF16)
        a = lax.dot_general(x, x, (((1,), (1,)), ((), ())),
                            preferred_element_type=F32)
        attn = jnp.where((pair >> (m.bit_length() - 1)) == 1, a, attn)
        m //= 2
        yield

    o = inter + jnp.dot(attn.astype(BF16), v, preferred_element_type=F32)
    yield

    upd = lax.dot_general(v, kd, (((0,), (0,)), ((), ())),
                          preferred_element_type=F32)
    st_ref[...] = st * jnp.exp2(blast) + upd
    yield

    ms = jnp.mean(o * o, axis=-1, keepdims=True)
    on = o * lax.rsqrt(ms + RMS_EPS) * norm_g
    return on * (gate * jax.nn.sigmoid(gate))


def _gla(proj_f, proj_b, proj_s, w_lr, b_lr, norm_g, B, S, cols):
    H = GLA_HEADS
    G = GLA_GROUP
    DK = w_lr.shape[1] // H
    DV = norm_g.shape[1]
    C = min(GLA_CHUNK, S)
    NC = S // C
    WK, WV = G * DK, G * DV
    assert S % C == 0 and C % (2 * SUBLANES) == 0 and H % G == 0
    assert all(cols[n] % WK == 0 for n in ("gq", "gk")) and all(cols[n] % WV == 0 for n in ("gr", "gv"))
    qo, ko, ro, vo = (cols["gq"] // WK, cols["gk"] // WK, cols["gr"] // WV, cols["gv"] // WV)
    rows = lambda b, h, c: b * NC + c
    blocks = (2 * _nbytes((C, WK), F32) + _nbytes((C, WV), BF16) + _nbytes((C, WV), F32)
              + _nbytes((C, LANES), F32) + _nbytes((LANES, WK), BF16) + _nbytes((C, WV), BF16))
    return pl.pallas_call(
        functools.partial(_gla_body, C=C, DK=DK, DV=DV, G=G),
        grid=(B, H // G, NC),
        in_specs=[
            pl.BlockSpec((C, WK), lambda b, h, c: (rows(b, h, c), qo + h)),
            pl.BlockSpec((C, WK), lambda b, h, c: (rows(b, h, c), ko + h)),
            pl.BlockSpec((C, WV), lambda b, h, c: (rows(b, h, c), vo + h)),
            pl.BlockSpec((C, WV), lambda b, h, c: (rows(b, h, c), ro + h)),
            pl.BlockSpec((C, LANES), lambda b, h, c: (rows(b, h, c), 0)),
            pl.BlockSpec((LANES, WK), lambda b, h, c: (0, h)),
            pl.BlockSpec((1, WK), lambda b, h, c: (0, h)),
            pl.BlockSpec((1, DV), lambda b, h, c: (0, 0)),
        ],
        out_specs=pl.BlockSpec((C, WV), lambda b, h, c: (rows(b, h, c), h)),
        out_shape=jax.ShapeDtypeStruct((B * S, H * DV), BF16),
        scratch_shapes=[pltpu.VMEM((G, DV, DK), F32)],
        compiler_params=pltpu.CompilerParams(
            dimension_semantics=("parallel", "parallel", "arbitrary"),
            vmem_limit_bytes=_vmem_limit(blocks, G * (12 << 20))),
        name="gla_chunked",
    )(proj_f, proj_f, proj_b, proj_f, proj_s, w_lr, b_lr, norm_g)


def _fox_body(q_ref, k_ref, v_ref, c_ref, o_ref, m_ref, l_ref, acc_ref, cqb_ref, *, T, G):
    qi = pl.program_id(2)
    lane_tiles = T // LANES
    for g in range(G):
        cqb_ref[g] = jnp.broadcast_to(c_ref[g, qi] * LOG2E, (LANES, T)).T
    m_ref[...] = jnp.full_like(m_ref, -jnp.inf)
    l_ref[...] = jnp.zeros_like(l_ref)
    acc_ref[...] = jnp.zeros_like(acc_ref)

    def qk(ki):
        start = pl.multiple_of(ki * T, T)
        return [lax.dot_general(
            q_ref[:, g * LANES:(g + 1) * LANES], k_ref[pl.ds(start, T), g * LANES:(g + 1) * LANES],
            (((1,), (1,)), ((), ())), preferred_element_type=F32) for g in range(G)]

    def softmax_pv(ki, scores, on_diagonal):
        start = pl.multiple_of(ki * T, T)
        for g in range(G):
            head = slice(g * LANES, (g + 1) * LANES)
            t = scores[g] - c_ref[g, ki] * LOG2E
            if on_diagonal:
                ti = lax.broadcasted_iota(jnp.int32, (T, T), 0)
                si = lax.broadcasted_iota(jnp.int32, (T, T), 1)
                t = jnp.where(si <= ti, t, -jnp.inf)
            cq = cqb_ref[g]
            m_prev = m_ref[g]
            m_new = jnp.maximum(m_prev, cq + jnp.max(t, axis=-1, keepdims=True))
            p = jnp.exp2(t + jnp.concatenate([cq - m_new] * lane_tiles, axis=1))
            alpha = jnp.exp2(m_prev - m_new)
            l_ref[g] = alpha * l_ref[g] + jnp.sum(p, axis=-1, keepdims=True)
            acc_ref[g] = alpha * acc_ref[g] + jnp.dot(
                p.astype(BF16), v_ref[pl.ds(start, T), head], preferred_element_type=F32)
            m_ref[g] = m_new

    def pair(kp, carry):
        sa = qk(2 * kp)
        sb = qk(2 * kp + 1)
        softmax_pv(2 * kp, sa, False)
        softmax_pv(2 * kp + 1, sb, False)
        return carry

    lax.fori_loop(0, qi // 2, pair, 0)

    @pl.when(qi % 2 == 1)
    def _():
        softmax_pv(qi - 1, qk(qi - 1), False)

    softmax_pv(qi, qk(qi), True)
    for g in range(G):
        o_ref[:, g * LANES:(g + 1) * LANES] = (acc_ref[g] / l_ref[g]).astype(o_ref.dtype)


def _fox(proj_b, c, B, S, cols, dh):
    H = FOX_HEADS
    G = FOX_GROUP
    T = min(FOX_TILE, S)
    NT = S // T
    W = G * dh
    assert S % T == 0 and H % G == 0 and dh == LANES
    qo, ko, vo = cols["fq"] // W, cols["fk"] // W, cols["fv"] // W
    assert all(cols[n] % W == 0 for n in ("fq", "fk", "fv"))
    c_row = c.reshape(B, H, NT, 1, T)
    blocks = (2 * _nbytes((T, W), BF16) + 2 * _nbytes((S, W), BF16)
              + G * _nbytes((NT, SUBLANES, T), F32))
    return pl.pallas_call(
        functools.partial(_fox_body, T=T, G=G),
        grid=(B, H // G, NT),
        in_specs=[
            pl.BlockSpec((T, W), lambda b, h, qi: (b * NT + qi, qo + h)),
            pl.BlockSpec((S, W), lambda b, h, qi: (b, ko + h)),
            pl.BlockSpec((S, W), lambda b, h, qi: (b, vo + h)),
            pl.BlockSpec((None, G, NT, 1, T), lambda b, h, qi: (b, h, 0, 0, 0)),
        ],
        out_specs=pl.BlockSpec((T, W), lambda b, h, qi: (b * NT + qi, h)),
        out_shape=jax.ShapeDtypeStruct((B * S, H * dh), BF16),
        scratch_shapes=[pltpu.VMEM((G, T, LANES), F32), pltpu.VMEM((G, T, LANES), F32),
                        pltpu.VMEM((G, T, dh), F32), pltpu.VMEM((G, T, LANES), F32)],
        compiler_params=pltpu.CompilerParams(
            dimension_semantics=("parallel", "parallel", "arbitrary"),
            vmem_limit_bytes=_vmem_limit(blocks, (4 * G + 8) * _nbytes((T, T), F32))),
        name="fox_attention",
    )(proj_b, proj_b, proj_b, c_row)


def _merge_body(og_ref, of_ref, wg_ref, wf_ref, ma_ref, mb_ref, o_ref, wgb_ref, wfb_ref):
    @pl.when(pl.program_id(1) == 0)
    def _():
        wgb_ref[...] = wg_ref[...].astype(BF16)
        wfb_ref[...] = wf_ref[...].astype(BF16)

    yg = jnp.dot(og_ref[...], wgb_ref[...], preferred_element_type=F32)
    yf = jnp.dot(of_ref[...], wfb_ref[...], preferred_element_type=F32)
    o_ref[...] = (jax.nn.sigmoid(ma_ref[...]) * yg
                  + jax.nn.sigmoid(mb_ref[...]) * yf).astype(o_ref.dtype)


def _merge(og, of, w_bg, w_bf, proj_f, cols, tm=1024, tn=512):
    M, KG = og.shape
    KF = of.shape[1]
    N = w_bg.shape[1]
    tm, tn = min(tm, M), min(tn, N)
    ao, bo = cols["ma"] // tn, cols["mb"] // tn
    assert cols["ma"] % tn == 0 and cols["mb"] % tn == 0
    blocks = (_nbytes((tm, KG + KF), BF16) + _nbytes((KG + KF, tn), F32)
              + 2 * _nbytes((tm, tn), F32) + _nbytes((tm, tn), BF16))
    return pl.pallas_call(
        _merge_body,
        grid=(N // tn, M // tm),
        scratch_shapes=[pltpu.VMEM((KG, tn), BF16), pltpu.VMEM((KF, tn), BF16)],
        in_specs=[
            pl.BlockSpec((tm, KG), lambda j, i: (i, 0)),
            pl.BlockSpec((tm, KF), lambda j, i: (i, 0)),
            pl.BlockSpec((KG, tn), lambda j, i: (0, j)),
            pl.BlockSpec((KF, tn), lambda j, i: (0, j)),
            pl.BlockSpec((tm, tn), lambda j, i: (i, ao + j)),
            pl.BlockSpec((tm, tn), lambda j, i: (i, bo + j)),
        ],
        out_specs=pl.BlockSpec((tm, tn), lambda j, i: (i, j)),
        out_shape=jax.ShapeDtypeStruct((M, N), BF16),
        compiler_params=pltpu.CompilerParams(
            dimension_semantics=("parallel", "arbitrary"),
            vmem_limit_bytes=_vmem_limit(
                blocks, _nbytes((KG + KF, tn), BF16) + 3 * _nbytes((tm, tn), F32))),
        name="branch_merge",
    )(og, of, w_bg, w_bf, proj_f, proj_f)


def _proj_ln_body(a_ref, w_ref, x_ref, g_ref, b_ref, o_ref, ob_ref, *, alpha):
    y = alpha * x_ref[...] + jnp.dot(a_ref[...], w_ref[...], preferred_element_type=F32)
    out = _layer_norm(y, g_ref[...], b_ref[...])
    o_ref[...] = out
    ob_ref[...] = out.astype(BF16)


def _proj_ln(a, w, resid, g, b, alpha, tm=512):
    M, K = a.shape
    N = w.shape[1]
    tm = min(tm, M)
    blocks = (_nbytes((tm, K), BF16) + _nbytes((K, N), BF16) + 2 * _nbytes((tm, N), F32)
              + _nbytes((tm, N), BF16))
    return pl.pallas_call(
        functools.partial(_proj_ln_body, alpha=alpha),
        grid=(M // tm,),
        in_specs=[
            pl.BlockSpec((tm, K), lambda i: (i, 0)),
            pl.BlockSpec((K, N), lambda i: (0, 0)),
            pl.BlockSpec((tm, N), lambda i: (i, 0)),
            pl.BlockSpec((1, N), lambda i: (0, 0)),
            pl.BlockSpec((1, N), lambda i: (0, 0)),
        ],
        out_specs=[pl.BlockSpec((tm, N), lambda i: (i, 0)),
                   pl.BlockSpec((tm, N), lambda i: (i, 0))],
        out_shape=[jax.ShapeDtypeStruct((M, N), F32), jax.ShapeDtypeStruct((M, N), BF16)],
        compiler_params=pltpu.CompilerParams(
            dimension_semantics=("parallel",),
            vmem_limit_bytes=_vmem_limit(blocks, 3 * _nbytes((tm, N), F32))),
        name="out_proj_layernorm",
    )(a, w, resid, g, b)


def _ffn_up_body(x_ref, wg_ref, wu_ref, cw_ref, cb_ref, h_ref, wgb_ref, wub_ref, gbuf_ref,
                 *, tm, tiles_per_seq, pr, pc):
    i = pl.program_id(1)
    HALO = SUBLANES
    tn = h_ref.shape[1]

    @pl.when(i == 0)
    def _():
        wgb_ref[...] = wg_ref[...].astype(BF16)
        wub_ref[...] = wu_ref[...].astype(BF16)

    @pl.when(i % tiles_per_seq == 0)
    def _():
        gbuf_ref[0:HALO, :] = jnp.zeros((HALO, tn), F32)

    cw = cw_ref[...]
    cb = cb_ref[...]
    parts = [(r, c) for c in range(tn // pc) for r in range(tm // pr)]

    def matmuls(r, c):
        x = x_ref[r * pr:(r + 1) * pr, :]
        cols = slice(c * pc, (c + 1) * pc)
        g = jnp.dot(x, wgb_ref[:, cols], preferred_element_type=F32)
        u = jnp.dot(x, wub_ref[:, cols], preferred_element_type=F32)
        gbuf_ref[HALO + r * pr:HALO + (r + 1) * pr, cols] = g
        return g, u

    def finish(r, c, g, u):
        cols = slice(c * pc, (c + 1) * pc)
        y = cb[:, cols] + cw[CONV_W - 1:CONV_W, cols] * g
        for j in range(CONV_W - 1):
            off = HALO - (CONV_W - 1) + j + r * pr
            y = y + cw[j:j + 1, cols] * gbuf_ref[off:off + pr, cols]
        h_ref[r * pr:(r + 1) * pr, cols] = (jax.nn.gelu(y, approximate=True) * u).astype(h_ref.dtype)

    pending = None
    for r, c in parts:
        gu = matmuls(r, c)
        if pending is not None:
            finish(*pending)
        pending = (r, c) + gu
    finish(*pending)
    gbuf_ref[0:HALO, :] = gbuf_ref[tm:tm + HALO, :]


def _ffn_up(xb, w_gate, w_up, conv_w, conv_b, S, tm=1024, tn=512, pr=512, pc=256):
    M, K = xb.shape
    N = w_gate.shape[1]
    tm, tn = min(tm, S), min(tn, N)
    pr, pc = min(pr, tm), min(pc, tn)
    assert S % tm == 0 and N % tn == 0 and tm % pr == 0 and tn % pc == 0
    blocks = (_nbytes((tm, K), BF16) + 2 * _nbytes((K, tn), F32) + _nbytes((tm, tn), BF16)
              + _nbytes((8, tn), F32) * 2)
    scratch = 2 * _nbytes((K, tn), BF16) + _nbytes((tm + SUBLANES, tn), F32)
    return pl.pallas_call(
        functools.partial(_ffn_up_body, tm=tm, tiles_per_seq=S // tm, pr=pr, pc=pc),
        grid=(N // tn, M // tm),
        in_specs=[
            pl.BlockSpec((tm, K), lambda j, i: (i, 0)),
            pl.BlockSpec((K, tn), lambda j, i: (0, j)),
            pl.BlockSpec((K, tn), lambda j, i: (0, j)),
            pl.BlockSpec((CONV_W, tn), lambda j, i: (0, j)),
            pl.BlockSpec((1, tn), lambda j, i: (0, j)),
        ],
        out_specs=pl.BlockSpec((tm, tn), lambda j, i: (i, j)),
        out_shape=jax.ShapeDtypeStruct((M, N), BF16),
        scratch_shapes=[pltpu.VMEM((K, tn), BF16), pltpu.VMEM((K, tn), BF16),
                        pltpu.VMEM((tm + SUBLANES, tn), F32)],
        compiler_params=pltpu.CompilerParams(
            dimension_semantics=("parallel", "arbitrary"),
            vmem_limit_bytes=_vmem_limit(blocks, scratch + 8 * _nbytes((pr, pc), F32))),
        name="ffn_gate_up",
    )(xb, w_gate, w_up, conv_w, conv_b)


def _ffn_down_body(h_ref, w_ref, x_ref, g_ref, b_ref, o_ref, *, alpha):
    kk = pl.program_id(1)

    @pl.when(kk == 0)
    def _():
        o_ref[...] = alpha * x_ref[...]

    o_ref[...] += jnp.dot(h_ref[...], w_ref[...], preferred_element_type=F32)

    @pl.when(kk == pl.num_programs(1) - 1)
    def _():
        o_ref[...] = _layer_norm(o_ref[...], g_ref[...], b_ref[...])


def _ffn_down(h, w_down, resid, g, b, alpha, tm=512, tk=2816):
    M, K = h.shape
    N = w_down.shape[1]
    tm, tk = min(tm, M), min(tk, K)
    assert M % tm == 0 and K % tk == 0
    blocks = (_nbytes((tm, tk), BF16) + _nbytes((tk, N), BF16) + 2 * _nbytes((tm, N), F32))
    return pl.pallas_call(
        functools.partial(_ffn_down_body, alpha=alpha),
        grid=(M // tm, K // tk),
        in_specs=[
            pl.BlockSpec((tm, tk), lambda i, kk: (i, kk)),
            pl.BlockSpec((tk, N), lambda i, kk: (kk, 0)),
            pl.BlockSpec((tm, N), lambda i, kk: (i, 0)),
            pl.BlockSpec((1, N), lambda i, kk: (0, 0)),
            pl.BlockSpec((1, N), lambda i, kk: (0, 0)),
        ],
        out_specs=pl.BlockSpec((tm, N), lambda i, kk: (i, 0)),
        out_shape=jax.ShapeDtypeStruct((M, N), F32),
        compiler_params=pltpu.CompilerParams(
            dimension_semantics=("parallel", "arbitrary"),
            vmem_limit_bytes=_vmem_limit(blocks, _nbytes((tm, N), F32))),
        name="ffn_down_layernorm",
    )(h, w_down, resid, g, b)


def _ple_body(x_ref, wg_ref, p_ref, wp_ref, o_ref, xb_ref, pb_ref):
    j = pl.program_id(1)
    tn = o_ref.shape[1]

    @pl.when(j == 0)
    def _():
        xb_ref[...] = x_ref[...].astype(BF16)
        pb_ref[...] = p_ref[...].astype(BF16)

    gate = jax.nn.sigmoid(jnp.dot(xb_ref[...], wg_ref[...], preferred_element_type=F32))
    emb = jnp.dot(pb_ref[...], wp_ref[...], preferred_element_type=F32)
    o_ref[...] = x_ref[:, pl.ds(pl.multiple_of(j * tn, tn), tn)] + gate * emb


def _ple(x, p, w_gate, w_proj, tm=1024, tn=1024):
    M, K = x.shape
    N = w_gate.shape[1]
    P = p.shape[1]
    tm, tn = min(tm, M), min(tn, N)
    assert K == N and M % tm == 0 and N % tn == 0
    blocks = (_nbytes((tm, K), F32) + _nbytes((K, tn), BF16) + _nbytes((tm, P), F32)
              + _nbytes((P, tn), BF16) + _nbytes((tm, tn), F32))
    scratch = _nbytes((tm, K), BF16) + _nbytes((tm, P), BF16)
    return pl.pallas_call(
        _ple_body,
        grid=(M // tm, N // tn),
        in_specs=[
            pl.BlockSpec((tm, K), lambda i, j: (i, 0)),
            pl.BlockSpec((K, tn), lambda i, j: (0, j)),
            pl.BlockSpec((tm, P), lambda i, j: (i, 0)),
            pl.BlockSpec((P, tn), lambda i, j: (0, j)),
        ],
        out_specs=pl.BlockSpec((tm, tn), lambda i, j: (i, j)),
        out_shape=jax.ShapeDtypeStruct((M, N), F32),
        scratch_shapes=[pltpu.VMEM((tm, K), BF16), pltpu.VMEM((tm, P), BF16)],
        compiler_params=pltpu.CompilerParams(
            dimension_semantics=("parallel", "arbitrary"),
            vmem_limit_bytes=_vmem_limit(blocks, scratch + 3 * _nbytes((tm, tn), F32))),
        name="ple_gate",
    )(x, w_gate, p, w_proj)


def _cast_small_body(x_ref, w_ref, xb_ref, o_ref):
    xb = x_ref[...].astype(BF16)
    xb_ref[...] = xb
    o_ref[...] = lax.dot_general(xb, w_ref[...], (((1,), (1,)), ((), ())),
                                 preferred_element_type=F32)


def _cast_and_small_proj(x, w_small, tm=1024):
    M, K = x.shape
    ns = w_small.shape[0]
    tm = min(tm, M)
    assert M % tm == 0
    blocks = _nbytes((tm, K), F32) + _nbytes((ns, K), BF16) + _nbytes((tm, K), BF16) \
        + _nbytes((tm, ns), F32)
    return pl.pallas_call(
        _cast_small_body,
        grid=(M // tm,),
        in_specs=[pl.BlockSpec((tm, K), lambda i: (i, 0)),
                  pl.BlockSpec((ns, K), lambda i: (0, 0))],
        out_specs=[pl.BlockSpec((tm, K), lambda i: (i, 0)),
                   pl.BlockSpec((tm, ns), lambda i: (i, 0))],
        out_shape=[jax.ShapeDtypeStruct((M, K), BF16), jax.ShapeDtypeStruct((M, ns), F32)],
        compiler_params=pltpu.CompilerParams(
            dimension_semantics=("parallel",),
            vmem_limit_bytes=_vmem_limit(blocks, _nbytes((tm, K), BF16))),
        name="in_proj_small",
    )(x, w_small)


def _in_proj(x, w_in, layer, rank):
    M, D = x.shape
    gla_qk = D // 2
    fox_w = FOX_HEADS * LANES
    names = ("gq", "gk", "gv", "gr", "glr", "fq", "fk", "fv", "ff", "ma", "mb")
    widths = (gla_qk, gla_qk, D, D, rank, fox_w, fox_w, fox_w, FOX_HEADS, D, D)
    assert sum(widths) == w_in.shape[2]
    width, offset, off = {}, {}, 0
    for n, wd in zip(names, widths):
        width[n], offset[n] = wd, off
        off += wd
    w_t = jnp.swapaxes(w_in, 1, 2)

    out_scale = {"gq": (gla_qk // GLA_HEADS) ** -0.5, "fq": LANES ** -0.5 * LOG2E}
    T = IN_PROJ_TILE

    def project(order, out_dtype, name):
        src_rows, scales, cols, o = [], [], {}, 0
        for n in order:
            assert width[n] % T == 0
            cols[n] = o
            o += width[n]
            src_rows += [offset[n] + t for t in range(0, width[n], T)]
            scales.append(jnp.full((1, width[n]), out_scale.get(n, 1.0), F32))
        out = _matmul_nt(xb, w_t, layer, src_rows, jnp.concatenate(scales, axis=1),
                         out_dtype, 1024, T, name)
        return out, cols

    small, cols_s = [], {}
    for t, n in enumerate(("glr", "ff")):
        wseg = w_t[layer, offset[n]:offset[n] + width[n], :]
        small.append(jnp.pad(wseg, ((0, LANES - width[n]), (0, 0))))
        cols_s[n] = t * LANES
    w_small = jnp.concatenate(small, axis=0).astype(BF16)
    xb, proj_s = _cast_and_small_proj(x, w_small)

    proj_b = project(("gv", "fq", "fk", "fv"), BF16, "in_proj_bf16")
    proj_f = project(("gq", "gk", "gr", "ma", "mb"), F32, "in_proj_f32")
    return proj_b, proj_f, (proj_s, cols_s)


def kernel(x, p, w_in, w_gla_lr, b_gla_lr, gla_norm_g, b_forget, w_branch_gla, w_branch_fox,
           w_out, ln1_g, ln1_b, w_gate, w_up, conv_w, conv_b, w_down, ln2_g, ln2_b,
           w_ple_gate, w_ple_proj):
    B, S, D = x.shape
    depth = w_in.shape[0]
    alpha = (2 * depth) ** 0.25
    M = B * S
    xf = x.reshape(M, D)
    for i in range(depth):
        rank = w_gla_lr.shape[1]
        (proj_b, cols_b), (proj_f, cols_f), (proj_s, cols_s) = _in_proj(xf, w_in, i, rank)

        ff = proj_s[:, cols_s["ff"]:cols_s["ff"] + FOX_HEADS]
        c = _forget_cumsum(ff, b_forget[i], B, S)
        of = _fox(proj_b, c, B, S, cols_b, LANES)

        w_lr = jnp.pad(w_gla_lr[i], ((0, LANES - rank), (0, 0))).astype(BF16)
        og = _gla(proj_f, proj_b, proj_s, w_lr, b_gla_lr[i].reshape(1, -1),
                  gla_norm_g[i].reshape(1, -1), B, S, {**cols_f, **cols_b})

        merged = _merge(og, of, w_branch_gla[i], w_branch_fox[i], proj_f, cols_f)
        x1, x1b = _proj_ln(merged, w_out[i].astype(BF16), xf, ln1_g[i].reshape(1, -1),
                           ln1_b[i].reshape(1, -1), alpha)
        h = _ffn_up(x1b, w_gate[i], w_up[i], conv_w[i], conv_b[i].reshape(1, -1), S)
        x2 = _ffn_down(h, w_down[i].astype(BF16), x1, ln2_g[i].reshape(1, -1),
                       ln2_b[i].reshape(1, -1), alpha)
        xf = _ple(x2, p[i].reshape(M, -1), w_ple_gate[i].astype(BF16),
                  w_ple_proj[i].astype(BF16))
    return xf.reshape(B, S, D)
```

```python
import functools

import jax
import jax.numpy as jnp
from jax import lax
from jax.experimental import pallas as pl
from jax.experimental.pallas import tpu as pltpu

F32 = jnp.float32
BF16 = jnp.bfloat16

GLA_HEADS = 4
GLA_TAU = 16.0
FOX_HEADS = 8
CONV_W = 3
LN_EPS = 1e-5
RMS_EPS = 1e-6
LOG2E = 1.4426950408889634

LANES = 128
SUBLANES = 8
MXU_COLS = 256
VMEM_BYTES_V7X = 64 * 1024 * 1024

IN_PROJ_TILE = 1024
GLA_CHUNK = 128
GLA_GROUP = 4
FOX_TILE = 512
FOX_GROUP = 4


def _vmem_limit(block_bytes, extra_bytes=0):
    est = 2 * block_bytes + extra_bytes + (4 << 20)
    return int(min(max(est, 16 << 20), VMEM_BYTES_V7X - (8 << 20)))


def _nbytes(shape, dtype):
    n = 1
    for s in shape:
        n *= s
    return n * jnp.dtype(dtype).itemsize


def _log_sigmoid(z):
    return jnp.minimum(z, 0.0) - jnp.log(1.0 + jnp.exp(-jnp.abs(z)))


def _layer_norm(y, g, b):
    mu = jnp.mean(y, axis=-1, keepdims=True)
    d = y - mu
    var = jnp.mean(d * d, axis=-1, keepdims=True)
    return d * lax.rsqrt(var + LN_EPS) * g + b


def _mm_nt_body(rows_ref, a_ref, wt_ref, cs_ref, o_ref, wb_ref):
    del rows_ref

    @pl.when(pl.program_id(1) == 0)
    def _():
        wb_ref[...] = wt_ref[0].astype(BF16)

    a = a_ref[...]
    for c in range(0, o_ref.shape[1], MXU_COLS):
        cols = slice(c, c + MXU_COLS)
        acc = lax.dot_general(a, wb_ref[cols, :], (((1,), (1,)), ((), ())),
                              preferred_element_type=F32)
        o_ref[:, cols] = (acc * cs_ref[:, cols]).astype(o_ref.dtype)


def _matmul_nt(a, w_t, layer, src_rows, col_scale, out_dtype, tm, tn, name):
    M, K = a.shape
    nt = len(src_rows)
    tm = min(tm, M)
    assert M % tm == 0 and col_scale.shape == (1, nt * tn)
    assert all(r % SUBLANES == 0 and r + tn <= w_t.shape[1] for r in src_rows)
    grid_spec = pltpu.PrefetchScalarGridSpec(
        num_scalar_prefetch=1,
        grid=(nt, M // tm),
        in_specs=[pl.BlockSpec((tm, K), lambda j, i, rows: (i, 0)),
                  pl.BlockSpec((pl.Element(1), pl.Element(tn), pl.Element(K)),
                               lambda j, i, rows: (layer, pl.multiple_of(rows[j], SUBLANES), 0)),
                  pl.BlockSpec((1, tn), lambda j, i, rows: (0, j))],
        out_specs=pl.BlockSpec((tm, tn), lambda j, i, rows: (i, j)),
        scratch_shapes=[pltpu.VMEM((tn, K), BF16)],
    )
    blocks = (_nbytes((tm, K), a.dtype) + _nbytes((tn, K), w_t.dtype)
              + _nbytes((tm, tn), out_dtype))
    return pl.pallas_call(
        _mm_nt_body,
        grid_spec=grid_spec,
        out_shape=jax.ShapeDtypeStruct((M, nt * tn), out_dtype),
        compiler_params=pltpu.CompilerParams(
            dimension_semantics=("parallel", "arbitrary"),
            vmem_limit_bytes=_vmem_limit(blocks, _nbytes((tn, K), BF16) + _nbytes((tm, tn), F32))),
        name=name,
    )(jnp.asarray(src_rows, jnp.int32), a, w_t, col_scale)


def _fcum_body(ff_ref, bias_ref, c_ref, *, groups):
    z = ff_ref[...] + bias_ref[...]
    x = _log_sigmoid(z)
    rows, lanes = x.shape
    lane = lax.broadcasted_iota(jnp.int32, x.shape, 1)
    s = 1
    while s < lanes:
        x = x + jnp.where(lane >= s, pltpu.roll(x, s, axis=1), 0.0)
        s *= 2
    tot = jnp.broadcast_to(x[:, lanes - 1:lanes], x.shape)
    grp = lax.broadcasted_iota(jnp.int32, x.shape, 0) % groups
    inc = tot
    s = 1
    while s < groups:
        inc = inc + jnp.where(grp >= s, pltpu.roll(inc, s, axis=0), 0.0)
        s *= 2
    c_ref[...] = x + (inc - tot)


def _forget_cumsum(ff, b_forget, B, S):
    H = ff.shape[1]
    groups = S // LANES
    ff_t = ff.reshape(B, S, H).transpose(0, 2, 1).reshape(B * H * groups, LANES)
    bias = jnp.broadcast_to(b_forget.reshape(1, H, 1, 1),
                            (B, H, groups, LANES)).reshape(B * H * groups, LANES)
    c = pl.pallas_call(
        functools.partial(_fcum_body, groups=groups),
        out_shape=jax.ShapeDtypeStruct(ff_t.shape, F32),
        name="fox_forget_cumsum",
    )(ff_t, bias)
    return c.reshape(B, H, S)


def _gla_body(q_ref, k_ref, v_ref, r_ref, lr_ref, wlr_ref, blr_ref, g_ref,
              o_ref, st_ref, *, C, DK, DV, G):
    c = pl.program_id(2)

    @pl.when(c == 0)
    def _():
        st_ref[...] = jnp.zeros_like(st_ref)

    lr = lr_ref[...].astype(BF16)
    heads = []
    for g in range(G):
        kcols = slice(g * DK, (g + 1) * DK)
        vcols = slice(g * DV, (g + 1) * DV)
        heads.append(_gla_head(
            q_ref[:, kcols], k_ref[:, kcols], v_ref[:, vcols], r_ref[:, vcols], lr,
            wlr_ref[:, kcols], blr_ref[:, kcols], g_ref[...], st_ref.at[g],
            C=C, DK=DK, DV=DV))
    live = list(range(G))
    while live:
        for g in list(live):
            try:
                next(heads[g])
            except StopIteration as done:
                o_ref[:, g * DV:(g + 1) * DV] = done.value.astype(o_ref.dtype)
                live.remove(g)


def _gla_head(q, k, v, gate, lr, wlr, blr, norm_g, st_ref, *, C, DK, DV):
    z = jnp.dot(lr, wlr, preferred_element_type=F32) + blr
    yield
    la = _log_sigmoid(z) * (LOG2E / GLA_TAU)

    row = lax.broadcasted_iota(jnp.int32, (C, DK), 0)
    ti = lax.broadcasted_iota(jnp.int32, (C, C), 0)
    si = lax.broadcasted_iota(jnp.int32, (C, C), 1)
    tril = jnp.where(si <= ti, 1.0, 0.0).astype(BF16)
    hi = la.astype(BF16)
    rest = la - hi.astype(F32)
    mid = rest.astype(BF16)
    lo = (rest - mid.astype(F32)).astype(BF16)
    yield
    b = (jnp.dot(tril, hi, preferred_element_type=F32)
         + jnp.dot(tril, mid, preferred_element_type=F32)
         + jnp.dot(tril, lo, preferred_element_type=F32))
    yield

    st = st_ref[...]
    qe = (q * jnp.exp2(b)).astype(BF16)
    inter = lax.dot_general(qe, st.astype(BF16), (((1,), (1,)), ((), ())),
                            preferred_element_type=F32)
    blast = b[C - 1:C, :]
    kd = (k * jnp.exp2(blast - b)).astype(BF16)
    yield

    pair =jnp.where(ti > si, ti ^ si, 0)
    attn = jnp.where(ti == si, jnp.sum(q * k, axis=-1, keepdims=True), 0.0)
    NB = C // SUBLANES
    b3 = b.reshape(NB, SUBLANES, DK)
    sub3 = lax.broadcasted_iota(jnp.int32, (NB, SUBLANES, DK), 1)
    m = C // 2
    while m >= 1:
        blk = 2 * m
        in_b = (row & (blk - 1)) >= m
        if m == 1:
            e = jnp.exp2(la)
        else:
            if blk >= SUBLANES:
                nblk = C // blk
                r = jnp.broadcast_to(b.reshape(nblk, blk, DK)[:, m - 1:m, :],
                                     (nblk, blk, DK)).reshape(C, DK)
            else:
                r3 = None
                for start in range(0, SUBLANES, blk):
                    cand = jnp.broadcast_to(b3[:, start + m - 1:start + m, :], b3.shape)
                    r3 = cand if r3 is None else jnp.where(sub3 >= start, cand, r3)
                r = r3.reshape(C, DK)
            e = jnp.exp2(-jnp.abs(b - r))
        x = jnp.where(in_b, q, k)
        x = (jnp.where(in_b, x * e, x) if m == 1 else x * e).astype(BF16)
        a = lax.dot_general(x, x, (((1,), (1,)), ((), ())),
                            preferred_element_type=F32)
        attn = jnp.where((pair >> (m.bit_length() - 1)) == 1, a, attn)
        m //= 2
        yield

    o = inter + jnp.dot(attn.astype(BF16), v, preferred_element_type=F32)
    yield

    upd = lax.dot_general(v, kd, (((0,), (0,)), ((), ())),
                          preferred_element_type=F32)
    st_ref[...] = st * jnp.exp2(blast) + upd
    yield

    ms = jnp.mean(o * o, axis=-1, keepdims=True)
    on = o * lax.rsqrt(ms + RMS_EPS) * norm_g
    return on * (gate * jax.nn.sigmoid(gate))


def _gla(proj_f, proj_b, proj_s, w_lr, b_lr, norm_g, B, S, cols):
    H = GLA_HEADS
    G = GLA_GROUP
    DK = w_lr.shape[1] // H
    DV = norm_g.shape[1]
    C = min(GLA_CHUNK, S)
    NC = S // C
    WK, WV = G * DK, G * DV
    assert S % C == 0 and C % (2 * SUBLANES) == 0 and H % G == 0
    assert all(cols[n] % WK == 0 for n in ("gq", "gk")) and all(cols[n] % WV == 0 for n in ("gr", "gv"))
    qo, ko, ro, vo = (cols["gq"] // WK, cols["gk"] // WK, cols["gr"] // WV, cols["gv"] // WV)
    rows = lambda b, h, c: b * NC + c
    blocks = (2 * _nbytes((C, WK), F32) + _nbytes((C, WV), BF16) + _nbytes((C, WV), F32)
              + _nbytes((C, LANES), F32) + _nbytes((LANES, WK), BF16) + _nbytes((C, WV), BF16))
    return pl.pallas_call(
        functools.partial(_gla_body, C=C, DK=DK, DV=DV, G=G),
        grid=(B, H // G, NC),
        in_specs=[
            pl.BlockSpec((C, WK), lambda b, h, c: (rows(b, h, c), qo + h)),
            pl.BlockSpec((C, WK), lambda b, h, c: (rows(b, h, c), ko + h)),
            pl.BlockSpec((C, WV), lambda b, h, c: (rows(b, h, c), vo + h)),
            pl.BlockSpec((C, WV), lambda b, h, c: (rows(b, h, c), ro + h)),
            pl.BlockSpec((C, LANES), lambda b, h, c: (rows(b, h, c), 0)),
            pl.BlockSpec((LANES, WK), lambda b, h, c: (0, h)),
            pl.BlockSpec((1, WK), lambda b, h, c: (0, h)),
            pl.BlockSpec((1, DV), lambda b, h, c: (0, 0)),
        ],
        out_specs=pl.BlockSpec((C, WV), lambda b, h, c: (rows(b, h, c), h)),
        out_shape=jax.ShapeDtypeStruct((B * S, H * DV), BF16),
        scratch_shapes=[pltpu.VMEM((G, DV, DK), F32)],
        compiler_params=pltpu.CompilerParams(
            dimension_semantics=("parallel", "parallel", "arbitrary"),
            vmem_limit_bytes=_vmem_limit(blocks, G * (12 << 20))),
        name="gla_chunked",
    )(proj_f, proj_f, proj_b, proj_f, proj_s, w_lr, b_lr, norm_g)


def _fox_body(q_ref, k_ref, v_ref, c_ref, o_ref, m_ref, l_ref, acc_ref, cqb_ref, *, T, G):
    qi = pl.program_id(2)
    lane_tiles = T // LANES
    for g in range(G):
        cqb_ref[g] = jnp.broadcast_to(c_ref[g, qi] * LOG2E, (LANES, T)).T
    m_ref[...] = jnp.full_like(m_ref, -jnp.inf)
    l_ref[...] = jnp.zeros_like(l_ref)
    acc_ref[...] = jnp.zeros_like(acc_ref)

    def qk(ki):
        start = pl.multiple_of(ki * T, T)
        return [lax.dot_general(
            q_ref[:, g * LANES:(g + 1) * LANES], k_ref[pl.ds(start, T), g * LANES:(g + 1) * LANES],
            (((1,), (1,)), ((), ())), preferred_element_type=F32) for g in range(G)]

    def softmax_pv(ki, scores, on_diagonal):
        start = pl.multiple_of(ki * T, T)
        for g in range(G):
            head = slice(g * LANES, (g + 1) * LANES)
            t = scores[g] - c_ref[g, ki] * LOG2E
            if on_diagonal:
                ti = lax.broadcasted_iota(jnp.int32, (T, T), 0)
                si = lax.broadcasted_iota(jnp.int32, (T, T), 1)
                t = jnp.where(si <= ti, t, -jnp.inf)
            cq = cqb_ref[g]
            m_prev = m_ref[g]
            m_new = jnp.maximum(m_prev, cq + jnp.max(t, axis=-1, keepdims=True))
            p = jnp.exp2(t + jnp.concatenate([cq - m_new] * lane_tiles, axis=1))
            alpha = jnp.exp2(m_prev - m_new)
            l_ref[g] = alpha * l_ref[g] + jnp.sum(p, axis=-1, keepdims=True)
            acc_ref[g] = alpha * acc_ref[g] + jnp.dot(
                p.astype(BF16), v_ref[pl.ds(start, T), head], preferred_element_type=F32)
            m_ref[g] = m_new

    def pair(kp, carry):
        sa = qk(2 * kp)
        sb = qk(2 * kp + 1)
        softmax_pv(2 * kp, sa, False)
        softmax_pv(2 * kp + 1, sb, False)
        return carry

    lax.fori_loop(0, qi // 2, pair, 0)

    @pl.when(qi % 2 == 1)
    def _():
        softmax_pv(qi - 1, qk(qi - 1), False)

    softmax_pv(qi, qk(qi), True)
    for g in range(G):
        o_ref[:, g * LANES:(g + 1) * LANES] = (acc_ref[g] / l_ref[g]).astype(o_ref.dtype)


def _fox(proj_b, c, B, S, cols, dh):
    H = FOX_HEADS
    G = FOX_GROUP
    T = min(FOX_TILE, S)
    NT = S // T
    W = G * dh
    assert S % T == 0 and H % G == 0 and dh == LANES
    qo, ko, vo = cols["fq"] // W, cols["fk"] // W, cols["fv"] // W
    assert all(cols[n] % W == 0 for n in ("fq", "fk", "fv"))
    c_row = c.reshape(B, H, NT, 1, T)
    blocks = (2 * _nbytes((T, W), BF16) + 2 * _nbytes((S, W), BF16)
              + G * _nbytes((NT, SUBLANES, T), F32))
    return pl.pallas_call(
        functools.partial(_fox_body, T=T, G=G),
        grid=(B, H // G, NT),
        in_specs=[
            pl.BlockSpec((T, W), lambda b, h, qi: (b * NT + qi, qo + h)),
            pl.BlockSpec((S, W), lambda b, h, qi: (b, ko + h)),
            pl.BlockSpec((S, W), lambda b, h, qi: (b, vo + h)),
            pl.BlockSpec((None, G, NT, 1, T), lambda b, h, qi: (b, h, 0, 0, 0)),
        ],
        out_specs=pl.BlockSpec((T, W), lambda b, h, qi: (b * NT + qi, h)),
        out_shape=jax.ShapeDtypeStruct((B * S, H * dh), BF16),
        scratch_shapes=[pltpu.VMEM((G, T, LANES), F32), pltpu.VMEM((G, T, LANES), F32),
                        pltpu.VMEM((G, T, dh), F32), pltpu.VMEM((G, T, LANES), F32)],
        compiler_params=pltpu.CompilerParams(
            dimension_semantics=("parallel", "parallel", "arbitrary"),
            vmem_limit_bytes=_vmem_limit(blocks, (4 * G + 8) * _nbytes((T, T), F32))),
        name="fox_attention",
    )(proj_b, proj_b, proj_b, c_row)


def _merge_body(og_ref, of_ref, wg_ref, wf_ref, ma_ref, mb_ref, o_ref):
    yg = jnp.dot(og_ref[...], wg_ref[...], preferred_element_type=F32)
    yf = jnp.dot(of_ref[...], wf_ref[...], preferred_element_type=F32)
    o_ref[...] = (jax.nn.sigmoid(ma_ref[...]) * yg
                  + jax.nn.sigmoid(mb_ref[...]) * yf).astype(o_ref.dtype)


def _merge(og, of, w_bg, w_bf, proj_f, cols, tm=1024, tn=1024):
    M, KG = og.shape
    KF = of.shape[1]
    N = w_bg.shape[1]
    tm, tn = min(tm, M), min(tn, N)
    ao, bo = cols["ma"] // tn, cols["mb"] // tn
    assert cols["ma"] % tn == 0 and cols["mb"] % tn == 0
    blocks = (_nbytes((tm, KG + KF), BF16) + _nbytes((KG + KF, tn), BF16)
              + 2 * _nbytes((tm, tn), F32) + _nbytes((tm, tn), BF16))
    return pl.pallas_call(
        _merge_body,
        grid=(N // tn, M // tm),
        in_specs=[
            pl.BlockSpec((tm, KG), lambda j, i: (i, 0)),
            pl.BlockSpec((tm, KF), lambda j, i: (i, 0)),
            pl.BlockSpec((KG, tn), lambda j, i: (0, j)),
            pl.BlockSpec((KF, tn), lambda j, i: (0, j)),
            pl.BlockSpec((tm, tn), lambda j, i: (i, ao + j)),
            pl.BlockSpec((tm, tn), lambda j, i: (i, bo + j)),
        ],
        out_specs=pl.BlockSpec((tm, tn), lambda j, i: (i, j)),
        out_shape=jax.ShapeDtypeStruct((M, N), BF16),
        compiler_params=pltpu.CompilerParams(
            dimension_semantics=("parallel", "parallel"),
            vmem_limit_bytes=_vmem_limit(blocks, 3 * _nbytes((tm, tn), F32))),
        name="branch_merge",
    )(og, of, w_bg, w_bf, proj_f, proj_f)


def _proj_ln_body(a_ref, w_ref, x_ref, g_ref, b_ref, o_ref, ob_ref, *, alpha, parts):
    pr = o_ref.shape[0] // parts
    ys = [jnp.dot(a_ref[r * pr:(r + 1) * pr, :], w_ref[...], preferred_element_type=F32)
          for r in range(parts)]
    for r in range(parts):
        rows = slice(r * pr, (r + 1) * pr)
        out = _layer_norm(alpha * x_ref[rows, :] + ys[r], g_ref[...], b_ref[...])
        o_ref[rows, :] = out
        ob_ref[rows, :] = out.astype(BF16)


def _proj_ln(a, w, resid, g, b, alpha, tm=512):
    M, K = a.shape
    N = w.shape[1]
    tm = min(tm, M)
    blocks = (_nbytes((tm, K), BF16) + _nbytes((K, N), BF16) + 2 * _nbytes((tm, N), F32)
              + _nbytes((tm, N), BF16))
    return pl.pallas_call(
        functools.partial(_proj_ln_body, alpha=alpha, parts=2),
        grid=(M // tm,),
        in_specs=[
            pl.BlockSpec((tm, K), lambda i: (i, 0)),
            pl.BlockSpec((K, N), lambda i: (0, 0)),
            pl.BlockSpec((tm, N), lambda i: (i, 0)),
            pl.BlockSpec((1, N), lambda i: (0, 0)),
            pl.BlockSpec((1, N), lambda i: (0, 0)),
        ],
        out_specs=[pl.BlockSpec((tm, N), lambda i: (i, 0)),
                   pl.BlockSpec((tm, N), lambda i: (i, 0))],
        out_shape=[jax.ShapeDtypeStruct((M, N), F32), jax.ShapeDtypeStruct((M, N), BF16)],
        compiler_params=pltpu.CompilerParams(
            dimension_semantics=("parallel",),
            vmem_limit_bytes=_vmem_limit(blocks, 3 * _nbytes((tm, N), F32))),
        name="out_proj_layernorm",
    )(a, w, resid, g, b)


def _ffn_up_body(x_ref, wg_ref, wu_ref, cw_ref, cb_ref, h_ref, wgb_ref, wub_ref, gbuf_ref,
                 *, tm, tiles_per_seq, pr, pc):
    i = pl.program_id(1)
    HALO = SUBLANES
    tn = h_ref.shape[1]

    @pl.when(i == 0)
    def _():
        wgb_ref[...] = wg_ref[...].astype(BF16)
        wub_ref[...] = wu_ref[...].astype(BF16)

    @pl.when(i % tiles_per_seq == 0)
    def _():
        gbuf_ref[0:HALO, :] = jnp.zeros((HALO, tn), F32)

    cw = cw_ref[...]
    cb = cb_ref[...]
    parts = [(r, c) for c in range(tn // pc) for r in range(tm // pr)]

    def matmuls(r, c):
        x = x_ref[r * pr:(r + 1) * pr, :]
        cols = slice(c * pc, (c + 1) * pc)
        g = jnp.dot(x, wgb_ref[:, cols], preferred_element_type=F32)
        u = jnp.dot(x, wub_ref[:, cols], preferred_element_type=F32)
        gbuf_ref[HALO + r * pr:HALO + (r + 1) * pr, cols] = g
        return g, u

    def finish(r, c, g, u):
        cols = slice(c * pc, (c + 1) * pc)
        y = cb[:, cols] + cw[CONV_W - 1:CONV_W, cols] * g
        for j in range(CONV_W - 1):
            off = HALO - (CONV_W - 1) + j + r * pr
            y = y + cw[j:j + 1, cols] * gbuf_ref[off:off + pr, cols]
        h_ref[r * pr:(r + 1) * pr, cols] = (jax.nn.gelu(y, approximate=True) * u).astype(h_ref.dtype)

    pending = None
    for r, c in parts:
        gu = matmuls(r, c)
        if pending is not None:
            finish(*pending)
        pending = (r, c) + gu
    finish(*pending)
    gbuf_ref[0:HALO, :] = gbuf_ref[tm:tm + HALO, :]


def _ffn_up(xb, w_gate, w_up, conv_w, conv_b, S, tm=1024, tn=512, pr=512, pc=256):
    M, K = xb.shape
    N = w_gate.shape[1]
    tm, tn = min(tm, S), min(tn, N)
    pr, pc = min(pr, tm), min(pc, tn)
    assert S % tm == 0 and N % tn == 0 and tm % pr == 0 and tn % pc == 0
    blocks = (_nbytes((tm, K), BF16) + 2 * _nbytes((K, tn), F32) + _nbytes((tm, tn), BF16)
              + _nbytes((8, tn), F32) * 2)
    scratch = 2 * _nbytes((K, tn), BF16) + _nbytes((tm + SUBLANES, tn), F32)
    return pl.pallas_call(
        functools.partial(_ffn_up_body, tm=tm, tiles_per_seq=S // tm, pr=pr, pc=pc),
        grid=(N // tn, M // tm),
        in_specs=[
            pl.BlockSpec((tm, K), lambda j, i: (i, 0)),
            pl.BlockSpec((K, tn), lambda j, i: (0, j)),
            pl.BlockSpec((K, tn), lambda j, i: (0, j)),
            pl.BlockSpec((CONV_W, tn), lambda j, i: (0, j)),
            pl.BlockSpec((1, tn), lambda j, i: (0, j)),
        ],
        out_specs=pl.BlockSpec((tm, tn), lambda j, i: (i, j)),
        out_shape=jax.ShapeDtypeStruct((M, N), BF16),
        scratch_shapes=[pltpu.VMEM((K, tn), BF16), pltpu.VMEM((K, tn), BF16),
                        pltpu.VMEM((tm + SUBLANES, tn), F32)],
        compiler_params=pltpu.CompilerParams(
            dimension_semantics=("parallel", "arbitrary"),
            vmem_limit_bytes=_vmem_limit(blocks, scratch + 8 * _nbytes((pr, pc), F32))),
        name="ffn_gate_up",
    )(xb, w_gate, w_up, conv_w, conv_b)


def _ffn_down_body(h_ref, w_ref, x_ref, g_ref, b_ref, o_ref, *, alpha):
    kk = pl.program_id(1)

    @pl.when(kk == 0)
    def _():
        o_ref[...] = alpha * x_ref[...]

    o_ref[...] += jnp.dot(h_ref[...], w_ref[...], preferred_element_type=F32)

    @pl.when(kk == pl.num_programs(1) - 1)
    def _():
        o_ref[...] = _layer_norm(o_ref[...], g_ref[...], b_ref[...])


def _ffn_down(h, w_down, resid, g, b, alpha, tm=512, tk=2816):
    M, K = h.shape
    N = w_down.shape[1]
    tm, tk = min(tm, M), min(tk, K)
    assert M % tm == 0 and K % tk == 0
    blocks = (_nbytes((tm, tk), BF16) + _nbytes((tk, N), BF16) + 2 * _nbytes((tm, N), F32))
    return pl.pallas_call(
        functools.partial(_ffn_down_body, alpha=alpha),
        grid=(M // tm, K // tk),
        in_specs=[
            pl.BlockSpec((tm, tk), lambda i, kk: (i, kk)),
            pl.BlockSpec((tk, N), lambda i, kk: (kk, 0)),
            pl.BlockSpec((tm, N), lambda i, kk: (i, 0)),
            pl.BlockSpec((1, N), lambda i, kk: (0, 0)),
            pl.BlockSpec((1, N), lambda i, kk: (0, 0)),
        ],
        out_specs=pl.BlockSpec((tm, N), lambda i, kk: (i, 0)),
        out_shape=jax.ShapeDtypeStruct((M, N), F32),
        compiler_params=pltpu.CompilerParams(
            dimension_semantics=("parallel", "arbitrary"),
            vmem_limit_bytes=_vmem_limit(blocks, _nbytes((tm, N), F32))),
        name="ffn_down_layernorm",
    )(h, w_down, resid, g, b)


def _ple_body(x_ref, wg_ref, p_ref, wp_ref, o_ref, xb_ref, pb_ref):
    j = pl.program_id(1)
    tn = o_ref.shape[1]

    @pl.when(j == 0)
    def _():
        xb_ref[...] = x_ref[...].astype(BF16)
        pb_ref[...] = p_ref[...].astype(BF16)

    gate = jax.nn.sigmoid(jnp.dot(xb_ref[...], wg_ref[...], preferred_element_type=F32))
    emb = jnp.dot(pb_ref[...], wp_ref[...], preferred_element_type=F32)
    o_ref[...] = x_ref[:, pl.ds(pl.multiple_of(j * tn, tn), tn)] + gate * emb


def _ple(x, p, w_gate, w_proj, tm=1024, tn=1024):
    M, K = x.shape
    N = w_gate.shape[1]
    P = p.shape[1]
    tm, tn = min(tm, M), min(tn, N)
    assert K == N and M % tm == 0 and N % tn == 0
    blocks = (_nbytes((tm, K), F32) + _nbytes((K, tn), BF16) + _nbytes((tm, P), F32)
              + _nbytes((P, tn), BF16) + _nbytes((tm, tn), F32))
    scratch = _nbytes((tm, K), BF16) + _nbytes((tm, P), BF16)
    return pl.pallas_call(
        _ple_body,
        grid=(M // tm, N // tn),
        in_specs=[
            pl.BlockSpec((tm, K), lambda i, j: (i, 0)),
            pl.BlockSpec((K, tn), lambda i, j: (0, j)),
            pl.BlockSpec((tm, P), lambda i, j: (i, 0)),
            pl.BlockSpec((P, tn), lambda i, j: (0, j)),
        ],
        out_specs=pl.BlockSpec((tm, tn), lambda i, j: (i, j)),
        out_shape=jax.ShapeDtypeStruct((M, N), F32),
        scratch_shapes=[pltpu.VMEM((tm, K), BF16), pltpu.VMEM((tm, P), BF16)],
        compiler_params=pltpu.CompilerParams(
            dimension_semantics=("parallel", "arbitrary"),
            vmem_limit_bytes=_vmem_limit(blocks, scratch + 3 * _nbytes((tm, tn), F32))),
        name="ple_gate",
    )(x, w_gate, p, w_proj)


def _cast_small_body(x_ref, w_ref, xb_ref, o_ref):
    xb = x_ref[...].astype(BF16)
    xb_ref[...] = xb
    o_ref[...] = lax.dot_general(xb, w_ref[...].astype(BF16), (((1,), (1,)), ((), ())),
                                 preferred_element_type=F32)


def _cast_and_small_proj(x, w_small, tm=1024):
    M, K = x.shape
    ns = w_small.shape[0]
    tm = min(tm, M)
    assert M % tm == 0
    blocks = _nbytes((tm, K), F32) + _nbytes((ns, K), F32) + _nbytes((tm, K), BF16) \
        + _nbytes((tm, ns), F32)
    return pl.pallas_call(
        _cast_small_body,
        grid=(M // tm,),
        in_specs=[pl.BlockSpec((tm, K), lambda i: (i, 0)),
                  pl.BlockSpec((ns, K), lambda i: (0, 0))],
        out_specs=[pl.BlockSpec((tm, K), lambda i: (i, 0)),
                   pl.BlockSpec((tm, ns), lambda i: (i, 0))],
        out_shape=[jax.ShapeDtypeStruct((M, K), BF16), jax.ShapeDtypeStruct((M, ns), F32)],
        compiler_params=pltpu.CompilerParams(
            dimension_semantics=("parallel",),
            vmem_limit_bytes=_vmem_limit(blocks, _nbytes((tm, K), BF16))),
        name="in_proj_small",
    )(x, w_small)


def _in_proj(x, w_in, layer, rank):
    M, D = x.shape
    gla_qk = D // 2
    fox_w = FOX_HEADS * LANES
    names = ("gq", "gk", "gv", "gr", "glr", "fq", "fk", "fv", "ff", "ma", "mb")
    widths = (gla_qk, gla_qk, D, D, rank, fox_w, fox_w, fox_w, FOX_HEADS, D, D)
    assert sum(widths) == w_in.shape[2]
    width, offset, off = {}, {}, 0
    for n, wd in zip(names, widths):
        width[n], offset[n] = wd, off
        off += wd
    w_t = jnp.swapaxes(w_in, 1, 2)

    out_scale = {"gq": (gla_qk // GLA_HEADS) ** -0.5, "fq": LANES ** -0.5 * LOG2E}
    T = IN_PROJ_TILE

    def project(order, out_dtype, name):
        src_rows, scales, cols, o = [], [], {}, 0
        for n in order:
            assert width[n] % T == 0
            cols[n] = o
            o += width[n]
            src_rows += [offset[n] + t for t in range(0, width[n], T)]
            scales.append(jnp.full((1, width[n]), out_scale.get(n, 1.0), F32))
        out = _matmul_nt(xb, w_t, layer, src_rows, jnp.concatenate(scales, axis=1),
                         out_dtype, 1024, T, name)
        return out, cols

    small, cols_s = [], {}
    for t, n in enumerate(("glr", "ff")):
        wseg = w_t[layer, offset[n]:offset[n] + width[n], :]
        small.append(jnp.pad(wseg, ((0, LANES - width[n]), (0, 0))))
        cols_s[n] = t * LANES
    w_small = jnp.concatenate(small, axis=0)
    xb, proj_s = _cast_and_small_proj(x, w_small)

    proj_b = project(("gv", "fq", "fk", "fv"), BF16, "in_proj_bf16")
    proj_f = project(("gq", "gk", "gr", "ma", "mb"), F32, "in_proj_f32")
    return proj_b, proj_f, (proj_s, cols_s)


def kernel(x, p, w_in, w_gla_lr, b_gla_lr, gla_norm_g, b_forget, w_branch_gla, w_branch_fox,
           w_out, ln1_g, ln1_b, w_gate, w_up, conv_w, conv_b, w_down, ln2_g, ln2_b,
           w_ple_gate, w_ple_proj):
    B, S, D = x.shape
    depth = w_in.shape[0]
    alpha = (2 * depth) ** 0.25
    M = B * S
    xf = x.reshape(M, D)
    for i in range(depth):
        rank = w_gla_lr.shape[1]
        (proj_b, cols_b), (proj_f, cols_f), (proj_s, cols_s) = _in_proj(xf, w_in, i, rank)

        ff = proj_s[:, cols_s["ff"]:cols_s["ff"] + FOX_HEADS]
        c = _forget_cumsum(ff, b_forget[i], B, S)
        of = _fox(proj_b, c, B, S, cols_b, LANES)

        w_lr = jnp.pad(w_gla_lr[i], ((0, LANES - rank), (0, 0))).astype(BF16)
        og = _gla(proj_f, proj_b, proj_s, w_lr, b_gla_lr[i].reshape(1, -1),
                  gla_norm_g[i].reshape(1, -1), B, S, {**cols_f, **cols_b})

        merged = _merge(og, of, w_branch_gla[i].astype(BF16), w_branch_fox[i].astype(BF16),
                        proj_f, cols_f)
        x1, x1b = _proj_ln(merged, w_out[i].astype(BF16), xf, ln1_g[i].reshape(1, -1),
                           ln1_b[i].reshape(1, -1), alpha)
        h = _ffn_up(x1b, w_gate[i], w_up[i], conv_w[i], conv_b[i].reshape(1, -1), S)
        x2 = _ffn_down(h, w_down[i].astype(BF16), x1, ln2_g[i].reshape(1, -1),
                       ln2_b[i].reshape(1, -1), alpha)
        xf = _ple(x2, p[i].reshape(M, -1), w_ple_gate[i].astype(BF16),
                  w_ple_proj[i].astype(BF16))
    return xf.reshape(B, S, D)
```

```python
import functools

import jax
import jax.numpy as jnp
from jax import lax
from jax.experimental import pallas as pl
from jax.experimental.pallas import tpu as pltpu

F32 = jnp.float32
BF16 = jnp.bfloat16

GLA_HEADS = 4
GLA_TAU = 16.0
FOX_HEADS = 8
CONV_W = 3
LN_EPS = 1e-5
RMS_EPS = 1e-6
LOG2E = 1.4426950408889634

LANES = 128
SUBLANES = 8
MXU_COLS = 256
VMEM_BYTES_V7X = 64 * 1024 * 1024

IN_PROJ_TILE = 1024
GLA_CHUNK = 128
GLA_GROUP = 4
FOX_TILE = 512
FOX_GROUP = 4


def _vmem_limit(block_bytes, extra_bytes=0):
    est = 2 * block_bytes + extra_bytes + (4 << 20)
    return int(min(max(est, 16 << 20), VMEM_BYTES_V7X - (8 << 20)))


def _nbytes(shape, dtype):
    n = 1
    for s in shape:
        n *= s
    return n * jnp.dtype(dtype).itemsize


def _log_sigmoid(z):
    return jnp.minimum(z, 0.0) - jnp.log(1.0 + jnp.exp(-jnp.abs(z)))


def _layer_norm(y, g, b):
    mu = jnp.mean(y, axis=-1, keepdims=True)
    d = y - mu
    var = jnp.mean(d * d, axis=-1, keepdims=True)
    return d * lax.rsqrt(var + LN_EPS) * g + b


def _mm_nt_body(rows_ref, a_ref, wt_ref, cs_ref, o_ref, wb_ref):
    del rows_ref

    @pl.when(pl.program_id(1) == 0)
    def _():
        wb_ref[...] = wt_ref[0].astype(BF16)

    a = a_ref[...]
    for c in range(0, o_ref.shape[1], MXU_COLS):
        cols = slice(c, c + MXU_COLS)
        acc = lax.dot_general(a, wb_ref[cols, :], (((1,), (1,)), ((), ())),
                              preferred_element_type=F32)
        o_ref[:, cols] = (acc * cs_ref[:, cols]).astype(o_ref.dtype)


def _matmul_nt(a, w_t, layer, src_rows, col_scale, out_dtype, tm, tn, name):
    M, K = a.shape
    nt = len(src_rows)
    tm = min(tm, M)
    assert M % tm == 0 and col_scale.shape == (1, nt * tn)
    assert all(r % SUBLANES == 0 and r + tn <= w_t.shape[1] for r in src_rows)
    grid_spec = pltpu.PrefetchScalarGridSpec(
        num_scalar_prefetch=1,
        grid=(nt, M // tm),
        in_specs=[pl.BlockSpec((tm, K), lambda j, i, rows: (i, 0)),
                  pl.BlockSpec((pl.Element(1), pl.Element(tn), pl.Element(K)),
                               lambda j, i, rows: (layer, pl.multiple_of(rows[j], SUBLANES), 0)),
                  pl.BlockSpec((1, tn), lambda j, i, rows: (0, j))],
        out_specs=pl.BlockSpec((tm, tn), lambda j, i, rows: (i, j)),
        scratch_shapes=[pltpu.VMEM((tn, K), BF16)],
    )
    blocks = (_nbytes((tm, K), a.dtype) + _nbytes((tn, K), w_t.dtype)
              + _nbytes((tm, tn), out_dtype))
    return pl.pallas_call(
        _mm_nt_body,
        grid_spec=grid_spec,
        out_shape=jax.ShapeDtypeStruct((M, nt * tn), out_dtype),
        compiler_params=pltpu.CompilerParams(
            dimension_semantics=("parallel", "arbitrary"),
            vmem_limit_bytes=_vmem_limit(blocks, _nbytes((tn, K), BF16) + _nbytes((tm, tn), F32))),
        name=name,
    )(jnp.asarray(src_rows, jnp.int32), a, w_t, col_scale)


def _fcum_body(ff_ref, bias_ref, c_ref, *, groups):
    z = ff_ref[...] + bias_ref[...]
    x = _log_sigmoid(z)
    rows, lanes = x.shape
    lane = lax.broadcasted_iota(jnp.int32, x.shape, 1)
    s = 1
    while s < lanes:
        x = x + jnp.where(lane >= s, pltpu.roll(x, s, axis=1), 0.0)
        s *= 2
    tot = jnp.broadcast_to(x[:, lanes - 1:lanes], x.shape)
    grp = lax.broadcasted_iota(jnp.int32, x.shape, 0) % groups
    inc = tot
    s = 1
    while s < groups:
        inc = inc + jnp.where(grp >= s, pltpu.roll(inc, s, axis=0), 0.0)
        s *= 2
    c_ref[...] = x + (inc - tot)


def _forget_cumsum(ff, b_forget, B, S):
    H = ff.shape[1]
    groups = S // LANES
    ff_t = ff.reshape(B, S, H).transpose(0, 2, 1).reshape(B * H * groups, LANES)
    bias = jnp.broadcast_to(b_forget.reshape(1, H, 1, 1),
                            (B, H, groups, LANES)).reshape(B * H * groups, LANES)
    c = pl.pallas_call(
        functools.partial(_fcum_body, groups=groups),
        out_shape=jax.ShapeDtypeStruct(ff_t.shape, F32),
        name="fox_forget_cumsum",
    )(ff_t, bias)
    return c.reshape(B, H, S)


def _gla_body(q_ref, k_ref, v_ref, r_ref, lr_ref, wlr_ref, blr_ref, g_ref,
              o_ref, st_ref, *, C, DK, DV, G):
    c = pl.program_id(2)

    @pl.when(c == 0)
    def _():
        st_ref[...] = jnp.zeros_like(st_ref)

    lr = lr_ref[...].astype(BF16)
    heads = []
    for g in range(G):
        kcols = slice(g * DK, (g + 1) * DK)
        vcols = slice(g * DV, (g + 1) * DV)
        heads.append(_gla_head(
            q_ref[:, kcols], k_ref[:, kcols], v_ref[:, vcols], r_ref[:, vcols], lr,
            wlr_ref[:, kcols], blr_ref[:, kcols], g_ref[...], st_ref.at[g],
            C=C, DK=DK, DV=DV))
    live = list(range(G))
    while live:
        for g in list(live):
            try:
                next(heads[g])
            except StopIteration as done:
                o_ref[:, g * DV:(g + 1) * DV] = done.value.astype(o_ref.dtype)
                live.remove(g)


def _gla_head(q, k, v, gate, lr, wlr, blr, norm_g, st_ref, *, C, DK, DV):
    z = jnp.dot(lr, wlr, preferred_element_type=F32) + blr
    yield
    la = _log_sigmoid(z) * (LOG2E / GLA_TAU)

    row = lax.broadcasted_iota(jnp.int32, (C, DK), 0)
    ti = lax.broadcasted_iota(jnp.int32, (C, C), 0)
    si = lax.broadcasted_iota(jnp.int32, (C, C), 1)
    tril = jnp.where(si <= ti, 1.0, 0.0).astype(BF16)
    hi = la.astype(BF16)
    rest = la - hi.astype(F32)
    mid = rest.astype(BF16)
    lo = (rest - mid.astype(F32)).astype(BF16)
    yield
    b = (jnp.dot(tril, hi, preferred_element_type=F32)
         + jnp.dot(tril, mid, preferred_element_type=F32)
         + jnp.dot(tril, lo, preferred_element_type=F32))
    yield

    st = st_ref[...]
    qe = (q * jnp.exp2(b)).astype(BF16)
    inter = lax.dot_general(qe, st.astype(BF16), (((1,), (1,)), ((), ())),
                            preferred_element_type=F32)
    blast = b[C - 1:C, :]
    kd = (k * jnp.exp2(blast - b)).astype(BF16)
    yield

    pair =jnp.where(ti > si, ti ^ si, 0)
    attn = jnp.where(ti == si, jnp.sum(q * k, axis=-1, keepdims=True), 0.0)
    NB = C // SUBLANES
    b3 = b.reshape(NB, SUBLANES, DK)
    sub3 = lax.broadcasted_iota(jnp.int32, (NB, SUBLANES, DK), 1)
    m = C // 2
    while m >= 1:
        blk = 2 * m
        in_b = (row & (blk - 1)) >= m
        if m == 1:
            e = jnp.exp2(la)
        else:
            if blk >= SUBLANES:
                nblk = C // blk
                r = jnp.broadcast_to(b.reshape(nblk, blk, DK)[:, m - 1:m, :],
                                     (nblk, blk, DK)).reshape(C, DK)
            else:
                r3 = None
                for start in range(0, SUBLANES, blk):
                    cand = jnp.broadcast_to(b3[:, start + m - 1:start + m, :], b3.shape)
                    r3 = cand if r3 is None else jnp.where(sub3 >= start, cand, r3)
                r = r3.reshape(C, DK)
            e = jnp.exp2(-jnp.abs(b - r))
        x = jnp.where(in_b, q, k)
        x = (jnp.where(in_b, x * e, x) if m == 1 else x * e).astype(BF16)
        a = lax.dot_general(x, x, (((1,), (1,)), ((), ())),
                            preferred_element_type=F32)
        attn = jnp.where((pair >> (m.bit_length() - 1)) == 1, a, attn)
        m //= 2
        yield

    o = inter + jnp.dot(attn.astype(BF16), v, preferred_element_type=F32)
    yield

    upd = lax.dot_general(v, kd, (((0,), (0,)), ((), ())),
                          preferred_element_type=F32)
    st_ref[...] = st * jnp.exp2(blast) + upd
    yield

    ms = jnp.mean(o * o, axis=-1, keepdims=True)
    on = o * lax.rsqrt(ms + RMS_EPS) * norm_g
    return on * (gate * jax.nn.sigmoid(gate))


def _gla(proj_f, proj_b, proj_s, w_lr, b_lr, norm_g, B, S, cols):
    H = GLA_HEADS
    G = GLA_GROUP
    DK = w_lr.shape[1] // H
    DV = norm_g.shape[1]
    C = min(GLA_CHUNK, S)
    NC = S // C
    WK, WV = G * DK, G * DV
    assert S % C == 0 and C % (2 * SUBLANES) == 0 and H % G == 0
    assert all(cols[n] % WK == 0 for n in ("gq", "gk")) and all(cols[n] % WV == 0 for n in ("gr", "gv"))
    qo, ko, ro, vo = (cols["gq"] // WK, cols["gk"] // WK, cols["gr"] // WV, cols["gv"] // WV)
    rows = lambda b, h, c: b * NC + c
    blocks = (2 * _nbytes((C, WK), F32) + _nbytes((C, WV), BF16) + _nbytes((C, WV), F32)
              + _nbytes((C, LANES), F32) + _nbytes((LANES, WK), BF16) + _nbytes((C, WV), BF16))
    return pl.pallas_call(
        functools.partial(_gla_body, C=C, DK=DK, DV=DV, G=G),
        grid=(B, H // G, NC),
        in_specs=[
            pl.BlockSpec((C, WK), lambda b, h, c: (rows(b, h, c), qo + h)),
            pl.BlockSpec((C, WK), lambda b, h, c: (rows(b, h, c), ko + h)),
            pl.BlockSpec((C, WV), lambda b, h, c: (rows(b, h, c), vo + h)),
            pl.BlockSpec((C, WV), lambda b, h, c: (rows(b, h, c), ro + h)),
            pl.BlockSpec((C, LANES), lambda b, h, c: (rows(b, h, c), 0)),
            pl.BlockSpec((LANES, WK), lambda b, h, c: (0, h)),
            pl.BlockSpec((1, WK), lambda b, h, c: (0, h)),
            pl.BlockSpec((1, DV), lambda b, h, c: (0, 0)),
        ],
        out_specs=pl.BlockSpec((C, WV), lambda b, h, c: (rows(b, h, c), h)),
        out_shape=jax.ShapeDtypeStruct((B * S, H * DV), BF16),
        scratch_shapes=[pltpu.VMEM((G, DV, DK), F32)],
        compiler_params=pltpu.CompilerParams(
            dimension_semantics=("parallel", "parallel", "arbitrary"),
            vmem_limit_bytes=_vmem_limit(blocks, G * (12 << 20))),
        name="gla_chunked",
    )(proj_f, proj_f, proj_b, proj_f, proj_s, w_lr, b_lr, norm_g)


def _fox_body(q_ref, k_ref, v_ref, c_ref, o_ref, m_ref, acc_ref, cqb_ref, *, T, G):
    qi = pl.program_id(2)
    lane_tiles = T // LANES
    for g in range(G):
        cqb_ref[g] = jnp.broadcast_to(c_ref[g, qi] * LOG2E, (LANES, T)).T
    m_ref[...] = jnp.full_like(m_ref, -jnp.inf)
    acc_ref[...] = jnp.zeros_like(acc_ref)
    ones = jnp.ones((T, LANES), BF16)

    def qk(ki):
        start = pl.multiple_of(ki * T, T)
        return [lax.dot_general(
            q_ref[:, g * LANES:(g + 1) * LANES], k_ref[pl.ds(start, T), g * LANES:(g + 1) * LANES],
            (((1,), (1,)), ((), ())), preferred_element_type=F32) for g in range(G)]

    def softmax_pv(ki, scores, on_diagonal):
        start = pl.multiple_of(ki * T, T)
        for g in range(G):
            head = slice(g * LANES, (g + 1) * LANES)
            t = scores[g] - c_ref[g, ki] * LOG2E
            if on_diagonal:
                ti = lax.broadcasted_iota(jnp.int32, (T, T), 0)
                si = lax.broadcasted_iota(jnp.int32, (T, T), 1)
                t = jnp.where(si <= ti, t, -jnp.inf)
            cq = cqb_ref[g]
            m_prev = m_ref[g]
            m_new = jnp.maximum(m_prev, cq + jnp.max(t, axis=-1, keepdims=True))
            p = jnp.exp2(t + jnp.concatenate([cq - m_new] * lane_tiles, axis=1))
            alpha = jnp.exp2(m_prev - m_new)
            v_ext = jnp.concatenate([v_ref[pl.ds(start, T), head], ones], axis=1)
            acc_ref[g] = jnp.concatenate([alpha, alpha], axis=1) * acc_ref[g] + jnp.dot(
                p.astype(BF16), v_ext, preferred_element_type=F32)
            m_ref[g] = m_new

    def pair(kp, carry):
        sa = qk(2 * kp)
        sb = qk(2 * kp + 1)
        softmax_pv(2 * kp, sa, False)
        softmax_pv(2 * kp + 1, sb, False)
        return carry

    lax.fori_loop(0, qi // 2, pair, 0)

    @pl.when(qi % 2 == 1)
    def _():
        softmax_pv(qi - 1, qk(qi - 1), False)

    softmax_pv(qi, qk(qi), True)
    for g in range(G):
        acc = acc_ref[g]
        o_ref[:, g * LANES:(g + 1) * LANES] = (acc[:, :LANES] / acc[:, LANES:]).astype(o_ref.dtype)


def _fox(proj_b, c, B, S, cols, dh):
    H = FOX_HEADS
    G = FOX_GROUP
    T = min(FOX_TILE, S)
    NT = S // T
    W = G * dh
    assert S % T == 0 and H % G == 0 and dh == LANES
    qo, ko, vo = cols["fq"] // W, cols["fk"] // W, cols["fv"] // W
    assert all(cols[n] % W == 0 for n in ("fq", "fk", "fv"))
    c_row = c.reshape(B, H, NT, 1, T)
    blocks = (2 * _nbytes((T, W), BF16) + 2 * _nbytes((S, W), BF16)
              + G * _nbytes((NT, SUBLANES, T), F32))
    return pl.pallas_call(
        functools.partial(_fox_body, T=T, G=G),
        grid=(B, H // G, NT),
        in_specs=[
            pl.BlockSpec((T, W), lambda b, h, qi: (b * NT + qi, qo + h)),
            pl.BlockSpec((S, W), lambda b, h, qi: (b, ko + h)),
            pl.BlockSpec((S, W), lambda b, h, qi: (b, vo + h)),
            pl.BlockSpec((None, G, NT, 1, T), lambda b, h, qi: (b, h, 0, 0, 0)),
        ],
        out_specs=pl.BlockSpec((T, W), lambda b, h, qi: (b * NT + qi, h)),
        out_shape=jax.ShapeDtypeStruct((B * S, H * dh), BF16),
        scratch_shapes=[pltpu.VMEM((G, T, LANES), F32), pltpu.VMEM((G, T, dh + LANES), F32),
                        pltpu.VMEM((G, T, LANES), F32)],
        compiler_params=pltpu.CompilerParams(
            dimension_semantics=("parallel", "parallel", "arbitrary"),
            vmem_limit_bytes=_vmem_limit(blocks, (4 * G + 8) * _nbytes((T, T), F32))),
        name="fox_attention",
    )(proj_b, proj_b, proj_b, c_row)


def _merge_body(og_ref, of_ref, wg_ref, wf_ref, ma_ref, mb_ref, o_ref):
    og = og_ref[...]
    of = of_ref[...]
    for c in range(0, o_ref.shape[1], MXU_COLS):
        cols = slice(c, c + MXU_COLS)
        yg = jnp.dot(og, wg_ref[:, cols], preferred_element_type=F32)
        yf = jnp.dot(of, wf_ref[:, cols], preferred_element_type=F32)
        o_ref[:, cols] = (jax.nn.sigmoid(ma_ref[:, cols]) * yg
                          + jax.nn.sigmoid(mb_ref[:, cols]) * yf).astype(o_ref.dtype)


def _merge(og, of, w_bg, w_bf, proj_f, cols, tm=1024, tn=1024):
    M, KG = og.shape
    KF = of.shape[1]
    N = w_bg.shape[1]
    tm, tn = min(tm, M), min(tn, N)
    ao, bo = cols["ma"] // tn, cols["mb"] // tn
    assert cols["ma"] % tn == 0 and cols["mb"] % tn == 0
    blocks = (_nbytes((tm, KG + KF), BF16) + _nbytes((KG + KF, tn), BF16)
              + 2 * _nbytes((tm, tn), F32) + _nbytes((tm, tn), BF16))
    return pl.pallas_call(
        _merge_body,
        grid=(N // tn, M // tm),
        in_specs=[
            pl.BlockSpec((tm, KG), lambda j, i: (i, 0)),
            pl.BlockSpec((tm, KF), lambda j, i: (i, 0)),
            pl.BlockSpec((KG, tn), lambda j, i: (0, j)),
            pl.BlockSpec((KF, tn), lambda j, i: (0, j)),
            pl.BlockSpec((tm, tn), lambda j, i: (i, ao + j)),
            pl.BlockSpec((tm, tn), lambda j, i: (i, bo + j)),
        ],
        out_specs=pl.BlockSpec((tm, tn), lambda j, i: (i, j)),
        out_shape=jax.ShapeDtypeStruct((M, N), BF16),
        compiler_params=pltpu.CompilerParams(
            dimension_semantics=("parallel", "parallel"),
            vmem_limit_bytes=_vmem_limit(blocks, 3 * _nbytes((tm, tn), F32))),
        name="branch_merge",
    )(og, of, w_bg, w_bf, proj_f, proj_f)


def _proj_ln_body(a_ref, w_ref, x_ref, g_ref, b_ref, o_ref, ob_ref, *, alpha, parts):
    pr = o_ref.shape[0] // parts
    ys = [jnp.dot(a_ref[r * pr:(r + 1) * pr, :], w_ref[...], preferred_element_type=F32)
          for r in range(parts)]
    for r in range(parts):
        rows = slice(r * pr, (r + 1) * pr)
        out = _layer_norm(alpha * x_ref[rows, :] + ys[r], g_ref[...], b_ref[...])
        o_ref[rows, :] = out
        ob_ref[rows, :] = out.astype(BF16)


def _proj_ln(a, w, resid, g, b, alpha, tm=512):
    M, K = a.shape
    N = w.shape[1]
    tm = min(tm, M)
    blocks = (_nbytes((tm, K), BF16) + _nbytes((K, N), BF16) + 2 * _nbytes((tm, N), F32)
              + _nbytes((tm, N), BF16))
    return pl.pallas_call(
        functools.partial(_proj_ln_body, alpha=alpha, parts=2),
        grid=(M // tm,),
        in_specs=[
            pl.BlockSpec((tm, K), lambda i: (i, 0)),
            pl.BlockSpec((K, N), lambda i: (0, 0)),
            pl.BlockSpec((tm, N), lambda i: (i, 0)),
            pl.BlockSpec((1, N), lambda i: (0, 0)),
            pl.BlockSpec((1, N), lambda i: (0, 0)),
        ],
        out_specs=[pl.BlockSpec((tm, N), lambda i: (i, 0)),
                   pl.BlockSpec((tm, N), lambda i: (i, 0))],
        out_shape=[jax.ShapeDtypeStruct((M, N), F32), jax.ShapeDtypeStruct((M, N), BF16)],
        compiler_params=pltpu.CompilerParams(
            dimension_semantics=("parallel",),
            vmem_limit_bytes=_vmem_limit(blocks, 3 * _nbytes((tm, N), F32))),
        name="out_proj_layernorm",
    )(a, w, resid, g, b)


def _ffn_up_body(x_ref, wg_ref, wu_ref, cw_ref, cb_ref, h_ref, wgb_ref, wub_ref, gbuf_ref,
                 *, tm, tiles_per_seq, pr, pc):
    i = pl.program_id(1)
    HALO = SUBLANES
    tn = h_ref.shape[1]

    @pl.when(i == 0)
    def _():
        wgb_ref[...] = wg_ref[...].astype(BF16)
        wub_ref[...] = wu_ref[...].astype(BF16)

    @pl.when(i % tiles_per_seq == 0)
    def _():
        gbuf_ref[0:HALO, :] = jnp.zeros((HALO, tn), F32)

    cw = cw_ref[...]
    cb = cb_ref[...]
    parts = [(r, c) for c in range(tn // pc) for r in range(tm // pr)]

    def matmuls(r, c):
        x = x_ref[r * pr:(r + 1) * pr, :]
        cols = slice(c * pc, (c + 1) * pc)
        g = jnp.dot(x, wgb_ref[:, cols], preferred_element_type=F32)
        u = jnp.dot(x, wub_ref[:, cols], preferred_element_type=F32)
        gbuf_ref[HALO + r * pr:HALO + (r + 1) * pr, cols] = g
        return g, u

    def finish(r, c, g, u):
        cols = slice(c * pc, (c + 1) * pc)
        y = cb[:, cols] + cw[CONV_W - 1:CONV_W, cols] * g
        for j in range(CONV_W - 1):
            off = HALO - (CONV_W - 1) + j + r * pr
            y = y + cw[j:j + 1, cols] * gbuf_ref[off:off + pr, cols]
        h_ref[r * pr:(r + 1) * pr, cols] = (jax.nn.gelu(y, approximate=True) * u).astype(h_ref.dtype)

    pending = None
    for r, c in parts:
        gu = matmuls(r, c)
        if pending is not None:
            finish(*pending)
        pending = (r, c) + gu
    finish(*pending)
    gbuf_ref[0:HALO, :] = gbuf_ref[tm:tm + HALO, :]


def _ffn_up(xb, w_gate, w_up, conv_w, conv_b, S, tm=1024, tn=512, pr=512, pc=256):
    M, K = xb.shape
    N = w_gate.shape[1]
    tm, tn = min(tm, S), min(tn, N)
    pr, pc = min(pr, tm), min(pc, tn)
    assert S % tm == 0 and N % tn == 0 and tm % pr == 0 and tn % pc == 0
    blocks = (_nbytes((tm, K), BF16) + 2 * _nbytes((K, tn), F32) + _nbytes((tm, tn), BF16)
              + _nbytes((8, tn), F32) * 2)
    scratch = 2 * _nbytes((K, tn), BF16) + _nbytes((tm + SUBLANES, tn), F32)
    return pl.pallas_call(
        functools.partial(_ffn_up_body, tm=tm, tiles_per_seq=S // tm, pr=pr, pc=pc),
        grid=(N // tn, M // tm),
        in_specs=[
            pl.BlockSpec((tm, K), lambda j, i: (i, 0)),
            pl.BlockSpec((K, tn), lambda j, i: (0, j)),
            pl.BlockSpec((K, tn), lambda j, i: (0, j)),
            pl.BlockSpec((CONV_W, tn), lambda j, i: (0, j)),
            pl.BlockSpec((1, tn), lambda j, i: (0, j)),
        ],
        out_specs=pl.BlockSpec((tm, tn), lambda j, i: (i, j)),
        out_shape=jax.ShapeDtypeStruct((M, N), BF16),
        scratch_shapes=[pltpu.VMEM((K, tn), BF16), pltpu.VMEM((K, tn), BF16),
                        pltpu.VMEM((tm + SUBLANES, tn), F32)],
        compiler_params=pltpu.CompilerParams(
            dimension_semantics=("parallel", "arbitrary"),
            vmem_limit_bytes=_vmem_limit(blocks, scratch + 8 * _nbytes((pr, pc), F32))),
        name="ffn_gate_up",
    )(xb, w_gate, w_up, conv_w, conv_b)


def _ffn_down_body(h_ref, w_ref, x_ref, g_ref, b_ref, o_ref, *, alpha):
    kk = pl.program_id(1)

    @pl.when(kk == 0)
    def _():
        o_ref[...] = alpha * x_ref[...]

    o_ref[...] += jnp.dot(h_ref[...], w_ref[...], preferred_element_type=F32)

    @pl.when(kk == pl.num_programs(1) - 1)
    def _():
        o_ref[...] = _layer_norm(o_ref[...], g_ref[...], b_ref[...])


def _ffn_down(h, w_down, resid, g, b, alpha, tm=512, tk=2816):
    M, K = h.shape
    N = w_down.shape[1]
    tm, tk = min(tm, M), min(tk, K)
    assert M % tm == 0 and K % tk == 0
    blocks = (_nbytes((tm, tk), BF16) + _nbytes((tk, N), BF16) + 2 * _nbytes((tm, N), F32))
    return pl.pallas_call(
        functools.partial(_ffn_down_body, alpha=alpha),
        grid=(M // tm, K // tk),
        in_specs=[
            pl.BlockSpec((tm, tk), lambda i, kk: (i, kk)),
            pl.BlockSpec((tk, N), lambda i, kk: (kk, 0)),
            pl.BlockSpec((tm, N), lambda i, kk: (i, 0)),
            pl.BlockSpec((1, N), lambda i, kk: (0, 0)),
            pl.BlockSpec((1, N), lambda i, kk: (0, 0)),
        ],
        out_specs=pl.BlockSpec((tm, N), lambda i, kk: (i, 0)),
        out_shape=jax.ShapeDtypeStruct((M, N), F32),
        compiler_params=pltpu.CompilerParams(
            dimension_semantics=("parallel", "arbitrary"),
            vmem_limit_bytes=_vmem_limit(blocks, _nbytes((tm, N), F32))),
        name="ffn_down_layernorm",
    )(h, w_down, resid, g, b)


def _ple_body(x_ref, wg_ref, p_ref, wp_ref, o_ref, xb_ref, pb_ref):
    j = pl.program_id(1)
    tn = o_ref.shape[1]

    @pl.when(j == 0)
    def _():
        xb_ref[...] = x_ref[...].astype(BF16)
        pb_ref[...] = p_ref[...].astype(BF16)

    xb = xb_ref[...]
    pb = pb_ref[...]
    for c in range(0, tn, MXU_COLS):
        cols = slice(c, c + MXU_COLS)
        gate = jax.nn.sigmoid(jnp.dot(xb, wg_ref[:, cols], preferred_element_type=F32))
        emb = jnp.dot(pb, wp_ref[:, cols], preferred_element_type=F32)
        resid = x_ref[:, pl.ds(pl.multiple_of(j * tn + c, MXU_COLS), MXU_COLS)]
        o_ref[:, cols] = resid + gate * emb


def _ple(x, p, w_gate, w_proj, tm=1024, tn=1024):
    M, K = x.shape
    N = w_gate.shape[1]
    P = p.shape[1]
    tm, tn = min(tm, M), min(tn, N)
    assert K == N and M % tm == 0 and N % tn == 0
    blocks = (_nbytes((tm, K), F32) + _nbytes((K, tn), BF16) + _nbytes((tm, P), F32)
              + _nbytes((P, tn), BF16) + _nbytes((tm, tn), F32))
    scratch = _nbytes((tm, K), BF16) + _nbytes((tm, P), BF16)
    return pl.pallas_call(
        _ple_body,
        grid=(M // tm, N // tn),
        in_specs=[
            pl.BlockSpec((tm, K), lambda i, j: (i, 0)),
            pl.BlockSpec((K, tn), lambda i, j: (0, j)),
            pl.BlockSpec((tm, P), lambda i, j: (i, 0)),
            pl.BlockSpec((P, tn), lambda i, j: (0, j)),
        ],
        out_specs=pl.BlockSpec((tm, tn), lambda i, j: (i, j)),
        out_shape=jax.ShapeDtypeStruct((M, N), F32),
        scratch_shapes=[pltpu.VMEM((tm, K), BF16), pltpu.VMEM((tm, P), BF16)],
        compiler_params=pltpu.CompilerParams(
            dimension_semantics=("parallel", "arbitrary"),
            vmem_limit_bytes=_vmem_limit(blocks, scratch + 3 * _nbytes((tm, tn), F32))),
        name="ple_gate",
    )(x, w_gate, p, w_proj)


def _cast_small_body(x_ref, w_ref, xb_ref, o_ref):
    xb = x_ref[...].astype(BF16)
    xb_ref[...] = xb
    o_ref[...] = lax.dot_general(xb, w_ref[...].astype(BF16), (((1,), (1,)), ((), ())),
                                 preferred_element_type=F32)


def _cast_and_small_proj(x, w_small, tm=1024):
    M, K = x.shape
    ns = w_small.shape[0]
    tm = min(tm, M)
    assert M % tm == 0
    blocks = _nbytes((tm, K), F32) + _nbytes((ns, K), F32) + _nbytes((tm, K), BF16) \
        + _nbytes((tm, ns), F32)
    return pl.pallas_call(
        _cast_small_body,
        grid=(M // tm,),
        in_specs=[pl.BlockSpec((tm, K), lambda i: (i, 0)),
                  pl.BlockSpec((ns, K), lambda i: (0, 0))],
        out_specs=[pl.BlockSpec((tm, K), lambda i: (i, 0)),
                   pl.BlockSpec((tm, ns), lambda i: (i, 0))],
        out_shape=[jax.ShapeDtypeStruct((M, K), BF16), jax.ShapeDtypeStruct((M, ns), F32)],
        compiler_params=pltpu.CompilerParams(
            dimension_semantics=("parallel",),
            vmem_limit_bytes=_vmem_limit(blocks, _nbytes((tm, K), BF16))),
        name="in_proj_small",
    )(x, w_small)


def _in_proj(x, w_in, layer, rank):
    M, D = x.shape
    gla_qk = D // 2
    fox_w = FOX_HEADS * LANES
    names = ("gq", "gk", "gv", "gr", "glr", "fq", "fk", "fv", "ff", "ma", "mb")
    widths = (gla_qk, gla_qk, D, D, rank, fox_w, fox_w, fox_w, FOX_HEADS, D, D)
    assert sum(widths) == w_in.shape[2]
    width, offset, off = {}, {}, 0
    for n, wd in zip(names, widths):
        width[n], offset[n] = wd, off
        off += wd
    w_t = jnp.swapaxes(w_in, 1, 2)

    out_scale = {"gq": (gla_qk // GLA_HEADS) ** -0.5, "fq": LANES ** -0.5 * LOG2E}
    T = IN_PROJ_TILE

    def project(order, out_dtype, name):
        src_rows, scales, cols, o = [], [], {}, 0
        for n in order:
            assert width[n] % T == 0
            cols[n] = o
            o += width[n]
            src_rows += [offset[n] + t for t in range(0, width[n], T)]
            scales.append(jnp.full((1, width[n]), out_scale.get(n, 1.0), F32))
        out = _matmul_nt(xb, w_t, layer, src_rows, jnp.concatenate(scales, axis=1),
                         out_dtype, 1024, T, name)
        return out, cols

    small, cols_s = [], {}
    for t, n in enumerate(("glr", "ff")):
        wseg = w_t[layer, offset[n]:offset[n] + width[n], :]
        small.append(jnp.pad(wseg, ((0, LANES - width[n]), (0, 0))))
        cols_s[n] = t * LANES
    w_small = jnp.concatenate(small, axis=0)
    xb, proj_s = _cast_and_small_proj(x, w_small)

    proj_b = project(("gv", "fq", "fk", "fv"), BF16, "in_proj_bf16")
    proj_f = project(("gq", "gk", "gr", "ma", "mb"), F32, "in_proj_f32")
    return proj_b, proj_f, (proj_s, cols_s)


def kernel(x, p, w_in, w_gla_lr, b_gla_lr, gla_norm_g, b_forget, w_branch_gla, w_branch_fox,
           w_out, ln1_g, ln1_b, w_gate, w_up, conv_w, conv_b, w_down, ln2_g, ln2_b,
           w_ple_gate, w_ple_proj):
    B, S, D = x.shape
    depth = w_in.shape[0]
    alpha = (2 * depth) ** 0.25
    M = B * S
    xf = x.reshape(M, D)
    for i in range(depth):
        rank = w_gla_lr.shape[1]
        (proj_b, cols_b), (proj_f, cols_f), (proj_s, cols_s) = _in_proj(xf, w_in, i, rank)

        ff = proj_s[:, cols_s["ff"]:cols_s["ff"] + FOX_HEADS]
        c = _forget_cumsum(ff, b_forget[i], B, S)
        of = _fox(proj_b, c, B, S, cols_b, LANES)

        w_lr = jnp.pad(w_gla_lr[i], ((0, LANES - rank), (0, 0))).astype(BF16)
        og = _gla(proj_f, proj_b, proj_s, w_lr, b_gla_lr[i].reshape(1, -1),
                  gla_norm_g[i].reshape(1, -1), B, S, {**cols_f, **cols_b})

        merged = _merge(og, of, w_branch_gla[i].astype(BF16), w_branch_fox[i].astype(BF16),
                        proj_f, cols_f)
        x1, x1b = _proj_ln(merged, w_out[i].astype(BF16), xf, ln1_g[i].reshape(1, -1),
                           ln1_b[i].reshape(1, -1), alpha)
        h = _ffn_up(x1b, w_gate[i], w_up[i], conv_w[i], conv_b[i].reshape(1, -1), S)
        x2 = _ffn_down(h, w_down[i].astype(BF16), x1, ln2_g[i].reshape(1, -1),
                       ln2_b[i].reshape(1, -1), alpha)
        xf = _ple(x2, p[i].reshape(M, -1), w_ple_gate[i].astype(BF16),
                  w_ple_proj[i].astype(BF16))
    return xf.reshape(B, S, D)
```

```python
import functools

import jax
import jax.numpy as jnp
from jax import lax
from jax.experimental import pallas as pl
from jax.experimental.pallas import tpu as pltpu

F32 = jnp.float32
BF16 = jnp.bfloat16

GLA_HEADS = 4
GLA_TAU = 16.0
FOX_HEADS = 8
CONV_W = 3
LN_EPS = 1e-5
RMS_EPS = 1e-6
LOG2E = 1.4426950408889634

LANES = 128
SUBLANES = 8
MXU_COLS = 256
VMEM_BYTES_V7X = 64 * 1024 * 1024

IN_PROJ_TILE = 1024
GLA_CHUNK = 128
GLA_GROUP = 4
FOX_TILE = 512
FOX_GROUP = 4


def _vmem_limit(block_bytes, extra_bytes=0):
    est = 2 * block_bytes + extra_bytes + (4 << 20)
    return int(min(max(est, 16 << 20), VMEM_BYTES_V7X - (8 << 20)))


def _nbytes(shape, dtype):
    n = 1
    for s in shape:
        n *= s
    return n * jnp.dtype(dtype).itemsize


def _log_sigmoid(z):
    return jnp.minimum(z, 0.0) - jnp.log(1.0 + jnp.exp(-jnp.abs(z)))


def _layer_norm(y, g, b):
    mu = jnp.mean(y, axis=-1, keepdims=True)
    d = y - mu
    var = jnp.mean(d * d, axis=-1, keepdims=True)
    return d * lax.rsqrt(var + LN_EPS) * g + b


def _mm_nt_body(rows_ref, a_ref, wt_ref, cs_ref, o_ref, wb_ref):
    del rows_ref

    @pl.when(pl.program_id(1) == 0)
    def _():
        wb_ref[...] = wt_ref[0].astype(BF16)

    a = a_ref[...]
    for c in range(0, o_ref.shape[1], MXU_COLS):
        cols = slice(c, c + MXU_COLS)
        acc = lax.dot_general(a, wb_ref[cols, :], (((1,), (1,)), ((), ())),
                              preferred_element_type=F32)
        o_ref[:, cols] = (acc * cs_ref[:, cols]).astype(o_ref.dtype)


def _matmul_nt(a, w_t, layer, src_rows, col_scale, out_dtype, tm, tn, name):
    M, K = a.shape
    nt = len(src_rows)
    tm = min(tm, M)
    assert M % tm == 0 and col_scale.shape == (1, nt * tn)
    assert all(r % SUBLANES == 0 and r + tn <= w_t.shape[1] for r in src_rows)
    grid_spec = pltpu.PrefetchScalarGridSpec(
        num_scalar_prefetch=1,
        grid=(nt, M // tm),
        in_specs=[pl.BlockSpec((tm, K), lambda j, i, rows: (i, 0)),
                  pl.BlockSpec((pl.Element(1), pl.Element(tn), pl.Element(K)),
                               lambda j, i, rows: (layer, pl.multiple_of(rows[j], SUBLANES), 0)),
                  pl.BlockSpec((1, tn), lambda j, i, rows: (0, j))],
        out_specs=pl.BlockSpec((tm, tn), lambda j, i, rows: (i, j)),
        scratch_shapes=[pltpu.VMEM((tn, K), BF16)],
    )
    blocks = (_nbytes((tm, K), a.dtype) + _nbytes((tn, K), w_t.dtype)
              + _nbytes((tm, tn), out_dtype))
    return pl.pallas_call(
        _mm_nt_body,
        grid_spec=grid_spec,
        out_shape=jax.ShapeDtypeStruct((M, nt * tn), out_dtype),
        compiler_params=pltpu.CompilerParams(
            dimension_semantics=("parallel", "arbitrary"),
            vmem_limit_bytes=_vmem_limit(blocks, _nbytes((tn, K), BF16) + _nbytes((tm, tn), F32))),
        name=name,
    )(jnp.asarray(src_rows, jnp.int32), a, w_t, col_scale)


def _fcum_body(ff_ref, bias_ref, c_ref, *, groups):
    z = ff_ref[...] + bias_ref[...]
    x = _log_sigmoid(z)
    rows, lanes = x.shape
    lane = lax.broadcasted_iota(jnp.int32, x.shape, 1)
    s = 1
    while s < lanes:
        x = x + jnp.where(lane >= s, pltpu.roll(x, s, axis=1), 0.0)
        s *= 2
    tot = jnp.broadcast_to(x[:, lanes - 1:lanes], x.shape)
    grp = lax.broadcasted_iota(jnp.int32, x.shape, 0) % groups
    inc = tot
    s = 1
    while s < groups:
        inc = inc + jnp.where(grp >= s, pltpu.roll(inc, s, axis=0), 0.0)
        s *= 2
    c_ref[...] = x + (inc - tot)


def _forget_cumsum(ff, b_forget, B, S):
    H = ff.shape[1]
    groups = S // LANES
    ff_t = ff.reshape(B, S, H).transpose(0, 2, 1).reshape(B * H * groups, LANES)
    bias = jnp.broadcast_to(b_forget.reshape(1, H, 1, 1),
                            (B, H, groups, LANES)).reshape(B * H * groups, LANES)
    c = pl.pallas_call(
        functools.partial(_fcum_body, groups=groups),
        out_shape=jax.ShapeDtypeStruct(ff_t.shape, F32),
        name="fox_forget_cumsum",
    )(ff_t, bias)
    return c.reshape(B, H, S)


def _gla_body(q_ref, k_ref, v_ref, r_ref, lr_ref, wlr_ref, blr_ref, g_ref, xb_ref, wt_ref,
              o_ref, pm_ref, st_ref, wb_ref, *, C, DK, DV, G, row_tiles):
    c = pl.program_id(2)
    step = pl.program_id(0) * pl.num_programs(2) + c

    @pl.when(c == 0)
    def _():
        st_ref[...] = jnp.zeros_like(st_ref)

    @pl.when(step % row_tiles == 0)
    def _():
        wb_ref[...] = wt_ref[0].astype(BF16)

    def store_head(g):
        def store(value):
            o_ref[:, g * DV:(g + 1) * DV] = value.astype(o_ref.dtype)
        return store

    lr = lr_ref[...].astype(BF16)
    chains = [(_side_projection(xb_ref, wb_ref, pm_ref), None)]
    for g in range(G):
        kcols = slice(g * DK, (g + 1) * DK)
        vcols = slice(g * DV, (g + 1) * DV)
        chains.append((_gla_head(
            q_ref[:, kcols], k_ref[:, kcols], v_ref[:, vcols], r_ref[:, vcols], lr,
            wlr_ref[:, kcols], blr_ref[:, kcols], g_ref[...], st_ref.at[g],
            C=C, DK=DK, DV=DV), store_head(g)))
    while chains:
        for chain in list(chains):
            try:
                next(chain[0])
            except StopIteration as done:
                if chain[1] is not None:
                    chain[1](done.value)
                chains.remove(chain)


def _side_projection(xb_ref, wb_ref, out_ref, parts=2):
    pr = out_ref.shape[0] // parts
    for col in range(0, out_ref.shape[1], MXU_COLS):
        for r in range(parts):
            rows = slice(r * pr, (r + 1) * pr)
            out_ref[rows, col:col + MXU_COLS] = lax.dot_general(
                xb_ref[rows, :], wb_ref[col:col + MXU_COLS, :], (((1,), (1,)), ((), ())),
                preferred_element_type=F32).astype(out_ref.dtype)
            yield


def _gla_head(q, k, v, gate, lr, wlr, blr, norm_g, st_ref, *, C, DK, DV):
    z = jnp.dot(lr, wlr, preferred_element_type=F32) + blr
    yield
    la = _log_sigmoid(z) * (LOG2E / GLA_TAU)

    row = lax.broadcasted_iota(jnp.int32, (C, DK), 0)
    ti = lax.broadcasted_iota(jnp.int32, (C, C), 0)
    si = lax.broadcasted_iota(jnp.int32, (C, C), 1)
    tril = jnp.where(si <= ti, 1.0, 0.0).astype(BF16)
    hi = la.astype(BF16)
    rest = la - hi.astype(F32)
    mid = rest.astype(BF16)
    lo = (rest - mid.astype(F32)).astype(BF16)
    yield
    b = (jnp.dot(tril, hi, preferred_element_type=F32)
         + jnp.dot(tril, mid, preferred_element_type=F32)
         + jnp.dot(tril, lo, preferred_element_type=F32))
    yield

    st = st_ref[...]
    qe = (q * jnp.exp2(b)).astype(BF16)
    inter = lax.dot_general(qe, st.astype(BF16), (((1,), (1,)), ((), ())),
                            preferred_element_type=F32)
    blast = b[C - 1:C, :]
    kd = (k * jnp.exp2(blast - b)).astype(BF16)
    yield

    pair =jnp.where(ti > si, ti ^ si, 0)
    attn = jnp.where(ti == si, jnp.sum(q * k, axis=-1, keepdims=True), 0.0)
    NB = C // SUBLANES
    b3 = b.reshape(NB, SUBLANES, DK)
    sub3 = lax.broadcasted_iota(jnp.int32, (NB, SUBLANES, DK), 1)
    m = C // 2
    while m >= 1:
        blk = 2 * m
        in_b = (row & (blk - 1)) >= m
        if m == 1:
            e = jnp.exp2(la)
        else:
            if blk >= SUBLANES:
                nblk = C // blk
                r = jnp.broadcast_to(b.reshape(nblk, blk, DK)[:, m - 1:m, :],
                                     (nblk, blk, DK)).reshape(C, DK)
            else:
                r3 = None
                for start in range(0, SUBLANES, blk):
                    cand = jnp.broadcast_to(b3[:, start + m - 1:start + m, :], b3.shape)
                    r3 = cand if r3 is None else jnp.where(sub3 >= start, cand, r3)
                r = r3.reshape(C, DK)
            e = jnp.exp2(-jnp.abs(b - r))
        x = jnp.where(in_b, q, k)
        x = (jnp.where(in_b, x * e, x) if m == 1 else x * e).astype(BF16)
        a = lax.dot_general(x, x, (((1,), (1,)), ((), ())),
                            preferred_element_type=F32)
        attn = jnp.where((pair >> (m.bit_length() - 1)) == 1, a, attn)
        m //= 2
        yield

    o = inter + jnp.dot(attn.astype(BF16), v, preferred_element_type=F32)
    yield

    upd = lax.dot_general(v, kd, (((0,), (0,)), ((), ())),
                          preferred_element_type=F32)
    st_ref[...] = st * jnp.exp2(blast) + upd
    yield

    ms = jnp.mean(o * o, axis=-1, keepdims=True)
    on = o * lax.rsqrt(ms + RMS_EPS) * norm_g
    return on * (gate * jax.nn.sigmoid(gate))


def _gla(proj_f, proj_b, proj_s, w_lr, b_lr, norm_g, B, S, cols, xb, w_t, layer, side_rows,
         side_tile=(1024, 512)):
    H = GLA_HEADS
    G = GLA_GROUP
    DK = w_lr.shape[1] // H
    DV = norm_g.shape[1]
    C = min(GLA_CHUNK, S)
    NC = S // C
    WK, WV = G * DK, G * DV
    M, K = xb.shape
    r0, n_side = side_rows
    tm, tn = min(side_tile[0], M), side_tile[1]
    TI = M // tm
    assert S % C == 0 and C % (2 * SUBLANES) == 0 and H == G
    assert M % tm == 0 and n_side % tn == 0 and TI * (n_side // tn) == B * NC
    assert r0 % SUBLANES == 0 and r0 + n_side <= w_t.shape[1]
    assert all(cols[n] % WK == 0 for n in ("gq", "gk")) and all(cols[n] % WV == 0 for n in ("gr", "gv"))
    qo, ko, ro, vo = (cols["gq"] // WK, cols["gk"] // WK, cols["gr"] // WV, cols["gv"] // WV)
    rows = lambda b, h, c: b * NC + c
    blocks = (2 * _nbytes((C, WK), F32) + _nbytes((C, WV), BF16) + _nbytes((C, WV), F32)
              + _nbytes((C, LANES), F32) + _nbytes((LANES, WK), BF16) + _nbytes((C, WV), BF16)
              + _nbytes((tm, K), BF16) + _nbytes((tn, K), F32) + _nbytes((tm, tn), F32))
    return pl.pallas_call(
        functools.partial(_gla_body, C=C, DK=DK, DV=DV, G=G, row_tiles=TI),
        grid=(B, H // G, NC),
        in_specs=[
            pl.BlockSpec((C, WK), lambda b, h, c: (rows(b, h, c), qo + h)),
            pl.BlockSpec((C, WK), lambda b, h, c: (rows(b, h, c), ko + h)),
            pl.BlockSpec((C, WV), lambda b, h, c: (rows(b, h, c), vo + h)),
            pl.BlockSpec((C, WV), lambda b, h, c: (rows(b, h, c), ro + h)),
            pl.BlockSpec((C, LANES), lambda b, h, c: (rows(b, h, c), 0)),
            pl.BlockSpec((LANES, WK), lambda b, h, c: (0, h)),
            pl.BlockSpec((1, WK), lambda b, h, c: (0, h)),
            pl.BlockSpec((1, DV), lambda b, h, c: (0, 0)),
            pl.BlockSpec((tm, K), lambda b, h, c: (rows(b, h, c) % TI, 0)),
            pl.BlockSpec((pl.Element(1), pl.Element(tn), pl.Element(K)),
                         lambda b, h, c: (layer, pl.multiple_of(
                             r0 + tn * (rows(b, h, c) // TI), SUBLANES), 0)),
        ],
        out_specs=[pl.BlockSpec((C, WV), lambda b, h, c: (rows(b, h, c), h)),
                   pl.BlockSpec((tm, tn), lambda b, h, c: (rows(b, h, c) % TI,
                                                           rows(b, h, c) // TI))],
        out_shape=[jax.ShapeDtypeStruct((B * S, H * DV), BF16),
                   jax.ShapeDtypeStruct((M, n_side), F32)],
        scratch_shapes=[pltpu.VMEM((G, DV, DK), F32), pltpu.VMEM((tn, K), BF16)],
        compiler_params=pltpu.CompilerParams(
            dimension_semantics=("arbitrary", "arbitrary", "arbitrary"),
            vmem_limit_bytes=_vmem_limit(blocks, _nbytes((tn, K), BF16) + G * (12 << 20))),
        name="gla_chunked",
    )(proj_f, proj_f, proj_b, proj_f, proj_s, w_lr, b_lr, norm_g, xb, w_t)


def _fox_body(q_ref, k_ref, v_ref, c_ref, o_ref, m_ref, l_ref, acc_ref, cqb_ref, *, T, G):
    qi = pl.program_id(2)
    lane_tiles = T // LANES
    for g in range(G):
        cqb_ref[g] = jnp.broadcast_to(c_ref[g, qi] * LOG2E, (LANES, T)).T
    m_ref[...] = jnp.full_like(m_ref, -jnp.inf)
    l_ref[...] = jnp.zeros_like(l_ref)
    acc_ref[...] = jnp.zeros_like(acc_ref)

    def qk(ki):
        start = pl.multiple_of(ki * T, T)
        return [lax.dot_general(
            q_ref[:, g * LANES:(g + 1) * LANES], k_ref[pl.ds(start, T), g * LANES:(g + 1) * LANES],
            (((1,), (1,)), ((), ())), preferred_element_type=F32) for g in range(G)]

    def softmax_pv(ki, scores, on_diagonal):
        start = pl.multiple_of(ki * T, T)
        for g in range(G):
            head = slice(g * LANES, (g + 1) * LANES)
            t = scores[g] - c_ref[g, ki] * LOG2E
            if on_diagonal:
                ti = lax.broadcasted_iota(jnp.int32, (T, T), 0)
                si = lax.broadcasted_iota(jnp.int32, (T, T), 1)
                t = jnp.where(si <= ti, t, -jnp.inf)
            cq = cqb_ref[g]
            m_prev = m_ref[g]
            m_new = jnp.maximum(m_prev, cq + jnp.max(t, axis=-1, keepdims=True))
            p = jnp.exp2(t + jnp.concatenate([cq - m_new] * lane_tiles, axis=1))
            alpha = jnp.exp2(m_prev - m_new)
            l_ref[g] = alpha * l_ref[g] + jnp.sum(p, axis=-1, keepdims=True)
            acc_ref[g] = alpha * acc_ref[g] + jnp.dot(
                p.astype(BF16), v_ref[pl.ds(start, T), head], preferred_element_type=F32)
            m_ref[g] = m_new

    def pair(kp, carry):
        sa = qk(2 * kp)
        sb = qk(2 * kp + 1)
        softmax_pv(2 * kp, sa, False)
        softmax_pv(2 * kp + 1, sb, False)
        return carry

    lax.fori_loop(0, qi // 2, pair, 0)

    @pl.when(qi % 2 == 1)
    def _():
        softmax_pv(qi - 1, qk(qi - 1), False)

    softmax_pv(qi, qk(qi), True)
    for g in range(G):
        o_ref[:, g * LANES:(g + 1) * LANES] = (acc_ref[g] / l_ref[g]).astype(o_ref.dtype)


def _fox(proj_b, c, B, S, cols, dh):
    H = FOX_HEADS
    G = FOX_GROUP
    T = min(FOX_TILE, S)
    NT = S // T
    W = G * dh
    assert S % T == 0 and H % G == 0 and dh == LANES
    qo, ko, vo = cols["fq"] // W, cols["fk"] // W, cols["fv"] // W
    assert all(cols[n] % W == 0 for n in ("fq", "fk", "fv"))
    c_row = c.reshape(B, H, NT, 1, T)
    blocks = (2 * _nbytes((T, W), BF16) + 2 * _nbytes((S, W), BF16)
              + G * _nbytes((NT, SUBLANES, T), F32))
    return pl.pallas_call(
        functools.partial(_fox_body, T=T, G=G),
        grid=(B, H // G, NT),
        in_specs=[
            pl.BlockSpec((T, W), lambda b, h, qi: (b * NT + qi, qo + h)),
            pl.BlockSpec((S, W), lambda b, h, qi: (b, ko + h)),
            pl.BlockSpec((S, W), lambda b, h, qi: (b, vo + h)),
            pl.BlockSpec((None, G, NT, 1, T), lambda b, h, qi: (b, h, 0, 0, 0)),
        ],
        out_specs=pl.BlockSpec((T, W), lambda b, h, qi: (b * NT + qi, h)),
        out_shape=jax.ShapeDtypeStruct((B * S, H * dh), BF16),
        scratch_shapes=[pltpu.VMEM((G, T, LANES), F32), pltpu.VMEM((G, T, LANES), F32),
                        pltpu.VMEM((G, T, dh), F32), pltpu.VMEM((G, T, LANES), F32)],
        compiler_params=pltpu.CompilerParams(
            dimension_semantics=("parallel", "parallel", "arbitrary"),
            vmem_limit_bytes=_vmem_limit(blocks, (4 * G + 8) * _nbytes((T, T), F32))),
        name="fox_attention",
    )(proj_b, proj_b, proj_b, c_row)


def _merge_body(og_ref, of_ref, wg_ref, wf_ref, ma_ref, mb_ref, o_ref):
    og = og_ref[...]
    of = of_ref[...]
    for c in range(0, o_ref.shape[1], MXU_COLS):
        cols = slice(c, c + MXU_COLS)
        yg = jnp.dot(og, wg_ref[:, cols], preferred_element_type=F32)
        yf = jnp.dot(of, wf_ref[:, cols], preferred_element_type=F32)
        o_ref[:, cols] = (jax.nn.sigmoid(ma_ref[:, cols]) * yg
                          + jax.nn.sigmoid(mb_ref[:, cols]) * yf).astype(o_ref.dtype)


def _merge(og, of, w_bg, w_bf, proj_f, cols, tm=1024, tn=1024):
    M, KG = og.shape
    KF = of.shape[1]
    N = w_bg.shape[1]
    tm, tn = min(tm, M), min(tn, N)
    ao, bo = cols["ma"] // tn, cols["mb"] // tn
    assert cols["ma"] % tn == 0 and cols["mb"] % tn == 0
    blocks = (_nbytes((tm, KG + KF), BF16) + _nbytes((KG + KF, tn), BF16)
              + 2 * _nbytes((tm, tn), F32) + _nbytes((tm, tn), BF16))
    return pl.pallas_call(
        _merge_body,
        grid=(N // tn, M // tm),
        in_specs=[
            pl.BlockSpec((tm, KG), lambda j, i: (i, 0)),
            pl.BlockSpec((tm, KF), lambda j, i: (i, 0)),
            pl.BlockSpec((KG, tn), lambda j, i: (0, j)),
            pl.BlockSpec((KF, tn), lambda j, i: (0, j)),
            pl.BlockSpec((tm, tn), lambda j, i: (i, ao + j)),
            pl.BlockSpec((tm, tn), lambda j, i: (i, bo + j)),
        ],
        out_specs=pl.BlockSpec((tm, tn), lambda j, i: (i, j)),
        out_shape=jax.ShapeDtypeStruct((M, N), BF16),
        compiler_params=pltpu.CompilerParams(
            dimension_semantics=("parallel", "parallel"),
            vmem_limit_bytes=_vmem_limit(blocks, 3 * _nbytes((tm, tn), F32))),
        name="branch_merge",
    )(og, of, w_bg, w_bf, proj_f, proj_f)


def _proj_ln_body(a_ref, w_ref, x_ref, g_ref, b_ref, o_ref, ob_ref, *, alpha, parts):
    pr = o_ref.shape[0] // parts
    ys = [jnp.dot(a_ref[r * pr:(r + 1) * pr, :], w_ref[...], preferred_element_type=F32)
          for r in range(parts)]
    for r in range(parts):
        rows = slice(r * pr, (r + 1) * pr)
        out = _layer_norm(alpha * x_ref[rows, :] + ys[r], g_ref[...], b_ref[...])
        o_ref[rows, :] = out
        ob_ref[rows, :] = out.astype(BF16)


def _proj_ln(a, w, resid, g, b, alpha, tm=512):
    M, K = a.shape
    N = w.shape[1]
    tm = min(tm, M)
    blocks = (_nbytes((tm, K), BF16) + _nbytes((K, N), BF16) + 2 * _nbytes((tm, N), F32)
              + _nbytes((tm, N), BF16))
    return pl.pallas_call(
        functools.partial(_proj_ln_body, alpha=alpha, parts=2),
        grid=(M // tm,),
        in_specs=[
            pl.BlockSpec((tm, K), lambda i: (i, 0)),
            pl.BlockSpec((K, N), lambda i: (0, 0)),
            pl.BlockSpec((tm, N), lambda i: (i, 0)),
            pl.BlockSpec((1, N), lambda i: (0, 0)),
            pl.BlockSpec((1, N), lambda i: (0, 0)),
        ],
        out_specs=[pl.BlockSpec((tm, N), lambda i: (i, 0)),
                   pl.BlockSpec((tm, N), lambda i: (i, 0))],
        out_shape=[jax.ShapeDtypeStruct((M, N), F32), jax.ShapeDtypeStruct((M, N), BF16)],
        compiler_params=pltpu.CompilerParams(
            dimension_semantics=("parallel",),
            vmem_limit_bytes=_vmem_limit(blocks, 3 * _nbytes((tm, N), F32))),
        name="out_proj_layernorm",
    )(a, w, resid, g, b)


def _ffn_up_body(x_ref, wg_ref, wu_ref, cw_ref, cb_ref, h_ref, wgb_ref, wub_ref, gbuf_ref,
                 *, tm, tiles_per_seq, pr, pc):
    i = pl.program_id(1)
    HALO = SUBLANES
    tn = h_ref.shape[1]

    @pl.when(i == 0)
    def _():
        wgb_ref[...] = wg_ref[...].astype(BF16)
        wub_ref[...] = wu_ref[...].astype(BF16)

    @pl.when(i % tiles_per_seq == 0)
    def _():
        gbuf_ref[0:HALO, :] = jnp.zeros((HALO, tn), F32)

    cw = cw_ref[...]
    cb = cb_ref[...]
    parts = [(r, c) for c in range(tn // pc) for r in range(tm // pr)]

    def matmuls(r, c):
        x = x_ref[r * pr:(r + 1) * pr, :]
        cols = slice(c * pc, (c + 1) * pc)
        g = jnp.dot(x, wgb_ref[:, cols], preferred_element_type=F32)
        u = jnp.dot(x, wub_ref[:, cols], preferred_element_type=F32)
        gbuf_ref[HALO + r * pr:HALO + (r + 1) * pr, cols] = g
        return g, u

    def finish(r, c, g, u):
        cols = slice(c * pc, (c + 1) * pc)
        y = cb[:, cols] + cw[CONV_W - 1:CONV_W, cols] * g
        for j in range(CONV_W - 1):
            off = HALO - (CONV_W - 1) + j + r * pr
            y = y + cw[j:j + 1, cols] * gbuf_ref[off:off + pr, cols]
        h_ref[r * pr:(r + 1) * pr, cols] = (jax.nn.gelu(y, approximate=True) * u).astype(h_ref.dtype)

    pending = None
    for r, c in parts:
        gu = matmuls(r, c)
        if pending is not None:
            finish(*pending)
        pending = (r, c) + gu
    finish(*pending)
    gbuf_ref[0:HALO, :] = gbuf_ref[tm:tm + HALO, :]


def _ffn_up(xb, w_gate, w_up, conv_w, conv_b, S, tm=1024, tn=512, pr=512, pc=256):
    M, K = xb.shape
    N = w_gate.shape[1]
    tm, tn = min(tm, S), min(tn, N)
    pr, pc = min(pr, tm), min(pc, tn)
    assert S % tm == 0 and N % tn == 0 and tm % pr == 0 and tn % pc == 0
    blocks = (_nbytes((tm, K), BF16) + 2 * _nbytes((K, tn), F32) + _nbytes((tm, tn), BF16)
              + _nbytes((8, tn), F32) * 2)
    scratch = 2 * _nbytes((K, tn), BF16) + _nbytes((tm + SUBLANES, tn), F32)
    return pl.pallas_call(
        functools.partial(_ffn_up_body, tm=tm, tiles_per_seq=S // tm, pr=pr, pc=pc),
        grid=(N // tn, M // tm),
        in_specs=[
            pl.BlockSpec((tm, K), lambda j, i: (i, 0)),
            pl.BlockSpec((K, tn), lambda j, i: (0, j)),
            pl.BlockSpec((K, tn), lambda j, i: (0, j)),
            pl.BlockSpec((CONV_W, tn), lambda j, i: (0, j)),
            pl.BlockSpec((1, tn), lambda j, i: (0, j)),
        ],
        out_specs=pl.BlockSpec((tm, tn), lambda j, i: (i, j)),
        out_shape=jax.ShapeDtypeStruct((M, N), BF16),
        scratch_shapes=[pltpu.VMEM((K, tn), BF16), pltpu.VMEM((K, tn), BF16),
                        pltpu.VMEM((tm + SUBLANES, tn), F32)],
        compiler_params=pltpu.CompilerParams(
            dimension_semantics=("parallel", "arbitrary"),
            vmem_limit_bytes=_vmem_limit(blocks, scratch + 8 * _nbytes((pr, pc), F32))),
        name="ffn_gate_up",
    )(xb, w_gate, w_up, conv_w, conv_b)


def _ffn_down_body(h_ref, w_ref, x_ref, g_ref, b_ref, o_ref, *, alpha):
    kk = pl.program_id(1)

    @pl.when(kk == 0)
    def _():
        o_ref[...] = alpha * x_ref[...]

    o_ref[...] += jnp.dot(h_ref[...], w_ref[...], preferred_element_type=F32)

    @pl.when(kk == pl.num_programs(1) - 1)
    def _():
        o_ref[...] = _layer_norm(o_ref[...], g_ref[...], b_ref[...])


def _ffn_down(h, w_down, resid, g, b, alpha, tm=512, tk=2816):
    M, K = h.shape
    N = w_down.shape[1]
    tm, tk = min(tm, M), min(tk, K)
    assert M % tm == 0 and K % tk == 0
    blocks = (_nbytes((tm, tk), BF16) + _nbytes((tk, N), BF16) + 2 * _nbytes((tm, N), F32))
    return pl.pallas_call(
        functools.partial(_ffn_down_body, alpha=alpha),
        grid=(M // tm, K // tk),
        in_specs=[
            pl.BlockSpec((tm, tk), lambda i, kk: (i, kk)),
            pl.BlockSpec((tk, N), lambda i, kk: (kk, 0)),
            pl.BlockSpec((tm, N), lambda i, kk: (i, 0)),
            pl.BlockSpec((1, N), lambda i, kk: (0, 0)),
            pl.BlockSpec((1, N), lambda i, kk: (0, 0)),
        ],
        out_specs=pl.BlockSpec((tm, N), lambda i, kk: (i, 0)),
        out_shape=jax.ShapeDtypeStruct((M, N), F32),
        compiler_params=pltpu.CompilerParams(
            dimension_semantics=("parallel", "arbitrary"),
            vmem_limit_bytes=_vmem_limit(blocks, _nbytes((tm, N), F32))),
        name="ffn_down_layernorm",
    )(h, w_down, resid, g, b)


def _ple_body(x_ref, wg_ref, p_ref, wp_ref, o_ref, xb_ref, pb_ref):
    j = pl.program_id(1)
    tn = o_ref.shape[1]

    @pl.when(j == 0)
    def _():
        xb_ref[...] = x_ref[...].astype(BF16)
        pb_ref[...] = p_ref[...].astype(BF16)

    xb = xb_ref[...]
    pb = pb_ref[...]
    for c in range(0, tn, MXU_COLS):
        cols = slice(c, c + MXU_COLS)
        gate = jax.nn.sigmoid(jnp.dot(xb, wg_ref[:, cols], preferred_element_type=F32))
        emb = jnp.dot(pb, wp_ref[:, cols], preferred_element_type=F32)
        resid = x_ref[:, pl.ds(pl.multiple_of(j * tn + c, MXU_COLS), MXU_COLS)]
        o_ref[:, cols] = resid + gate * emb


def _ple(x, p, w_gate, w_proj, tm=1024, tn=1024):
    M, K = x.shape
    N = w_gate.shape[1]
    P = p.shape[1]
    tm, tn = min(tm, M), min(tn, N)
    assert K == N and M % tm == 0 and N % tn == 0
    blocks = (_nbytes((tm, K), F32) + _nbytes((K, tn), BF16) + _nbytes((tm, P), F32)
              + _nbytes((P, tn), BF16) + _nbytes((tm, tn), F32))
    scratch = _nbytes((tm, K), BF16) + _nbytes((tm, P), BF16)
    return pl.pallas_call(
        _ple_body,
        grid=(M // tm, N // tn),
        in_specs=[
            pl.BlockSpec((tm, K), lambda i, j: (i, 0)),
            pl.BlockSpec((K, tn), lambda i, j: (0, j)),
            pl.BlockSpec((tm, P), lambda i, j: (i, 0)),
            pl.BlockSpec((P, tn), lambda i, j: (0, j)),
        ],
        out_specs=pl.BlockSpec((tm, tn), lambda i, j: (i, j)),
        out_shape=jax.ShapeDtypeStruct((M, N), F32),
        scratch_shapes=[pltpu.VMEM((tm, K), BF16), pltpu.VMEM((tm, P), BF16)],
        compiler_params=pltpu.CompilerParams(
            dimension_semantics=("parallel", "arbitrary"),
            vmem_limit_bytes=_vmem_limit(blocks, scratch + 3 * _nbytes((tm, tn), F32))),
        name="ple_gate",
    )(x, w_gate, p, w_proj)


def _cast_small_body(x_ref, w_ref, xb_ref, o_ref):
    xb = x_ref[...].astype(BF16)
    xb_ref[...] = xb
    o_ref[...] = lax.dot_general(xb, w_ref[...].astype(BF16), (((1,), (1,)), ((), ())),
                                 preferred_element_type=F32)


def _cast_and_small_proj(x, w_small, tm=1024):
    M, K = x.shape
    ns = w_small.shape[0]
    tm = min(tm, M)
    assert M % tm == 0
    blocks = _nbytes((tm, K), F32) + _nbytes((ns, K), F32) + _nbytes((tm, K), BF16) \
        + _nbytes((tm, ns), F32)
    return pl.pallas_call(
        _cast_small_body,
        grid=(M // tm,),
        in_specs=[pl.BlockSpec((tm, K), lambda i: (i, 0)),
                  pl.BlockSpec((ns, K), lambda i: (0, 0))],
        out_specs=[pl.BlockSpec((tm, K), lambda i: (i, 0)),
                   pl.BlockSpec((tm, ns), lambda i: (i, 0))],
        out_shape=[jax.ShapeDtypeStruct((M, K), BF16), jax.ShapeDtypeStruct((M, ns), F32)],
        compiler_params=pltpu.CompilerParams(
            dimension_semantics=("parallel",),
            vmem_limit_bytes=_vmem_limit(blocks, _nbytes((tm, K), BF16))),
        name="in_proj_small",
    )(x, w_small)


def _in_proj(x, w_in, layer, rank):
    M, D = x.shape
    gla_qk = D // 2
    fox_w = FOX_HEADS * LANES
    names = ("gq", "gk", "gv", "gr", "glr", "fq", "fk", "fv", "ff", "ma", "mb")
    widths = (gla_qk, gla_qk, D, D, rank, fox_w, fox_w, fox_w, FOX_HEADS, D, D)
    assert sum(widths) == w_in.shape[2]
    width, offset, off = {}, {}, 0
    for n, wd in zip(names, widths):
        width[n], offset[n] = wd, off
        off += wd
    w_t = jnp.swapaxes(w_in, 1, 2)

    out_scale = {"gq": (gla_qk // GLA_HEADS) ** -0.5, "fq": LANES ** -0.5 * LOG2E}
    T = IN_PROJ_TILE

    def project(order, out_dtype, name):
        src_rows, scales, cols, o = [], [], {}, 0
        for n in order:
            assert width[n] % T == 0
            cols[n] = o
            o += width[n]
            src_rows += [offset[n] + t for t in range(0, width[n], T)]
            scales.append(jnp.full((1, width[n]), out_scale.get(n, 1.0), F32))
        out = _matmul_nt(xb, w_t, layer, src_rows, jnp.concatenate(scales, axis=1),
                         out_dtype, 1024, T, name)
        return out, cols

    small, cols_s = [], {}
    for t, n in enumerate(("glr", "ff")):
        wseg = w_t[layer, offset[n]:offset[n] + width[n], :]
        small.append(jnp.pad(wseg, ((0, LANES - width[n]), (0, 0))))
        cols_s[n] = t * LANES
    w_small = jnp.concatenate(small, axis=0)
    xb, proj_s = _cast_and_small_proj(x, w_small)

    proj_b = project(("gv", "fq", "fk", "fv"), BF16, "in_proj_bf16")
    proj_f = project(("gq", "gk", "gr"), F32, "in_proj_f32")
    assert offset["mb"] == offset["ma"] + width["ma"]
    gates = (xb, w_t, (offset["ma"], width["ma"] + width["mb"]), {"ma": 0, "mb": width["ma"]})
    return proj_b, proj_f, (proj_s, cols_s), gates


def kernel(x, p, w_in, w_gla_lr, b_gla_lr, gla_norm_g, b_forget, w_branch_gla, w_branch_fox,
           w_out, ln1_g, ln1_b, w_gate, w_up, conv_w, conv_b, w_down, ln2_g, ln2_b,
           w_ple_gate, w_ple_proj):
    B, S, D = x.shape
    depth = w_in.shape[0]
    alpha = (2 * depth) ** 0.25
    M = B * S
    xf = x.reshape(M, D)
    for i in range(depth):
        rank = w_gla_lr.shape[1]
        (proj_b, cols_b), (proj_f, cols_f), (proj_s, cols_s), gates = _in_proj(xf, w_in, i, rank)
        xb, w_t, gate_rows, cols_m = gates

        ff = proj_s[:, cols_s["ff"]:cols_s["ff"] + FOX_HEADS]
        c = _forget_cumsum(ff, b_forget[i], B, S)
        of = _fox(proj_b, c, B, S, cols_b, LANES)

        w_lr = jnp.pad(w_gla_lr[i], ((0, LANES - rank), (0, 0))).astype(BF16)
        og, proj_m = _gla(proj_f, proj_b, proj_s, w_lr, b_gla_lr[i].reshape(1, -1),
                          gla_norm_g[i].reshape(1, -1), B, S, {**cols_f, **cols_b},
                          xb, w_t, i, gate_rows)

        merged = _merge(og, of, w_branch_gla[i].astype(BF16), w_branch_fox[i].astype(BF16),
                        proj_m, cols_m)
        x1, x1b = _proj_ln(merged, w_out[i].astype(BF16), xf, ln1_g[i].reshape(1, -1),
                           ln1_b[i].reshape(1, -1), alpha)
        h = _ffn_up(x1b, w_gate[i], w_up[i], conv_w[i], conv_b[i].reshape(1, -1), S)
        x2 = _ffn_down(h, w_down[i].astype(BF16), x1, ln2_g[i].reshape(1, -1),
                       ln2_b[i].reshape(1, -1), alpha)
        xf = _ple(x2, p[i].reshape(M, -1), w_ple_gate[i].astype(BF16),
                  w_ple_proj[i].astype(BF16))
    return xf.reshape(B, S, D)
```

```python
import functools

import jax
import jax.numpy as jnp
from jax import lax
from jax.experimental import pallas as pl
from jax.experimental.pallas import tpu as pltpu

F32 = jnp.float32
BF16 = jnp.bfloat16

GLA_HEADS = 4
GLA_TAU = 16.0
FOX_HEADS = 8
CONV_W = 3
LN_EPS = 1e-5
RMS_EPS = 1e-6
LOG2E = 1.4426950408889634

LANES = 128
SUBLANES = 8
MXU_COLS = 256
VMEM_BYTES_V7X = 64 * 1024 * 1024

IN_PROJ_TILE = 1024
GLA_CHUNK = 128
GLA_GROUP = 4
FOX_TILE = 512
FOX_GROUP = 4


def _vmem_limit(block_bytes, extra_bytes=0):
    est = 2 * block_bytes + extra_bytes + (4 << 20)
    return int(min(max(est, 16 << 20), VMEM_BYTES_V7X - (8 << 20)))


def _nbytes(shape, dtype):
    n = 1
    for s in shape:
        n *= s
    return n * jnp.dtype(dtype).itemsize


def _log_sigmoid(z):
    return jnp.minimum(z, 0.0) - jnp.log(1.0 + jnp.exp(-jnp.abs(z)))


def _layer_norm(y, g, b):
    mu = jnp.mean(y, axis=-1, keepdims=True)
    d = y - mu
    var = jnp.mean(d * d, axis=-1, keepdims=True)
    return d * lax.rsqrt(var + LN_EPS) * g + b


def _mm_nt_body(rows_ref, a_ref, wt_ref, cs_ref, *refs, n_side):
    del rows_ref
    side_in, o_ref, side_out, wb_ref = (refs[:n_side], refs[n_side],
                                        refs[n_side + 1:2 * n_side + 1], refs[-1])

    @pl.when(pl.program_id(1) == 0)
    def _():
        wb_ref[...] = wt_ref[0].astype(BF16)

    a = a_ref[...]
    for c in range(0, o_ref.shape[1], MXU_COLS):
        cols = slice(c, c + MXU_COLS)
        acc = lax.dot_general(a, wb_ref[cols, :], (((1,), (1,)), ((), ())),
                              preferred_element_type=F32)
        o_ref[:, cols] = (acc * cs_ref[:, cols]).astype(o_ref.dtype)
    for src, dst in zip(side_in, side_out):
        dst[...] = src[...].astype(dst.dtype)


def _matmul_nt(a, w_t, layer, src_rows, col_scale, out_dtype, tm, tn, name, side_casts=()):
    M, K = a.shape
    nt = len(src_rows)
    tm = min(tm, M)
    assert M % tm == 0 and col_scale.shape == (1, nt * tn)
    assert all(r % SUBLANES == 0 and r + tn <= w_t.shape[1] for r in src_rows)
    TI = M // tm
    steps = nt * TI
    BF16_ROWS = 2 * SUBLANES
    assert all(s.shape[0] % (steps * BF16_ROWS) == 0 for s in side_casts)
    slab = lambda s: (s.shape[0] // steps, s.shape[1])
    side_specs = [pl.BlockSpec(slab(s), lambda j, i, rows: (j * TI + i, 0)) for s in side_casts]
    grid_spec = pltpu.PrefetchScalarGridSpec(
        num_scalar_prefetch=1,
        grid=(nt, TI),
        in_specs=[pl.BlockSpec((tm, K), lambda j, i, rows: (i, 0)),
                  pl.BlockSpec((pl.Element(1), pl.Element(tn), pl.Element(K)),
                               lambda j, i, rows: (layer, pl.multiple_of(rows[j], SUBLANES), 0)),
                  pl.BlockSpec((1, tn), lambda j, i, rows: (0, j))] + side_specs,
        out_specs=[pl.BlockSpec((tm, tn), lambda j, i, rows: (i, j))] + side_specs,
        scratch_shapes=[pltpu.VMEM((tn, K), BF16)],
    )
    blocks = (_nbytes((tm, K), a.dtype) + _nbytes((tn, K), w_t.dtype)
              + _nbytes((tm, tn), out_dtype)
              + sum(_nbytes(slab(s), F32) + _nbytes(slab(s), BF16) for s in side_casts))
    outs = pl.pallas_call(
        functools.partial(_mm_nt_body, n_side=len(side_casts)),
        grid_spec=grid_spec,
        out_shape=[jax.ShapeDtypeStruct((M, nt * tn), out_dtype)]
        + [jax.ShapeDtypeStruct(s.shape, BF16) for s in side_casts],
        compiler_params=pltpu.CompilerParams(
            dimension_semantics=("arbitrary", "arbitrary"),
            vmem_limit_bytes=_vmem_limit(blocks, _nbytes((tn, K), BF16) + _nbytes((tm, tn), F32))),
        name=name,
    )(jnp.asarray(src_rows, jnp.int32), a, w_t, col_scale, *side_casts)
    return outs[0], list(outs[1:])


def _fcum_body(ff_ref, bias_ref, c_ref, *, groups):
    z = ff_ref[...] + bias_ref[...]
    x = _log_sigmoid(z)
    rows, lanes = x.shape
    lane = lax.broadcasted_iota(jnp.int32, x.shape, 1)
    s = 1
    while s < lanes:
        x = x + jnp.where(lane >= s, pltpu.roll(x, s, axis=1), 0.0)
        s *= 2
    tot = jnp.broadcast_to(x[:, lanes - 1:lanes], x.shape)
    grp = lax.broadcasted_iota(jnp.int32, x.shape, 0) % groups
    inc = tot
    s = 1
    while s < groups:
        inc = inc + jnp.where(grp >= s, pltpu.roll(inc, s, axis=0), 0.0)
        s *= 2
    c_ref[...] = x + (inc - tot)


def _forget_cumsum(ff, b_forget, B, S):
    H = ff.shape[1]
    groups = S // LANES
    ff_t = ff.reshape(B, S, H).transpose(0, 2, 1).reshape(B * H * groups, LANES)
    bias = jnp.broadcast_to(b_forget.reshape(1, H, 1, 1),
                            (B, H, groups, LANES)).reshape(B * H * groups, LANES)
    c = pl.pallas_call(
        functools.partial(_fcum_body, groups=groups),
        out_shape=jax.ShapeDtypeStruct(ff_t.shape, F32),
        name="fox_forget_cumsum",
    )(ff_t, bias)
    return c.reshape(B, H, S)


def _gla_body(q_ref, k_ref, v_ref, r_ref, lr_ref, wlr_ref, blr_ref, g_ref, xb_ref, wt_ref,
              o_ref, pm_ref, st_ref, wb_ref, *, C, DK, DV, G, row_tiles):
    c = pl.program_id(2)
    step = pl.program_id(0) * pl.num_programs(2) + c

    @pl.when(c == 0)
    def _():
        st_ref[...] = jnp.zeros_like(st_ref)

    @pl.when(step % row_tiles == 0)
    def _():
        wb_ref[...] = wt_ref[0].astype(BF16)

    def store_head(g):
        def store(value):
            o_ref[:, g * DV:(g + 1) * DV] = value.astype(o_ref.dtype)
        return store

    lr = lr_ref[...].astype(BF16)
    chains = [(_side_projection(xb_ref, wb_ref, pm_ref), None)]
    for g in range(G):
        kcols = slice(g * DK, (g + 1) * DK)
        vcols = slice(g * DV, (g + 1) * DV)
        chains.append((_gla_head(
            q_ref[:, kcols], k_ref[:, kcols], v_ref[:, vcols], r_ref[:, vcols], lr,
            wlr_ref[:, kcols], blr_ref[:, kcols], g_ref[...], st_ref.at[g],
            C=C, DK=DK, DV=DV), store_head(g)))
    while chains:
        for chain in list(chains):
            try:
                next(chain[0])
            except StopIteration as done:
                if chain[1] is not None:
                    chain[1](done.value)
                chains.remove(chain)


def _side_projection(xb_ref, wb_ref, out_ref, parts=2):
    pr = out_ref.shape[0] // parts
    for col in range(0, out_ref.shape[1], MXU_COLS):
        for r in range(parts):
            rows = slice(r * pr, (r + 1) * pr)
            out_ref[rows, col:col + MXU_COLS] = lax.dot_general(
                xb_ref[rows, :], wb_ref[col:col + MXU_COLS, :], (((1,), (1,)), ((), ())),
                preferred_element_type=F32).astype(out_ref.dtype)
            yield


def _gla_head(q, k, v, gate, lr, wlr, blr, norm_g, st_ref, *, C, DK, DV):
    z = jnp.dot(lr, wlr, preferred_element_type=F32) + blr
    yield
    la = _log_sigmoid(z) * (LOG2E / GLA_TAU)

    row = lax.broadcasted_iota(jnp.int32, (C, DK), 0)
    ti = lax.broadcasted_iota(jnp.int32, (C, C), 0)
    si = lax.broadcasted_iota(jnp.int32, (C, C), 1)
    tril = jnp.where(si <= ti, 1.0, 0.0).astype(BF16)
    hi = la.astype(BF16)
    rest = la - hi.astype(F32)
    mid = rest.astype(BF16)
    lo = (rest - mid.astype(F32)).astype(BF16)
    yield
    b = (jnp.dot(tril, hi, preferred_element_type=F32)
         + jnp.dot(tril, mid, preferred_element_type=F32)
         + jnp.dot(tril, lo, preferred_element_type=F32))
    yield

    st = st_ref[...]
    qe = (q * jnp.exp2(b)).astype(BF16)
    inter = lax.dot_general(qe, st.astype(BF16), (((1,), (1,)), ((), ())),
                            preferred_element_type=F32)
    blast = b[C - 1:C, :]
    kd = (k * jnp.exp2(blast - b)).astype(BF16)
    yield

    pair =jnp.where(ti > si, ti ^ si, 0)
    attn = jnp.where(ti == si, jnp.sum(q * k, axis=-1, keepdims=True), 0.0)
    NB = C // SUBLANES
    b3 = b.reshape(NB, SUBLANES, DK)
    sub3 = lax.broadcasted_iota(jnp.int32, (NB, SUBLANES, DK), 1)
    m = C // 2
    while m >= 1:
        blk = 2 * m
        in_b = (row & (blk - 1)) >= m
        if m == 1:
            e = jnp.exp2(la)
        else:
            if blk >= SUBLANES:
                nblk = C // blk
                r = jnp.broadcast_to(b.reshape(nblk, blk, DK)[:, m - 1:m, :],
                                     (nblk, blk, DK)).reshape(C, DK)
            else:
                r3 = None
                for start in range(0, SUBLANES, blk):
                    cand = jnp.broadcast_to(b3[:, start + m - 1:start + m, :], b3.shape)
                    r3 = cand if r3 is None else jnp.where(sub3 >= start, cand, r3)
                r = r3.reshape(C, DK)
            e = jnp.exp2(-jnp.abs(b - r))
        x = jnp.where(in_b, q, k)
        x = (jnp.where(in_b, x * e, x) if m == 1 else x * e).astype(BF16)
        a = lax.dot_general(x, x, (((1,), (1,)), ((), ())),
                            preferred_element_type=F32)
        attn = jnp.where((pair >> (m.bit_length() - 1)) == 1, a, attn)
        m //= 2
        yield

    o = inter + jnp.dot(attn.astype(BF16), v, preferred_element_type=F32)
    yield

    upd = lax.dot_general(v, kd, (((0,), (0,)), ((), ())),
                          preferred_element_type=F32)
    st_ref[...] = st * jnp.exp2(blast) + upd
    yield

    ms = jnp.mean(o * o, axis=-1, keepdims=True)
    on = o * lax.rsqrt(ms + RMS_EPS) * norm_g
    return on * (gate * jax.nn.sigmoid(gate))


def _gla(proj_f, proj_b, proj_s, w_lr, b_lr, norm_g, B, S, cols, xb, w_t, layer, side_rows,
         side_tile=(1024, 512)):
    H = GLA_HEADS
    G = GLA_GROUP
    DK = w_lr.shape[1] // H
    DV = norm_g.shape[1]
    C = min(GLA_CHUNK, S)
    NC = S // C
    WK, WV = G * DK, G * DV
    M, K = xb.shape
    r0, n_side = side_rows
    tm, tn = min(side_tile[0], M), side_tile[1]
    TI = M // tm
    assert S % C == 0 and C % (2 * SUBLANES) == 0 and H == G
    assert M % tm == 0 and n_side % tn == 0 and TI * (n_side // tn) == B * NC
    assert r0 % SUBLANES == 0 and r0 + n_side <= w_t.shape[1]
    assert all(cols[n] % WK == 0 for n in ("gq", "gk")) and all(cols[n] % WV == 0 for n in ("gr", "gv"))
    qo, ko, ro, vo = (cols["gq"] // WK, cols["gk"] // WK, cols["gr"] // WV, cols["gv"] // WV)
    rows = lambda b, h, c: b * NC + c
    blocks = (2 * _nbytes((C, WK), F32) + _nbytes((C, WV), BF16) + _nbytes((C, WV), F32)
              + _nbytes((C, LANES), F32) + _nbytes((LANES, WK), BF16) + _nbytes((C, WV), BF16)
              + _nbytes((tm, K), BF16) + _nbytes((tn, K), F32) + _nbytes((tm, tn), F32))
    return pl.pallas_call(
        functools.partial(_gla_body, C=C, DK=DK, DV=DV, G=G, row_tiles=TI),
        grid=(B, H // G, NC),
        in_specs=[
            pl.BlockSpec((C, WK), lambda b, h, c: (rows(b, h, c), qo + h)),
            pl.BlockSpec((C, WK), lambda b, h, c: (rows(b, h, c), ko + h)),
            pl.BlockSpec((C, WV), lambda b, h, c: (rows(b, h, c), vo + h)),
            pl.BlockSpec((C, WV), lambda b, h, c: (rows(b, h, c), ro + h)),
            pl.BlockSpec((C, LANES), lambda b, h, c: (rows(b, h, c), 0)),
            pl.BlockSpec((LANES, WK), lambda b, h, c: (0, h)),
            pl.BlockSpec((1, WK), lambda b, h, c: (0, h)),
            pl.BlockSpec((1, DV), lambda b, h, c: (0, 0)),
            pl.BlockSpec((tm, K), lambda b, h, c: (rows(b, h, c) % TI, 0)),
            pl.BlockSpec((pl.Element(1), pl.Element(tn), pl.Element(K)),
                         lambda b, h, c: (layer, pl.multiple_of(
                             r0 + tn * (rows(b, h, c) // TI), SUBLANES), 0)),
        ],
        out_specs=[pl.BlockSpec((C, WV), lambda b, h, c: (rows(b, h, c), h)),
                   pl.BlockSpec((tm, tn), lambda b, h, c: (rows(b, h, c) % TI,
                                                           rows(b, h, c) // TI))],
        out_shape=[jax.ShapeDtypeStruct((B * S, H * DV), BF16),
                   jax.ShapeDtypeStruct((M, n_side), F32)],
        scratch_shapes=[pltpu.VMEM((G, DV, DK), F32), pltpu.VMEM((tn, K), BF16)],
        compiler_params=pltpu.CompilerParams(
            dimension_semantics=("arbitrary", "arbitrary", "arbitrary"),
            vmem_limit_bytes=_vmem_limit(blocks, _nbytes((tn, K), BF16) + G * (12 << 20))),
        name="gla_chunked",
    )(proj_f, proj_f, proj_b, proj_f, proj_s, w_lr, b_lr, norm_g, xb, w_t)


def _fox_body(q_ref, k_ref, v_ref, c_ref, o_ref, m_ref, l_ref, acc_ref, cqb_ref, *, T, G):
    qi = pl.program_id(2)
    lane_tiles = T // LANES
    for g in range(G):
        cqb_ref[g] = jnp.broadcast_to(c_ref[g, qi] * LOG2E, (LANES, T)).T
    m_ref[...] = jnp.full_like(m_ref, -jnp.inf)
    l_ref[...] = jnp.zeros_like(l_ref)
    acc_ref[...] = jnp.zeros_like(acc_ref)

    def qk(ki):
        start = pl.multiple_of(ki * T, T)
        return [lax.dot_general(
            q_ref[:, g * LANES:(g + 1) * LANES], k_ref[pl.ds(start, T), g * LANES:(g + 1) * LANES],
            (((1,), (1,)), ((), ())), preferred_element_type=F32) for g in range(G)]

    def softmax_pv(ki, scores, on_diagonal):
        start = pl.multiple_of(ki * T, T)
        for g in range(G):
            head = slice(g * LANES, (g + 1) * LANES)
            t = scores[g] - c_ref[g, ki] * LOG2E
            if on_diagonal:
                ti = lax.broadcasted_iota(jnp.int32, (T, T), 0)
                si = lax.broadcasted_iota(jnp.int32, (T, T), 1)
                t = jnp.where(si <= ti, t, -jnp.inf)
            cq = cqb_ref[g]
            m_prev = m_ref[g]
            m_new = jnp.maximum(m_prev, cq + jnp.max(t, axis=-1, keepdims=True))
            p = jnp.exp2(t + jnp.concatenate([cq - m_new] * lane_tiles, axis=1))
            alpha = jnp.exp2(m_prev - m_new)
            l_ref[g] = alpha * l_ref[g] + jnp.sum(p, axis=-1, keepdims=True)
            acc_ref[g] = alpha * acc_ref[g] + jnp.dot(
                p.astype(BF16), v_ref[pl.ds(start, T), head], preferred_element_type=F32)
            m_ref[g] = m_new

    def pair(kp, carry):
        sa = qk(2 * kp)
        sb = qk(2 * kp + 1)
        softmax_pv(2 * kp, sa, False)
        softmax_pv(2 * kp + 1, sb, False)
        return carry

    lax.fori_loop(0, qi // 2, pair, 0)

    @pl.when(qi % 2 == 1)
    def _():
        softmax_pv(qi - 1, qk(qi - 1), False)

    softmax_pv(qi, qk(qi), True)
    for g in range(G):
        o_ref[:, g * LANES:(g + 1) * LANES] = (acc_ref[g] / l_ref[g]).astype(o_ref.dtype)


def _fox(proj_b, c, B, S, cols, dh):
    H = FOX_HEADS
    G = FOX_GROUP
    T = min(FOX_TILE, S)
    NT = S // T
    W = G * dh
    assert S % T == 0 and H % G == 0 and dh == LANES
    qo, ko, vo = cols["fq"] // W, cols["fk"] // W, cols["fv"] // W
    assert all(cols[n] % W == 0 for n in ("fq", "fk", "fv"))
    c_row = c.reshape(B, H, NT, 1, T)
    blocks = (2 * _nbytes((T, W), BF16) + 2 * _nbytes((S, W), BF16)
              + G * _nbytes((NT, SUBLANES, T), F32))
    return pl.pallas_call(
        functools.partial(_fox_body, T=T, G=G),
        grid=(B, H // G, NT),
        in_specs=[
            pl.BlockSpec((T, W), lambda b, h, qi: (b * NT + qi, qo + h)),
            pl.BlockSpec((S, W), lambda b, h, qi: (b, ko + h)),
            pl.BlockSpec((S, W), lambda b, h, qi: (b, vo + h)),
            pl.BlockSpec((None, G, NT, 1, T), lambda b, h, qi: (b, h, 0, 0, 0)),
        ],
        out_specs=pl.BlockSpec((T, W), lambda b, h, qi: (b * NT + qi, h)),
        out_shape=jax.ShapeDtypeStruct((B * S, H * dh), BF16),
        scratch_shapes=[pltpu.VMEM((G, T, LANES), F32), pltpu.VMEM((G, T, LANES), F32),
                        pltpu.VMEM((G, T, dh), F32), pltpu.VMEM((G, T, LANES), F32)],
        compiler_params=pltpu.CompilerParams(
            dimension_semantics=("parallel", "parallel", "arbitrary"),
            vmem_limit_bytes=_vmem_limit(blocks, (4 * G + 8) * _nbytes((T, T), F32))),
        name="fox_attention",
    )(proj_b, proj_b, proj_b, c_row)


def _merge_body(og_ref, of_ref, wg_ref, wf_ref, ma_ref, mb_ref, o_ref):
    og = og_ref[...]
    of = of_ref[...]
    for c in range(0, o_ref.shape[1], MXU_COLS):
        cols = slice(c, c + MXU_COLS)
        yg = jnp.dot(og, wg_ref[:, cols], preferred_element_type=F32)
        yf = jnp.dot(of, wf_ref[:, cols], preferred_element_type=F32)
        o_ref[:, cols] = (jax.nn.sigmoid(ma_ref[:, cols]) * yg
                          + jax.nn.sigmoid(mb_ref[:, cols]) * yf).astype(o_ref.dtype)


def _merge(og, of, w_bg, w_bf, proj_f, cols, tm=1024, tn=1024):
    M, KG = og.shape
    KF = of.shape[1]
    N = w_bg.shape[1]
    tm, tn = min(tm, M), min(tn, N)
    ao, bo = cols["ma"] // tn, cols["mb"] // tn
    assert cols["ma"] % tn == 0 and cols["mb"] % tn == 0
    blocks = (_nbytes((tm, KG + KF), BF16) + _nbytes((KG + KF, tn), BF16)
              + 2 * _nbytes((tm, tn), F32) + _nbytes((tm, tn), BF16))
    return pl.pallas_call(
        _merge_body,
        grid=(N // tn, M // tm),
        in_specs=[
            pl.BlockSpec((tm, KG), lambda j, i: (i, 0)),
            pl.BlockSpec((tm, KF), lambda j, i: (i, 0)),
            pl.BlockSpec((KG, tn), lambda j, i: (0, j)),
            pl.BlockSpec((KF, tn), lambda j, i: (0, j)),
            pl.BlockSpec((tm, tn), lambda j, i: (i, ao + j)),
            pl.BlockSpec((tm, tn), lambda j, i: (i, bo + j)),
        ],
        out_specs=pl.BlockSpec((tm, tn), lambda j, i: (i, j)),
        out_shape=jax.ShapeDtypeStruct((M, N), BF16),
        compiler_params=pltpu.CompilerParams(
            dimension_semantics=("parallel", "parallel"),
            vmem_limit_bytes=_vmem_limit(blocks, 3 * _nbytes((tm, tn), F32))),
        name="branch_merge",
    )(og, of, w_bg, w_bf, proj_f, proj_f)


def _proj_ln_body(a_ref, w_ref, x_ref, g_ref, b_ref, o_ref, ob_ref, *, alpha, parts):
    pr = o_ref.shape[0] // parts
    ys = [jnp.dot(a_ref[r * pr:(r + 1) * pr, :], w_ref[...], preferred_element_type=F32)
          for r in range(parts)]
    for r in range(parts):
        rows = slice(r * pr, (r + 1) * pr)
        out = _layer_norm(alpha * x_ref[rows, :] + ys[r], g_ref[...], b_ref[...])
        o_ref[rows, :] = out
        ob_ref[rows, :] = out.astype(BF16)


def _proj_ln(a, w, resid, g, b, alpha, tm=512):
    M, K = a.shape
    N = w.shape[1]
    tm = min(tm, M)
    blocks = (_nbytes((tm, K), BF16) + _nbytes((K, N), BF16) + 2 * _nbytes((tm, N), F32)
              + _nbytes((tm, N), BF16))
    return pl.pallas_call(
        functools.partial(_proj_ln_body, alpha=alpha, parts=2),
        grid=(M // tm,),
        in_specs=[
            pl.BlockSpec((tm, K), lambda i: (i, 0)),
            pl.BlockSpec((K, N), lambda i: (0, 0)),
            pl.BlockSpec((tm, N), lambda i: (i, 0)),
            pl.BlockSpec((1, N), lambda i: (0, 0)),
            pl.BlockSpec((1, N), lambda i: (0, 0)),
        ],
        out_specs=[pl.BlockSpec((tm, N), lambda i: (i, 0)),
                   pl.BlockSpec((tm, N), lambda i: (i, 0))],
        out_shape=[jax.ShapeDtypeStruct((M, N), F32), jax.ShapeDtypeStruct((M, N), BF16)],
        compiler_params=pltpu.CompilerParams(
            dimension_semantics=("parallel",),
            vmem_limit_bytes=_vmem_limit(blocks, 3 * _nbytes((tm, N), F32))),
        name="out_proj_layernorm",
    )(a, w, resid, g, b)


def _ffn_up_body(x_ref, wg_ref, wu_ref, cw_ref, cb_ref, h_ref, wgb_ref, wub_ref, gbuf_ref,
                 *, tm, tiles_per_seq, pr, pc):
    i = pl.program_id(1)
    HALO = SUBLANES
    tn = h_ref.shape[1]

    @pl.when(i == 0)
    def _():
        wgb_ref[...] = wg_ref[...].astype(BF16)
        wub_ref[...] = wu_ref[...].astype(BF16)

    @pl.when(i % tiles_per_seq == 0)
    def _():
        gbuf_ref[0:HALO, :] = jnp.zeros((HALO, tn), F32)

    cw = cw_ref[...]
    cb = cb_ref[...]
    parts = [(r, c) for c in range(tn // pc) for r in range(tm // pr)]

    def matmuls(r, c):
        x = x_ref[r * pr:(r + 1) * pr, :]
        cols = slice(c * pc, (c + 1) * pc)
        g = jnp.dot(x, wgb_ref[:, cols], preferred_element_type=F32)
        u = jnp.dot(x, wub_ref[:, cols], preferred_element_type=F32)
        gbuf_ref[HALO + r * pr:HALO + (r + 1) * pr, cols] = g
        return g, u

    def finish(r, c, g, u):
        cols = slice(c * pc, (c + 1) * pc)
        y = cb[:, cols] + cw[CONV_W - 1:CONV_W, cols] * g
        for j in range(CONV_W - 1):
            off = HALO - (CONV_W - 1) + j + r * pr
            y = y + cw[j:j + 1, cols] * gbuf_ref[off:off + pr, cols]
        h_ref[r * pr:(r + 1) * pr, cols] = (jax.nn.gelu(y, approximate=True) * u).astype(h_ref.dtype)

    pending = None
    for r, c in parts:
        gu = matmuls(r, c)
        if pending is not None:
            finish(*pending)
        pending = (r, c) + gu
    finish(*pending)
    gbuf_ref[0:HALO, :] = gbuf_ref[tm:tm + HALO, :]


def _ffn_up(xb, w_gate, w_up, conv_w, conv_b, S, tm=1024, tn=512, pr=512, pc=256):
    M, K = xb.shape
    N = w_gate.shape[1]
    tm, tn = min(tm, S), min(tn, N)
    pr, pc = min(pr, tm), min(pc, tn)
    assert S % tm == 0 and N % tn == 0 and tm % pr == 0 and tn % pc == 0
    blocks = (_nbytes((tm, K), BF16) + 2 * _nbytes((K, tn), F32) + _nbytes((tm, tn), BF16)
              + _nbytes((8, tn), F32) * 2)
    scratch = 2 * _nbytes((K, tn), BF16) + _nbytes((tm + SUBLANES, tn), F32)
    return pl.pallas_call(
        functools.partial(_ffn_up_body, tm=tm, tiles_per_seq=S // tm, pr=pr, pc=pc),
        grid=(N // tn, M // tm),
        in_specs=[
            pl.BlockSpec((tm, K), lambda j, i: (i, 0)),
            pl.BlockSpec((K, tn), lambda j, i: (0, j)),
            pl.BlockSpec((K, tn), lambda j, i: (0, j)),
            pl.BlockSpec((CONV_W, tn), lambda j, i: (0, j)),
            pl.BlockSpec((1, tn), lambda j, i: (0, j)),
        ],
        out_specs=pl.BlockSpec((tm, tn), lambda j, i: (i, j)),
        out_shape=jax.ShapeDtypeStruct((M, N), BF16),
        scratch_shapes=[pltpu.VMEM((K, tn), BF16), pltpu.VMEM((K, tn), BF16),
                        pltpu.VMEM((tm + SUBLANES, tn), F32)],
        compiler_params=pltpu.CompilerParams(
            dimension_semantics=("parallel", "arbitrary"),
            vmem_limit_bytes=_vmem_limit(blocks, scratch + 8 * _nbytes((pr, pc), F32))),
        name="ffn_gate_up",
    )(xb, w_gate, w_up, conv_w, conv_b)


def _ffn_down_body(h_ref, w_ref, x_ref, g_ref, b_ref, o_ref, *, alpha):
    kk = pl.program_id(1)

    @pl.when(kk == 0)
    def _():
        o_ref[...] = alpha * x_ref[...]

    o_ref[...] += jnp.dot(h_ref[...], w_ref[...], preferred_element_type=F32)

    @pl.when(kk == pl.num_programs(1) - 1)
    def _():
        o_ref[...] = _layer_norm(o_ref[...], g_ref[...], b_ref[...])


def _ffn_down(h, w_down, resid, g, b, alpha, tm=512, tk=2816):
    M, K = h.shape
    N = w_down.shape[1]
    tm, tk = min(tm, M), min(tk, K)
    assert M % tm == 0 and K % tk == 0
    blocks = (_nbytes((tm, tk), BF16) + _nbytes((tk, N), BF16) + 2 * _nbytes((tm, N), F32))
    return pl.pallas_call(
        functools.partial(_ffn_down_body, alpha=alpha),
        grid=(M // tm, K // tk),
        in_specs=[
            pl.BlockSpec((tm, tk), lambda i, kk: (i, kk)),
            pl.BlockSpec((tk, N), lambda i, kk: (kk, 0)),
            pl.BlockSpec((tm, N), lambda i, kk: (i, 0)),
            pl.BlockSpec((1, N), lambda i, kk: (0, 0)),
            pl.BlockSpec((1, N), lambda i, kk: (0, 0)),
        ],
        out_specs=pl.BlockSpec((tm, N), lambda i, kk: (i, 0)),
        out_shape=jax.ShapeDtypeStruct((M, N), F32),
        compiler_params=pltpu.CompilerParams(
            dimension_semantics=("parallel", "arbitrary"),
            vmem_limit_bytes=_vmem_limit(blocks, _nbytes((tm, N), F32))),
        name="ffn_down_layernorm",
    )(h, w_down, resid, g, b)


def _ple_body(x_ref, wg_ref, p_ref, wp_ref, o_ref, xb_ref, pb_ref):
    j = pl.program_id(1)
    tn = o_ref.shape[1]

    @pl.when(j == 0)
    def _():
        xb_ref[...] = x_ref[...].astype(BF16)
        pb_ref[...] = p_ref[...].astype(BF16)

    xb = xb_ref[...]
    pb = pb_ref[...]
    for c in range(0, tn, MXU_COLS):
        cols = slice(c, c + MXU_COLS)
        gate = jax.nn.sigmoid(jnp.dot(xb, wg_ref[:, cols], preferred_element_type=F32))
        emb = jnp.dot(pb, wp_ref[:, cols], preferred_element_type=F32)
        resid = x_ref[:, pl.ds(pl.multiple_of(j * tn + c, MXU_COLS), MXU_COLS)]
        o_ref[:, cols] = resid + gate * emb


def _ple(x, p, w_gate, w_proj, tm=1024, tn=1024):
    M, K = x.shape
    N = w_gate.shape[1]
    P = p.shape[1]
    tm, tn = min(tm, M), min(tn, N)
    assert K == N and M % tm == 0 and N % tn == 0
    blocks = (_nbytes((tm, K), F32) + _nbytes((K, tn), BF16) + _nbytes((tm, P), F32)
              + _nbytes((P, tn), BF16) + _nbytes((tm, tn), F32))
    scratch = _nbytes((tm, K), BF16) + _nbytes((tm, P), BF16)
    return pl.pallas_call(
        _ple_body,
        grid=(M // tm, N // tn),
        in_specs=[
            pl.BlockSpec((tm, K), lambda i, j: (i, 0)),
            pl.BlockSpec((K, tn), lambda i, j: (0, j)),
            pl.BlockSpec((tm, P), lambda i, j: (i, 0)),
            pl.BlockSpec((P, tn), lambda i, j: (0, j)),
        ],
        out_specs=pl.BlockSpec((tm, tn), lambda i, j: (i, j)),
        out_shape=jax.ShapeDtypeStruct((M, N), F32),
        scratch_shapes=[pltpu.VMEM((tm, K), BF16), pltpu.VMEM((tm, P), BF16)],
        compiler_params=pltpu.CompilerParams(
            dimension_semantics=("parallel", "arbitrary"),
            vmem_limit_bytes=_vmem_limit(blocks, scratch + 3 * _nbytes((tm, tn), F32))),
        name="ple_gate",
    )(x, w_gate, p, w_proj)


def _cast_small_body(x_ref, w_ref, xb_ref, o_ref):
    xb = x_ref[...].astype(BF16)
    xb_ref[...] = xb
    o_ref[...] = lax.dot_general(xb, w_ref[...].astype(BF16), (((1,), (1,)), ((), ())),
                                 preferred_element_type=F32)


def _cast_and_small_proj(x, w_small, tm=1024):
    M, K = x.shape
    ns = w_small.shape[0]
    tm = min(tm, M)
    assert M % tm == 0
    blocks = _nbytes((tm, K), F32) + _nbytes((ns, K), F32) + _nbytes((tm, K), BF16) \
        + _nbytes((tm, ns), F32)
    return pl.pallas_call(
        _cast_small_body,
        grid=(M // tm,),
        in_specs=[pl.BlockSpec((tm, K), lambda i: (i, 0)),
                  pl.BlockSpec((ns, K), lambda i: (0, 0))],
        out_specs=[pl.BlockSpec((tm, K), lambda i: (i, 0)),
                   pl.BlockSpec((tm, ns), lambda i: (i, 0))],
        out_shape=[jax.ShapeDtypeStruct((M, K), BF16), jax.ShapeDtypeStruct((M, ns), F32)],
        compiler_params=pltpu.CompilerParams(
            dimension_semantics=("parallel",),
            vmem_limit_bytes=_vmem_limit(blocks, _nbytes((tm, K), BF16))),
        name="in_proj_small",
    )(x, w_small)


def _in_proj(x, w_in, layer, rank, later_weights):
    M, D = x.shape
    gla_qk = D // 2
    fox_w = FOX_HEADS * LANES
    names = ("gq", "gk", "gv", "gr", "glr", "fq", "fk", "fv", "ff", "ma", "mb")
    widths = (gla_qk, gla_qk, D, D, rank, fox_w, fox_w, fox_w, FOX_HEADS, D, D)
    assert sum(widths) == w_in.shape[2]
    width, offset, off = {}, {}, 0
    for n, wd in zip(names, widths):
        width[n], offset[n] = wd, off
        off += wd
    w_t = jnp.swapaxes(w_in, 1, 2)

    out_scale = {"gq": (gla_qk // GLA_HEADS) ** -0.5, "fq": LANES ** -0.5 * LOG2E}
    T = IN_PROJ_TILE

    def project(order, out_dtype, name, side_casts=()):
        src_rows, scales, cols, o = [], [], {}, 0
        for n in order:
            assert width[n] % T == 0
            cols[n] = o
            o += width[n]
            src_rows += [offset[n] + t for t in range(0, width[n], T)]
            scales.append(jnp.full((1, width[n]), out_scale.get(n, 1.0), F32))
        out, casts = _matmul_nt(xb, w_t, layer, src_rows, jnp.concatenate(scales, axis=1),
                                out_dtype, 1024, T, name, side_casts)
        return (out, cols), casts

    small, cols_s = [], {}
    for t, n in enumerate(("glr", "ff")):
        wseg = w_t[layer, offset[n]:offset[n] + width[n], :]
        small.append(jnp.pad(wseg, ((0, LANES - width[n]), (0, 0))))
        cols_s[n] = t * LANES
    w_small = jnp.concatenate(small, axis=0)
    xb, proj_s = _cast_and_small_proj(x, w_small)

    proj_b, _ = project(("gv", "fq", "fk", "fv"), BF16, "in_proj_bf16")
    proj_f, weights_bf16 = project(("gq", "gk", "gr"), F32, "in_proj_f32", later_weights)
    assert offset["mb"] == offset["ma"] + width["ma"]
    gates = (xb, w_t, (offset["ma"], width["ma"] + width["mb"]), {"ma": 0, "mb": width["ma"]})
    return proj_b, proj_f, (proj_s, cols_s), gates, weights_bf16


def kernel(x, p, w_in, w_gla_lr, b_gla_lr, gla_norm_g, b_forget, w_branch_gla, w_branch_fox,
           w_out, ln1_g, ln1_b, w_gate, w_up, conv_w, conv_b, w_down, ln2_g, ln2_b,
           w_ple_gate, w_ple_proj):
    B, S, D = x.shape
    depth = w_in.shape[0]
    alpha = (2 * depth) ** 0.25
    M = B * S
    xf = x.reshape(M, D)
    for i in range(depth):
        rank = w_gla_lr.shape[1]
        later = (w_branch_gla[i], w_branch_fox[i], w_out[i], w_down[i], w_ple_gate[i])
        (proj_b, cols_b), (proj_f, cols_f), (proj_s, cols_s), gates, later_bf16 = _in_proj(
            xf, w_in, i, rank, later)
        xb, w_t, gate_rows, cols_m = gates
        w_bg, w_bf, w_o, w_d, w_pg = later_bf16

        ff = proj_s[:, cols_s["ff"]:cols_s["ff"] + FOX_HEADS]
        c = _forget_cumsum(ff, b_forget[i], B, S)
        of = _fox(proj_b, c, B, S, cols_b, LANES)

        w_lr = jnp.pad(w_gla_lr[i], ((0, LANES - rank), (0, 0))).astype(BF16)
        og, proj_m = _gla(proj_f, proj_b, proj_s, w_lr, b_gla_lr[i].reshape(1, -1),
                          gla_norm_g[i].reshape(1, -1), B, S, {**cols_f, **cols_b},
                          xb, w_t, i, gate_rows)

        merged = _merge(og, of, w_bg, w_bf, proj_m, cols_m)
        x1, x1b = _proj_ln(merged, w_o, xf, ln1_g[i].reshape(1, -1),
                           ln1_b[i].reshape(1, -1), alpha)
        h = _ffn_up(x1b, w_gate[i], w_up[i], conv_w[i], conv_b[i].reshape(1, -1), S)
        x2 = _ffn_down(h, w_d, x1, ln2_g[i].reshape(1, -1), ln2_b[i].reshape(1, -1), alpha)
        xf = _ple(x2, p[i].reshape(M, -1), w_pg, w_ple_proj[i].astype(BF16))
    return xf.reshape(B, S, D)
```

```python
import functools

import jax
import jax.numpy as jnp
from jax import lax
from jax.experimental import pallas as pl
from jax.experimental.pallas import tpu as pltpu

F32 = jnp.float32
BF16 = jnp.bfloat16

GLA_HEADS = 4
GLA_TAU = 16.0
FOX_HEADS = 8
CONV_W = 3
LN_EPS = 1e-5
RMS_EPS = 1e-6
LOG2E = 1.4426950408889634

LANES = 128
SUBLANES = 8
MXU_COLS = 256
VMEM_BYTES_V7X = 64 * 1024 * 1024

IN_PROJ_TILE = 1024
GLA_CHUNK = 128
GLA_GROUP = 4
FOX_TILE = 512
FOX_GROUP = 4


def _vmem_limit(block_bytes, extra_bytes=0):
    est = 2 * block_bytes + extra_bytes + (4 << 20)
    return int(min(max(est, 16 << 20), VMEM_BYTES_V7X - (2 << 20)))


def _nbytes(shape, dtype):
    n = 1
    for s in shape:
        n *= s
    return n * jnp.dtype(dtype).itemsize


def _log_sigmoid(z):
    return jnp.minimum(z, 0.0) - jnp.log(1.0 + jnp.exp(-jnp.abs(z)))


def _layer_norm(y, g, b):
    mu = jnp.mean(y, axis=-1, keepdims=True)
    d = y - mu
    var = jnp.mean(d * d, axis=-1, keepdims=True)
    return d * lax.rsqrt(var + LN_EPS) * g + b


def _mm_nt_body(rows_ref, a_ref, wt_ref, cs_ref, *refs, n_side):
    del rows_ref
    side_in, o_ref, side_out, wb_ref = (refs[:n_side], refs[n_side],
                                        refs[n_side + 1:2 * n_side + 1], refs[-1])

    @pl.when(pl.program_id(1) == 0)
    def _():
        wb_ref[...] = wt_ref[0].astype(BF16)

    a = a_ref[...]
    for c in range(0, o_ref.shape[1], MXU_COLS):
        cols = slice(c, c + MXU_COLS)
        acc = lax.dot_general(a, wb_ref[cols, :], (((1,), (1,)), ((), ())),
                              preferred_element_type=F32)
        o_ref[:, cols] = (acc * cs_ref[:, cols]).astype(o_ref.dtype)
    for src, dst in zip(side_in, side_out):
        dst[...] = src[...].astype(dst.dtype)


def _matmul_nt(a, w_t, layer, src_rows, col_scale, out_dtype, tm, tn, name, side_casts=()):
    M, K = a.shape
    nt = len(src_rows)
    tm = min(tm, M)
    assert M % tm == 0 and col_scale.shape == (1, nt * tn)
    assert all(r % SUBLANES == 0 and r + tn <= w_t.shape[1] for r in src_rows)
    TI = M // tm
    steps = nt * TI
    BF16_ROWS = 2 * SUBLANES
    assert all(s.shape[0] % (steps * BF16_ROWS) == 0 for s in side_casts)
    slab = lambda s: (s.shape[0] // steps, s.shape[1])
    side_specs = [pl.BlockSpec(slab(s), lambda j, i, rows: (j * TI + i, 0)) for s in side_casts]
    grid_spec = pltpu.PrefetchScalarGridSpec(
        num_scalar_prefetch=1,
        grid=(nt, TI),
        in_specs=[pl.BlockSpec((tm, K), lambda j, i, rows: (i, 0)),
                  pl.BlockSpec((pl.Element(1), pl.Element(tn), pl.Element(K)),
                               lambda j, i, rows: (layer, pl.multiple_of(rows[j], SUBLANES), 0)),
                  pl.BlockSpec((1, tn), lambda j, i, rows: (0, j))] + side_specs,
        out_specs=[pl.BlockSpec((tm, tn), lambda j, i, rows: (i, j))] + side_specs,
        scratch_shapes=[pltpu.VMEM((tn, K), BF16)],
    )
    blocks = (_nbytes((tm, K), a.dtype) + _nbytes((tn, K), w_t.dtype)
              + _nbytes((tm, tn), out_dtype)
              + sum(_nbytes(slab(s), F32) + _nbytes(slab(s), BF16) for s in side_casts))
    outs = pl.pallas_call(
        functools.partial(_mm_nt_body, n_side=len(side_casts)),
        grid_spec=grid_spec,
        out_shape=[jax.ShapeDtypeStruct((M, nt * tn), out_dtype)]
        + [jax.ShapeDtypeStruct(s.shape, BF16) for s in side_casts],
        compiler_params=pltpu.CompilerParams(
            dimension_semantics=("arbitrary", "arbitrary"),
            vmem_limit_bytes=_vmem_limit(blocks, _nbytes((tn, K), BF16) + _nbytes((tm, tn), F32))),
        name=name,
    )(jnp.asarray(src_rows, jnp.int32), a, w_t, col_scale, *side_casts)
    return outs[0], list(outs[1:])


def _fcum_body(ff_ref, bias_ref, c_ref, *, groups):
    z = ff_ref[...] + bias_ref[...]
    x = _log_sigmoid(z)
    rows, lanes = x.shape
    lane = lax.broadcasted_iota(jnp.int32, x.shape, 1)
    s = 1
    while s < lanes:
        x = x + jnp.where(lane >= s, pltpu.roll(x, s, axis=1), 0.0)
        s *= 2
    tot = jnp.broadcast_to(x[:, lanes - 1:lanes], x.shape)
    grp = lax.broadcasted_iota(jnp.int32, x.shape, 0) % groups
    inc = tot
    s = 1
    while s < groups:
        inc = inc + jnp.where(grp >= s, pltpu.roll(inc, s, axis=0), 0.0)
        s *= 2
    c_ref[...] = x + (inc - tot)


def _forget_cumsum(ff, b_forget, B, S):
    H = ff.shape[1]
    groups = S // LANES
    ff_t = ff.reshape(B, S, H).transpose(0, 2, 1).reshape(B * H * groups, LANES)
    bias = jnp.broadcast_to(b_forget.reshape(1, H, 1, 1),
                            (B, H, groups, LANES)).reshape(B * H * groups, LANES)
    c = pl.pallas_call(
        functools.partial(_fcum_body, groups=groups),
        out_shape=jax.ShapeDtypeStruct(ff_t.shape, F32),
        name="fox_forget_cumsum",
    )(ff_t, bias)
    return c.reshape(B, H, S)


def _gla_body(q_ref, k_ref, v_ref, r_ref, lr_ref, wlr_ref, blr_ref, g_ref, xb_ref, wt_ref,
              o_ref, pm_ref, st_ref, wb_ref, *, C, DK, DV, G, row_tiles):
    c = pl.program_id(2)
    step = pl.program_id(0) * pl.num_programs(2) + c

    @pl.when(c == 0)
    def _():
        st_ref[...] = jnp.zeros_like(st_ref)

    @pl.when(step % row_tiles == 0)
    def _():
        wb_ref[...] = wt_ref[0].astype(BF16)

    def store_head(g):
        def store(value):
            o_ref[:, g * DV:(g + 1) * DV] = value.astype(o_ref.dtype)
        return store

    lr = lr_ref[...].astype(BF16)
    chains = [(_side_projection(xb_ref, wb_ref, pm_ref), None)]
    for g in range(G):
        kcols = slice(g * DK, (g + 1) * DK)
        vcols = slice(g * DV, (g + 1) * DV)
        chains.append((_gla_head(
            q_ref[:, kcols], k_ref[:, kcols], v_ref[:, vcols], r_ref[:, vcols], lr,
            wlr_ref[:, kcols], blr_ref[:, kcols], g_ref[...], st_ref.at[g],
            C=C, DK=DK, DV=DV), store_head(g)))
    while chains:
        for chain in list(chains):
            try:
                next(chain[0])
            except StopIteration as done:
                if chain[1] is not None:
                    chain[1](done.value)
                chains.remove(chain)


def _side_projection(xb_ref, wb_ref, out_ref, parts=2, scale_ref=None):
    pr = out_ref.shape[0] // parts
    for col in range(0, out_ref.shape[1], MXU_COLS):
        for r in range(parts):
            rows = slice(r * pr, (r + 1) * pr)
            acc = lax.dot_general(
                xb_ref[rows, :], wb_ref[col:col + MXU_COLS, :], (((1,), (1,)), ((), ())),
                preferred_element_type=F32)
            if scale_ref is not None:
                acc = acc * scale_ref[:, col:col + MXU_COLS]
            out_ref[rows, col:col + MXU_COLS] = acc.astype(out_ref.dtype)
            yield


def _gla_head(q, k, v, gate, lr, wlr, blr, norm_g, st_ref, *, C, DK, DV):
    z = jnp.dot(lr, wlr, preferred_element_type=F32) + blr
    yield
    la = _log_sigmoid(z) * (LOG2E / GLA_TAU)

    row = lax.broadcasted_iota(jnp.int32, (C, DK), 0)
    ti = lax.broadcasted_iota(jnp.int32, (C, C), 0)
    si = lax.broadcasted_iota(jnp.int32, (C, C), 1)
    tril = jnp.where(si <= ti, 1.0, 0.0).astype(BF16)
    hi = la.astype(BF16)
    rest = la - hi.astype(F32)
    mid = rest.astype(BF16)
    lo = (rest - mid.astype(F32)).astype(BF16)
    yield
    b = (jnp.dot(tril, hi, preferred_element_type=F32)
         + jnp.dot(tril, mid, preferred_element_type=F32)
         + jnp.dot(tril, lo, preferred_element_type=F32))
    yield

    st = st_ref[...]
    qe = (q * jnp.exp2(b)).astype(BF16)
    inter = lax.dot_general(qe, st.astype(BF16), (((1,), (1,)), ((), ())),
                            preferred_element_type=F32)
    blast = b[C - 1:C, :]
    kd = (k * jnp.exp2(blast - b)).astype(BF16)
    yield

    pair =jnp.where(ti > si, ti ^ si, 0)
    attn = jnp.where(ti == si, jnp.sum(q * k, axis=-1, keepdims=True), 0.0)
    NB = C // SUBLANES
    b3 = b.reshape(NB, SUBLANES, DK)
    sub3 = lax.broadcasted_iota(jnp.int32, (NB, SUBLANES, DK), 1)
    m = C // 2
    while m >= 1:
        blk = 2 * m
        in_b = (row & (blk - 1)) >= m
        if m == 1:
            e = jnp.exp2(la)
        else:
            if blk >= SUBLANES:
                nblk = C // blk
                r = jnp.broadcast_to(b.reshape(nblk, blk, DK)[:, m - 1:m, :],
                                     (nblk, blk, DK)).reshape(C, DK)
            else:
                r3 = None
                for start in range(0, SUBLANES, blk):
                    cand = jnp.broadcast_to(b3[:, start + m - 1:start + m, :], b3.shape)
                    r3 = cand if r3 is None else jnp.where(sub3 >= start, cand, r3)
                r = r3.reshape(C, DK)
            e = jnp.exp2(-jnp.abs(b - r))
        x = jnp.where(in_b, q, k)
        x = (jnp.where(in_b, x * e, x) if m == 1 else x * e).astype(BF16)
        a = lax.dot_general(x, x, (((1,), (1,)), ((), ())),
                            preferred_element_type=F32)
        attn = jnp.where((pair >> (m.bit_length() - 1)) == 1, a, attn)
        m //= 2
        yield

    o = inter + jnp.dot(attn.astype(BF16), v, preferred_element_type=F32)
    yield

    upd = lax.dot_general(v, kd, (((0,), (0,)), ((), ())),
                          preferred_element_type=F32)
    st_ref[...] = st * jnp.exp2(blast) + upd
    yield

    ms = jnp.mean(o * o, axis=-1, keepdims=True)
    on = o * lax.rsqrt(ms + RMS_EPS) * norm_g
    return on * (gate * jax.nn.sigmoid(gate))


def _gla(proj_f, proj_b, proj_s, w_lr, b_lr, norm_g, B, S, cols, xb, w_t, layer, side_rows,
         side_tile=(1024, 512)):
    H = GLA_HEADS
    G = GLA_GROUP
    DK = w_lr.shape[1] // H
    DV = norm_g.shape[1]
    C = min(GLA_CHUNK, S)
    NC = S // C
    WK, WV = G * DK, G * DV
    M, K = xb.shape
    r0, n_side = side_rows
    tm, tn = min(side_tile[0], M), side_tile[1]
    TI = M // tm
    assert S % C == 0 and C % (2 * SUBLANES) == 0 and H == G
    assert M % tm == 0 and n_side % tn == 0 and TI * (n_side // tn) == B * NC
    assert r0 % SUBLANES == 0 and r0 + n_side <= w_t.shape[1]
    assert all(cols[n] % WK == 0 for n in ("gq", "gk")) and all(cols[n] % WV == 0 for n in ("gr", "gv"))
    qo, ko, ro, vo = (cols["gq"] // WK, cols["gk"] // WK, cols["gr"] // WV, cols["gv"] // WV)
    rows = lambda b, h, c: b * NC + c
    blocks = (2 * _nbytes((C, WK), F32) + _nbytes((C, WV), BF16) + _nbytes((C, WV), F32)
              + _nbytes((C, LANES), F32) + _nbytes((LANES, WK), BF16) + _nbytes((C, WV), BF16)
              + _nbytes((tm, K), BF16) + _nbytes((tn, K), F32) + _nbytes((tm, tn), F32))
    return pl.pallas_call(
        functools.partial(_gla_body, C=C, DK=DK, DV=DV, G=G, row_tiles=TI),
        grid=(B, H // G, NC),
        in_specs=[
            pl.BlockSpec((C, WK), lambda b, h, c: (rows(b, h, c), qo + h)),
            pl.BlockSpec((C, WK), lambda b, h, c: (rows(b, h, c), ko + h)),
            pl.BlockSpec((C, WV), lambda b, h, c: (rows(b, h, c), vo + h)),
            pl.BlockSpec((C, WV), lambda b, h, c: (rows(b, h, c), ro + h)),
            pl.BlockSpec((C, LANES), lambda b, h, c: (rows(b, h, c), 0)),
            pl.BlockSpec((LANES, WK), lambda b, h, c: (0, h)),
            pl.BlockSpec((1, WK), lambda b, h, c: (0, h)),
            pl.BlockSpec((1, DV), lambda b, h, c: (0, 0)),
            pl.BlockSpec((tm, K), lambda b, h, c: (rows(b, h, c) % TI, 0)),
            pl.BlockSpec((pl.Element(1), pl.Element(tn), pl.Element(K)),
                         lambda b, h, c: (layer, pl.multiple_of(
                             r0 + tn * (rows(b, h, c) // TI), SUBLANES), 0)),
        ],
        out_specs=[pl.BlockSpec((C, WV), lambda b, h, c: (rows(b, h, c), h)),
                   pl.BlockSpec((tm, tn), lambda b, h, c: (rows(b, h, c) % TI,
                                                           rows(b, h, c) // TI))],
        out_shape=[jax.ShapeDtypeStruct((B * S, H * DV), BF16),
                   jax.ShapeDtypeStruct((M, n_side), F32)],
        scratch_shapes=[pltpu.VMEM((G, DV, DK), F32), pltpu.VMEM((tn, K), BF16)],
        compiler_params=pltpu.CompilerParams(
            dimension_semantics=("arbitrary", "arbitrary", "arbitrary"),
            vmem_limit_bytes=_vmem_limit(blocks, _nbytes((tn, K), BF16) + G * (12 << 20))),
        name="gla_chunked",
    )(proj_f, proj_f, proj_b, proj_f, proj_s, w_lr, b_lr, norm_g, xb, w_t)


def _fox_body(rows_ref, q_ref, k_ref, v_ref, c_ref, xb_ref, wt_ref, cs_ref, *refs, T, G, n_side):
    del rows_ref
    side_in, (o_ref, pf_ref) = refs[:n_side], refs[n_side:n_side + 2]
    side_out = refs[n_side + 2:2 * n_side + 2]
    m_ref, l_ref, acc_ref, cqb_ref, wb_ref = refs[2 * n_side + 2:]
    qi = pl.program_id(2)
    lane_tiles = T // LANES

    @pl.when(qi == 0)
    def _():
        wb_ref[...] = wt_ref[0].astype(BF16)
    for g in range(G):
        cqb_ref[g] = jnp.broadcast_to(c_ref[g, qi] * LOG2E, (LANES, T)).T
    m_ref[...] = jnp.full_like(m_ref, -jnp.inf)
    l_ref[...] = jnp.zeros_like(l_ref)
    acc_ref[...] = jnp.zeros_like(acc_ref)

    def qk(ki):
        start = pl.multiple_of(ki * T, T)
        return [lax.dot_general(
            q_ref[:, g * LANES:(g + 1) * LANES], k_ref[pl.ds(start, T), g * LANES:(g + 1) * LANES],
            (((1,), (1,)), ((), ())), preferred_element_type=F32) for g in range(G)]

    def softmax_pv(ki, scores, on_diagonal, side=None):
        start = pl.multiple_of(ki * T, T)
        for g in range(G):
            if side is not None:
                next(side, None)
            head = slice(g * LANES, (g + 1) * LANES)
            t = scores[g] - c_ref[g, ki] * LOG2E
            if on_diagonal:
                ti = lax.broadcasted_iota(jnp.int32, (T, T), 0)
                si = lax.broadcasted_iota(jnp.int32, (T, T), 1)
                t = jnp.where(si <= ti, t, -jnp.inf)
            cq = cqb_ref[g]
            m_prev = m_ref[g]
            m_new = jnp.maximum(m_prev, cq + jnp.max(t, axis=-1, keepdims=True))
            p = jnp.exp2(t + jnp.concatenate([cq - m_new] * lane_tiles, axis=1))
            alpha = jnp.exp2(m_prev - m_new)
            l_ref[g] = alpha * l_ref[g] + jnp.sum(p, axis=-1, keepdims=True)
            acc_ref[g] = alpha * acc_ref[g] + jnp.dot(
                p.astype(BF16), v_ref[pl.ds(start, T), head], preferred_element_type=F32)
            m_ref[g] = m_new

    def sweep(ki, carry):
        softmax_pv(ki, qk(ki), False)
        return carry

    lax.fori_loop(0, qi, sweep, 0)

    side = _side_projection(xb_ref, wb_ref, pf_ref, parts=1, scale_ref=cs_ref)
    softmax_pv(qi, qk(qi), True, side)
    for _ in side:
        pass
    for g in range(G):
        o_ref[:, g * LANES:(g + 1) * LANES] = (acc_ref[g] / l_ref[g]).astype(o_ref.dtype)
    for src, dst in zip(side_in, side_out):
        dst[...] = src[...].astype(dst.dtype)


def _fox(proj_b, c, B, S, cols, dh, xb, w_t, layer, src_rows, col_scale, side_casts,
         side_tile=(1024, 1024)):
    H = FOX_HEADS
    G = FOX_GROUP
    T = min(FOX_TILE, S)
    NT = S // T
    W = G * dh
    HG = H // G
    M, K = xb.shape
    tm, tn = min(side_tile[0], M), side_tile[1]
    steps = B * HG * NT
    BF16_ROWS = 2 * SUBLANES
    assert S % T == 0 and H % G == 0 and dh == LANES
    assert M // tm == NT and len(src_rows) == B * HG and col_scale.shape == (1, len(src_rows) * tn)
    assert all(r % SUBLANES == 0 and r + tn <= w_t.shape[1] for r in src_rows)
    assert all(s.shape[0] % (steps * BF16_ROWS) == 0 for s in side_casts)
    qo, ko, vo = cols["fq"] // W, cols["fk"] // W, cols["fv"] // W
    assert all(cols[n] % W == 0 for n in ("fq", "fk", "fv"))
    c_row = c.reshape(B, H, NT, 1, T)
    slab = lambda s: (s.shape[0] // steps, s.shape[1])
    side_specs = [pl.BlockSpec(slab(s), lambda b, h, qi, rows: ((b * HG + h) * NT + qi, 0))
                  for s in side_casts]
    once = pl.Buffered(1)
    grid_spec = pltpu.PrefetchScalarGridSpec(
        num_scalar_prefetch=1,
        grid=(B, HG, NT),
        in_specs=[
            pl.BlockSpec((T, W), lambda b, h, qi, rows: (b * NT + qi, qo + h)),
            pl.BlockSpec((S, W), lambda b, h, qi, rows: (b, ko + h), pipeline_mode=once),
            pl.BlockSpec((S, W), lambda b, h, qi, rows: (b, vo + h), pipeline_mode=once),
            pl.BlockSpec((None, G, NT, 1, T), lambda b, h, qi, rows: (b, h, 0, 0, 0)),
            pl.BlockSpec((tm, K), lambda b, h, qi, rows: (qi, 0)),
            pl.BlockSpec((pl.Element(1), pl.Element(tn), pl.Element(K)),
                         lambda b, h, qi, rows: (
                             layer, pl.multiple_of(rows[b * HG + h], SUBLANES), 0),
                         pipeline_mode=once),
            pl.BlockSpec((1, tn), lambda b, h, qi, rows: (0, b * HG + h)),
        ] + side_specs,
        out_specs=[pl.BlockSpec((T, W), lambda b, h, qi, rows: (b * NT + qi, h)),
                   pl.BlockSpec((tm, tn), lambda b, h, qi, rows: (qi, b * HG + h))] + side_specs,
        scratch_shapes=[pltpu.VMEM((G, T, LANES), F32), pltpu.VMEM((G, T, LANES), F32),
                        pltpu.VMEM((G, T, dh), F32), pltpu.VMEM((G, T, LANES), F32),
                        pltpu.VMEM((tn, K), BF16)],
    )
    resident = 2 * _nbytes((S, W), BF16) + _nbytes((tn, K), F32) + _nbytes((tn, K), BF16)
    blocks = (2 * _nbytes((T, W), BF16) + G * _nbytes((NT, SUBLANES, T), F32)
              + _nbytes((tm, K), BF16) + _nbytes((tm, tn), F32)
              + sum(_nbytes(slab(s), F32) + _nbytes(slab(s), BF16) for s in side_casts))
    outs = pl.pallas_call(
        functools.partial(_fox_body, T=T, G=G, n_side=len(side_casts)),
        grid_spec=grid_spec,
        out_shape=[jax.ShapeDtypeStruct((B * S, H * dh), BF16),
                   jax.ShapeDtypeStruct((M, len(src_rows) * tn), F32)]
        + [jax.ShapeDtypeStruct(s.shape, BF16) for s in side_casts],
        compiler_params=pltpu.CompilerParams(
            dimension_semantics=("arbitrary", "arbitrary", "arbitrary"),
            vmem_limit_bytes=_vmem_limit(
                blocks, resident + (4 * G + 8) * _nbytes((T, T), F32))),
        name="fox_attention",
    )(jnp.asarray(src_rows, jnp.int32), proj_b, proj_b, proj_b, c_row, xb, w_t, col_scale,
      *side_casts)
    return outs[0], outs[1], list(outs[2:])


def _merge_body(og_ref, of_ref, wg_ref, wf_ref, ma_ref, mb_ref, o_ref):
    og = og_ref[...]
    of = of_ref[...]
    for c in range(0, o_ref.shape[1], MXU_COLS):
        cols = slice(c, c + MXU_COLS)
        yg = jnp.dot(og, wg_ref[:, cols], preferred_element_type=F32)
        yf = jnp.dot(of, wf_ref[:, cols], preferred_element_type=F32)
        o_ref[:, cols] = (jax.nn.sigmoid(ma_ref[:, cols]) * yg
                          + jax.nn.sigmoid(mb_ref[:, cols]) * yf).astype(o_ref.dtype)


def _merge(og, of, w_bg, w_bf, proj_f, cols, tm=1024, tn=1024):
    M, KG = og.shape
    KF = of.shape[1]
    N = w_bg.shape[1]
    tm, tn = min(tm, M), min(tn, N)
    ao, bo = cols["ma"] // tn, cols["mb"] // tn
    assert cols["ma"] % tn == 0 and cols["mb"] % tn == 0
    blocks = (_nbytes((tm, KG + KF), BF16) + _nbytes((KG + KF, tn), BF16)
              + 2 * _nbytes((tm, tn), F32) + _nbytes((tm, tn), BF16))
    return pl.pallas_call(
        _merge_body,
        grid=(N // tn, M // tm),
        in_specs=[
            pl.BlockSpec((tm, KG), lambda j, i: (i, 0)),
            pl.BlockSpec((tm, KF), lambda j, i: (i, 0)),
            pl.BlockSpec((KG, tn), lambda j, i: (0, j)),
            pl.BlockSpec((KF, tn), lambda j, i: (0, j)),
            pl.BlockSpec((tm, tn), lambda j, i: (i, ao + j)),
            pl.BlockSpec((tm, tn), lambda j, i: (i, bo + j)),
        ],
        out_specs=pl.BlockSpec((tm, tn), lambda j, i: (i, j)),
        out_shape=jax.ShapeDtypeStruct((M, N), BF16),
        compiler_params=pltpu.CompilerParams(
            dimension_semantics=("parallel", "parallel"),
            vmem_limit_bytes=_vmem_limit(blocks, 3 * _nbytes((tm, tn), F32))),
        name="branch_merge",
    )(og, of, w_bg, w_bf, proj_f, proj_f)


def _proj_ln_body(a_ref, w_ref, x_ref, g_ref, b_ref, o_ref, ob_ref, *, alpha, parts):
    pr = o_ref.shape[0] // parts
    ys = [jnp.dot(a_ref[r * pr:(r + 1) * pr, :], w_ref[...], preferred_element_type=F32)
          for r in range(parts)]
    for r in range(parts):
        rows = slice(r * pr, (r + 1) * pr)
        out = _layer_norm(alpha * x_ref[rows, :] + ys[r], g_ref[...], b_ref[...])
        o_ref[rows, :] = out
        ob_ref[rows, :] = out.astype(BF16)


def _proj_ln(a, w, resid, g, b, alpha, tm=512):
    M, K = a.shape
    N = w.shape[1]
    tm = min(tm, M)
    blocks = (_nbytes((tm, K), BF16) + _nbytes((K, N), BF16) + 2 * _nbytes((tm, N), F32)
              + _nbytes((tm, N), BF16))
    return pl.pallas_call(
        functools.partial(_proj_ln_body, alpha=alpha, parts=2),
        grid=(M // tm,),
        in_specs=[
            pl.BlockSpec((tm, K), lambda i: (i, 0)),
            pl.BlockSpec((K, N), lambda i: (0, 0)),
            pl.BlockSpec((tm, N), lambda i: (i, 0)),
            pl.BlockSpec((1, N), lambda i: (0, 0)),
            pl.BlockSpec((1, N), lambda i: (0, 0)),
        ],
        out_specs=[pl.BlockSpec((tm, N), lambda i: (i, 0)),
                   pl.BlockSpec((tm, N), lambda i: (i, 0))],
        out_shape=[jax.ShapeDtypeStruct((M, N), F32), jax.ShapeDtypeStruct((M, N), BF16)],
        compiler_params=pltpu.CompilerParams(
            dimension_semantics=("parallel",),
            vmem_limit_bytes=_vmem_limit(blocks, 3 * _nbytes((tm, N), F32))),
        name="out_proj_layernorm",
    )(a, w, resid, g, b)


def _ffn_up_body(x_ref, wg_ref, wu_ref, cw_ref, cb_ref, h_ref, wgb_ref, wub_ref, gbuf_ref,
                 *, tm, tiles_per_seq, pr, pc):
    i = pl.program_id(1)
    HALO = SUBLANES
    tn = h_ref.shape[1]

    @pl.when(i == 0)
    def _():
        wgb_ref[...] = wg_ref[...].astype(BF16)
        wub_ref[...] = wu_ref[...].astype(BF16)

    @pl.when(i % tiles_per_seq == 0)
    def _():
        gbuf_ref[0:HALO, :] = jnp.zeros((HALO, tn), F32)

    cw = cw_ref[...]
    cb = cb_ref[...]
    parts = [(r, c) for c in range(tn // pc) for r in range(tm // pr)]

    def matmuls(r, c):
        x = x_ref[r * pr:(r + 1) * pr, :]
        cols = slice(c * pc, (c + 1) * pc)
        g = jnp.dot(x, wgb_ref[:, cols], preferred_element_type=F32)
        u = jnp.dot(x, wub_ref[:, cols], preferred_element_type=F32)
        gbuf_ref[HALO + r * pr:HALO + (r + 1) * pr, cols] = g
        return g, u

    def finish(r, c, g, u):
        cols = slice(c * pc, (c + 1) * pc)
        y = cb[:, cols] + cw[CONV_W - 1:CONV_W, cols] * g
        for j in range(CONV_W - 1):
            off = HALO - (CONV_W - 1) + j + r * pr
            y = y + cw[j:j + 1, cols] * gbuf_ref[off:off + pr, cols]
        h_ref[r * pr:(r + 1) * pr, cols] = (jax.nn.gelu(y, approximate=True) * u).astype(h_ref.dtype)

    pending = None
    for r, c in parts:
        gu = matmuls(r, c)
        if pending is not None:
            finish(*pending)
        pending = (r, c) + gu
    finish(*pending)
    gbuf_ref[0:HALO, :] = gbuf_ref[tm:tm + HALO, :]


def _ffn_up(xb, w_gate, w_up, conv_w, conv_b, S, tm=1024, tn=512, pr=512, pc=256):
    M, K = xb.shape
    N = w_gate.shape[1]
    tm, tn = min(tm, S), min(tn, N)
    pr, pc = min(pr, tm), min(pc, tn)
    assert S % tm == 0 and N % tn == 0 and tm % pr == 0 and tn % pc == 0
    blocks = (_nbytes((tm, K), BF16) + 2 * _nbytes((K, tn), F32) + _nbytes((tm, tn), BF16)
              + _nbytes((8, tn), F32) * 2)
    scratch = 2 * _nbytes((K, tn), BF16) + _nbytes((tm + SUBLANES, tn), F32)
    return pl.pallas_call(
        functools.partial(_ffn_up_body, tm=tm, tiles_per_seq=S // tm, pr=pr, pc=pc),
        grid=(N // tn, M // tm),
        in_specs=[
            pl.BlockSpec((tm, K), lambda j, i: (i, 0)),
            pl.BlockSpec((K, tn), lambda j, i: (0, j)),
            pl.BlockSpec((K, tn), lambda j, i: (0, j)),
            pl.BlockSpec((CONV_W, tn), lambda j, i: (0, j)),
            pl.BlockSpec((1, tn), lambda j, i: (0, j)),
        ],
        out_specs=pl.BlockSpec((tm, tn), lambda j, i: (i, j)),
        out_shape=jax.ShapeDtypeStruct((M, N), BF16),
        scratch_shapes=[pltpu.VMEM((K, tn), BF16), pltpu.VMEM((K, tn), BF16),
                        pltpu.VMEM((tm + SUBLANES, tn), F32)],
        compiler_params=pltpu.CompilerParams(
            dimension_semantics=("parallel", "arbitrary"),
            vmem_limit_bytes=_vmem_limit(blocks, scratch + 8 * _nbytes((pr, pc), F32))),
        name="ffn_gate_up",
    )(xb, w_gate, w_up, conv_w, conv_b)


def _ffn_down_body(h_ref, w_ref, x_ref, g_ref, b_ref, o_ref, *, alpha):
    kk = pl.program_id(1)

    @pl.when(kk == 0)
    def _():
        o_ref[...] = alpha * x_ref[...]

    o_ref[...] += jnp.dot(h_ref[...], w_ref[...], preferred_element_type=F32)

    @pl.when(kk == pl.num_programs(1) - 1)
    def _():
        o_ref[...] = _layer_norm(o_ref[...], g_ref[...], b_ref[...])


def _ffn_down(h, w_down, resid, g, b, alpha, tm=512, tk=2816):
    M, K = h.shape
    N = w_down.shape[1]
    tm, tk = min(tm, M), min(tk, K)
    assert M % tm == 0 and K % tk == 0
    blocks = (_nbytes((tm, tk), BF16) + _nbytes((tk, N), BF16) + 2 * _nbytes((tm, N), F32))
    return pl.pallas_call(
        functools.partial(_ffn_down_body, alpha=alpha),
        grid=(M // tm, K // tk),
        in_specs=[
            pl.BlockSpec((tm, tk), lambda i, kk: (i, kk)),
            pl.BlockSpec((tk, N), lambda i, kk: (kk, 0)),
            pl.BlockSpec((tm, N), lambda i, kk: (i, 0)),
            pl.BlockSpec((1, N), lambda i, kk: (0, 0)),
            pl.BlockSpec((1, N), lambda i, kk: (0, 0)),
        ],
        out_specs=pl.BlockSpec((tm, N), lambda i, kk: (i, 0)),
        out_shape=jax.ShapeDtypeStruct((M, N), F32),
        compiler_params=pltpu.CompilerParams(
            dimension_semantics=("parallel", "arbitrary"),
            vmem_limit_bytes=_vmem_limit(blocks, _nbytes((tm, N), F32))),
        name="ffn_down_layernorm",
    )(h, w_down, resid, g, b)


def _ple_body(x_ref, wg_ref, p_ref, wp_ref, o_ref, xb_ref, pb_ref):
    j = pl.program_id(1)
    tn = o_ref.shape[1]

    @pl.when(j == 0)
    def _():
        xb_ref[...] = x_ref[...].astype(BF16)
        pb_ref[...] = p_ref[...].astype(BF16)

    xb = xb_ref[...]
    pb = pb_ref[...]
    for c in range(0, tn, MXU_COLS):
        cols = slice(c, c + MXU_COLS)
        gate = jax.nn.sigmoid(jnp.dot(xb, wg_ref[:, cols], preferred_element_type=F32))
        emb = jnp.dot(pb, wp_ref[:, cols], preferred_element_type=F32)
        resid = x_ref[:, pl.ds(pl.multiple_of(j * tn + c, MXU_COLS), MXU_COLS)]
        o_ref[:, cols] = resid + gate * emb


def _ple(x, p, w_gate, w_proj, tm=1024, tn=1024):
    M, K = x.shape
    N = w_gate.shape[1]
    P = p.shape[1]
    tm, tn = min(tm, M), min(tn, N)
    assert K == N and M % tm == 0 and N % tn == 0
    blocks = (_nbytes((tm, K), F32) + _nbytes((K, tn), BF16) + _nbytes((tm, P), F32)
              + _nbytes((P, tn), BF16) + _nbytes((tm, tn), F32))
    scratch = _nbytes((tm, K), BF16) + _nbytes((tm, P), BF16)
    return pl.pallas_call(
        _ple_body,
        grid=(M // tm, N // tn),
        in_specs=[
            pl.BlockSpec((tm, K), lambda i, j: (i, 0)),
            pl.BlockSpec((K, tn), lambda i, j: (0, j)),
            pl.BlockSpec((tm, P), lambda i, j: (i, 0)),
            pl.BlockSpec((P, tn), lambda i, j: (0, j)),
        ],
        out_specs=pl.BlockSpec((tm, tn), lambda i, j: (i, j)),
        out_shape=jax.ShapeDtypeStruct((M, N), F32),
        scratch_shapes=[pltpu.VMEM((tm, K), BF16), pltpu.VMEM((tm, P), BF16)],
        compiler_params=pltpu.CompilerParams(
            dimension_semantics=("parallel", "arbitrary"),
            vmem_limit_bytes=_vmem_limit(blocks, scratch + 3 * _nbytes((tm, tn), F32))),
        name="ple_gate",
    )(x, w_gate, p, w_proj)


def _cast_small_body(x_ref, w_ref, xb_ref, o_ref):
    xb = x_ref[...].astype(BF16)
    xb_ref[...] = xb
    o_ref[...] = lax.dot_general(xb, w_ref[...].astype(BF16), (((1,), (1,)), ((), ())),
                                 preferred_element_type=F32)


def _cast_and_small_proj(x, w_small, tm=1024):
    M, K = x.shape
    ns = w_small.shape[0]
    tm = min(tm, M)
    assert M % tm == 0
    blocks = _nbytes((tm, K), F32) + _nbytes((ns, K), F32) + _nbytes((tm, K), BF16) \
        + _nbytes((tm, ns), F32)
    return pl.pallas_call(
        _cast_small_body,
        grid=(M // tm,),
        in_specs=[pl.BlockSpec((tm, K), lambda i: (i, 0)),
                  pl.BlockSpec((ns, K), lambda i: (0, 0))],
        out_specs=[pl.BlockSpec((tm, K), lambda i: (i, 0)),
                   pl.BlockSpec((tm, ns), lambda i: (i, 0))],
        out_shape=[jax.ShapeDtypeStruct((M, K), BF16), jax.ShapeDtypeStruct((M, ns), F32)],
        compiler_params=pltpu.CompilerParams(
            dimension_semantics=("parallel",),
            vmem_limit_bytes=_vmem_limit(blocks, _nbytes((tm, K), BF16))),
        name="in_proj_small",
    )(x, w_small)


def _in_proj(x, w_in, layer, rank):
    M, D = x.shape
    gla_qk = D // 2
    fox_w = FOX_HEADS * LANES
    names = ("gq", "gk", "gv", "gr", "glr", "fq", "fk", "fv", "ff", "ma", "mb")
    widths = (gla_qk, gla_qk, D, D, rank, fox_w, fox_w, fox_w, FOX_HEADS, D, D)
    assert sum(widths) == w_in.shape[2]
    width, offset, off = {}, {}, 0
    for n, wd in zip(names, widths):
        width[n], offset[n] = wd, off
        off += wd
    w_t = jnp.swapaxes(w_in, 1, 2)

    out_scale = {"gq": (gla_qk // GLA_HEADS) ** -0.5, "fq": LANES ** -0.5 * LOG2E}
    T = IN_PROJ_TILE

    def plan(order):
        src_rows, scales, cols, o = [], [], {}, 0
        for n in order:
            assert width[n] % T == 0
            cols[n] = o
            o += width[n]
            src_rows += [offset[n] + t for t in range(0, width[n], T)]
            scales.append(jnp.full((1, width[n]), out_scale.get(n, 1.0), F32))
        return src_rows, jnp.concatenate(scales, axis=1), cols

    small, cols_s = [], {}
    for t, n in enumerate(("glr", "ff")):
        wseg = w_t[layer, offset[n]:offset[n] + width[n], :]
        small.append(jnp.pad(wseg, ((0, LANES - width[n]), (0, 0))))
        cols_s[n] = t * LANES
    w_small = jnp.concatenate(small, axis=0)
    xb, proj_s = _cast_and_small_proj(x, w_small)

    rows_b, scale_b, cols_b = plan(("gv", "fq", "fk", "fv"))
    proj_b, _ = _matmul_nt(xb, w_t, layer, rows_b, scale_b, BF16, 1024, T, "in_proj_bf16")
    assert offset["mb"] == offset["ma"] + width["ma"]
    gates = ((offset["ma"], width["ma"] + width["mb"]), {"ma": 0, "mb": width["ma"]})
    return xb, w_t, (proj_b, cols_b), (proj_s, cols_s), plan(("gq", "gk", "gr")), gates


def kernel(x, p, w_in, w_gla_lr, b_gla_lr, gla_norm_g, b_forget, w_branch_gla, w_branch_fox,
           w_out, ln1_g, ln1_b, w_gate, w_up, conv_w, conv_b, w_down, ln2_g, ln2_b,
           w_ple_gate, w_ple_proj):
    B, S, D = x.shape
    depth = w_in.shape[0]
    alpha = (2 * depth) ** 0.25
    M = B * S
    xf = x.reshape(M, D)
    for i in range(depth):
        rank = w_gla_lr.shape[1]
        xb, w_t, (proj_b, cols_b), (proj_s, cols_s), f32_plan, gates = _in_proj(xf, w_in, i, rank)
        rows_f, scale_f, cols_f = f32_plan
        gate_rows, cols_m = gates

        ff = proj_s[:, cols_s["ff"]:cols_s["ff"] + FOX_HEADS]
        c = _forget_cumsum(ff, b_forget[i], B, S)
        later = (w_branch_gla[i], w_branch_fox[i], w_out[i], w_down[i], w_ple_gate[i])
        of, proj_f, (w_bg, w_bf, w_o, w_d, w_pg) = _fox(
            proj_b, c, B, S, cols_b, LANES, xb, w_t, i, rows_f, scale_f, later)

        w_lr = jnp.pad(w_gla_lr[i], ((0, LANES - rank), (0, 0))).astype(BF16)
        og, proj_m = _gla(proj_f, proj_b, proj_s, w_lr, b_gla_lr[i].reshape(1, -1),
                          gla_norm_g[i].reshape(1, -1), B, S, {**cols_f, **cols_b},
                          xb, w_t, i, gate_rows)

        merged = _merge(og, of, w_bg, w_bf, proj_m, cols_m)
        x1, x1b = _proj_ln(merged, w_o, xf, ln1_g[i].reshape(1, -1),
                           ln1_b[i].reshape(1, -1), alpha)
        h = _ffn_up(x1b, w_gate[i], w_up[i], conv_w[i], conv_b[i].reshape(1, -1), S)
        x2 = _ffn_down(h, w_d, x1, ln2_g[i].reshape(1, -1), ln2_b[i].reshape(1, -1), alpha)
        xf = _ple(x2, p[i].reshape(M, -1), w_pg, w_ple_proj[i].astype(BF16))
    return xf.reshape(B, S, D)
```

```python
import functools

import jax
import jax.numpy as jnp
from jax import lax
from jax.experimental import pallas as pl
from jax.experimental.pallas import tpu as pltpu

F32 = jnp.float32
BF16 = jnp.bfloat16

GLA_HEADS = 4
GLA_TAU = 16.0
FOX_HEADS = 8
CONV_W = 3
LN_EPS = 1e-5
RMS_EPS = 1e-6
LOG2E = 1.4426950408889634

LANES = 128
SUBLANES = 8
MXU_COLS = 256
VMEM_BYTES_V7X = 64 * 1024 * 1024

IN_PROJ_TILE = 1024
GLA_CHUNK = 128
GLA_GROUP = 4
FOX_TILE = 512
FOX_GROUP = 4


def _vmem_limit(block_bytes, extra_bytes=0):
    est = 2 * block_bytes + extra_bytes + (4 << 20)
    return int(min(max(est, 16 << 20), VMEM_BYTES_V7X - (8 << 20)))


def _nbytes(shape, dtype):
    n = 1
    for s in shape:
        n *= s
    return n * jnp.dtype(dtype).itemsize


def _log_sigmoid(z):
    return jnp.minimum(z, 0.0) - jnp.log(1.0 + jnp.exp(-jnp.abs(z)))


def _layer_norm(y, g, b):
    mu = jnp.mean(y, axis=-1, keepdims=True)
    d = y - mu
    var = jnp.mean(d * d, axis=-1, keepdims=True)
    return d * lax.rsqrt(var + LN_EPS) * g + b


def _mm_nt_body(rows_ref, a_ref, wt_ref, cs_ref, *refs, n_side):
    del rows_ref
    side_in, o_ref, side_out, wb_ref = (refs[:n_side], refs[n_side],
                                        refs[n_side + 1:2 * n_side + 1], refs[-1])

    @pl.when(pl.program_id(1) == 0)
    def _():
        wb_ref[...] = wt_ref[0].astype(BF16)

    a = a_ref[...]
    for c in range(0, o_ref.shape[1], MXU_COLS):
        cols = slice(c, c + MXU_COLS)
        acc = lax.dot_general(a, wb_ref[cols, :], (((1,), (1,)), ((), ())),
                              preferred_element_type=F32)
        o_ref[:, cols] = (acc * cs_ref[:, cols]).astype(o_ref.dtype)
    for src, dst in zip(side_in, side_out):
        dst[...] = src[...].astype(dst.dtype)


def _matmul_nt(a, w_t, layer, src_rows, col_scale, out_dtype, tm, tn, name, side_casts=()):
    M, K = a.shape
    nt = len(src_rows)
    tm = min(tm, M)
    assert M % tm == 0 and col_scale.shape == (1, nt * tn)
    assert all(r % SUBLANES == 0 and r + tn <= w_t.shape[1] for r in src_rows)
    TI = M // tm
    steps = nt * TI
    BF16_ROWS = 2 * SUBLANES
    assert all(s.shape[0] % (steps * BF16_ROWS) == 0 for s in side_casts)
    slab = lambda s: (s.shape[0] // steps, s.shape[1])
    side_specs = [pl.BlockSpec(slab(s), lambda j, i, rows: (j * TI + i, 0)) for s in side_casts]
    grid_spec = pltpu.PrefetchScalarGridSpec(
        num_scalar_prefetch=1,
        grid=(nt, TI),
        in_specs=[pl.BlockSpec((tm, K), lambda j, i, rows: (i, 0)),
                  pl.BlockSpec((pl.Element(1), pl.Element(tn), pl.Element(K)),
                               lambda j, i, rows: (layer, pl.multiple_of(rows[j], SUBLANES), 0)),
                  pl.BlockSpec((1, tn), lambda j, i, rows: (0, j))] + side_specs,
        out_specs=[pl.BlockSpec((tm, tn), lambda j, i, rows: (i, j))] + side_specs,
        scratch_shapes=[pltpu.VMEM((tn, K), BF16)],
    )
    blocks = (_nbytes((tm, K), a.dtype) + _nbytes((tn, K), w_t.dtype)
              + _nbytes((tm, tn), out_dtype)
              + sum(_nbytes(slab(s), F32) + _nbytes(slab(s), BF16) for s in side_casts))
    outs = pl.pallas_call(
        functools.partial(_mm_nt_body, n_side=len(side_casts)),
        grid_spec=grid_spec,
        out_shape=[jax.ShapeDtypeStruct((M, nt * tn), out_dtype)]
        + [jax.ShapeDtypeStruct(s.shape, BF16) for s in side_casts],
        compiler_params=pltpu.CompilerParams(
            dimension_semantics=("arbitrary", "arbitrary"),
            vmem_limit_bytes=_vmem_limit(blocks, _nbytes((tn, K), BF16) + _nbytes((tm, tn), F32))),
        name=name,
    )(jnp.asarray(src_rows, jnp.int32), a, w_t, col_scale, *side_casts)
    return outs[0], list(outs[1:])


def _fcum_body(ff_ref, bias_ref, c_ref, *, groups):
    z = ff_ref[...] + bias_ref[...]
    x = _log_sigmoid(z)
    rows, lanes = x.shape
    lane = lax.broadcasted_iota(jnp.int32, x.shape, 1)
    s = 1
    while s < lanes:
        x = x + jnp.where(lane >= s, pltpu.roll(x, s, axis=1), 0.0)
        s *= 2
    tot = jnp.broadcast_to(x[:, lanes - 1:lanes], x.shape)
    grp = lax.broadcasted_iota(jnp.int32, x.shape, 0) % groups
    inc = tot
    s = 1
    while s < groups:
        inc = inc + jnp.where(grp >= s, pltpu.roll(inc, s, axis=0), 0.0)
        s *= 2
    c_ref[...] = x + (inc - tot)


def _forget_cumsum(ff, b_forget, B, S):
    H = ff.shape[1]
    groups = S // LANES
    ff_t = ff.reshape(B, S, H).transpose(0, 2, 1).reshape(B * H * groups, LANES)
    bias = jnp.broadcast_to(b_forget.reshape(1, H, 1, 1),
                            (B, H, groups, LANES)).reshape(B * H * groups, LANES)
    c = pl.pallas_call(
        functools.partial(_fcum_body, groups=groups),
        out_shape=jax.ShapeDtypeStruct(ff_t.shape, F32),
        name="fox_forget_cumsum",
    )(ff_t, bias)
    return c.reshape(B, H, S)


def _gla_body(q_ref, k_ref, v_ref, r_ref, lr_ref, wlr_ref, blr_ref, g_ref, xb_ref, wt_ref,
              o_ref, pm_ref, st_ref, wb_ref, *, C, DK, DV, G, row_tiles):
    c = pl.program_id(2)
    step = pl.program_id(0) * pl.num_programs(2) + c

    @pl.when(c == 0)
    def _():
        st_ref[...] = jnp.zeros_like(st_ref)

    @pl.when(step % row_tiles == 0)
    def _():
        wb_ref[...] = wt_ref[0].astype(BF16)

    def store_head(g):
        def store(value):
            o_ref[:, g * DV:(g + 1) * DV] = value.astype(o_ref.dtype)
        return store

    lr = lr_ref[...].astype(BF16)
    chains = [(_side_projection(xb_ref, wb_ref, pm_ref), None)]
    for g in range(G):
        kcols = slice(g * DK, (g + 1) * DK)
        vcols = slice(g * DV, (g + 1) * DV)
        chains.append((_gla_head(
            q_ref[:, kcols], k_ref[:, kcols], v_ref[:, vcols], r_ref[:, vcols], lr,
            wlr_ref[:, kcols], blr_ref[:, kcols], g_ref[...], st_ref.at[g],
            C=C, DK=DK, DV=DV), store_head(g)))
    while chains:
        for chain in list(chains):
            try:
                next(chain[0])
            except StopIteration as done:
                if chain[1] is not None:
                    chain[1](done.value)
                chains.remove(chain)


def _side_projection(xb_ref, wb_ref, out_ref, parts=2):
    pr = out_ref.shape[0] // parts
    for col in range(0, out_ref.shape[1], MXU_COLS):
        for r in range(parts):
            rows = slice(r * pr, (r + 1) * pr)
            out_ref[rows, col:col + MXU_COLS] = lax.dot_general(
                xb_ref[rows, :], wb_ref[col:col + MXU_COLS, :], (((1,), (1,)), ((), ())),
                preferred_element_type=F32).astype(out_ref.dtype)
            yield


def _gla_head(q, k, v, gate, lr, wlr, blr, norm_g, st_ref, *, C, DK, DV):
    z = jnp.dot(lr, wlr, preferred_element_type=F32) + blr
    yield
    la = _log_sigmoid(z) * (LOG2E / GLA_TAU)

    row = lax.broadcasted_iota(jnp.int32, (C, DK), 0)
    ti = lax.broadcasted_iota(jnp.int32, (C, C), 0)
    si = lax.broadcasted_iota(jnp.int32, (C, C), 1)
    tril = jnp.where(si <= ti, 1.0, 0.0).astype(BF16)
    hi = la.astype(BF16)
    rest = la - hi.astype(F32)
    mid = rest.astype(BF16)
    lo = (rest - mid.astype(F32)).astype(BF16)
    yield
    b = (jnp.dot(tril, hi, preferred_element_type=F32)
         + jnp.dot(tril, mid, preferred_element_type=F32)
         + jnp.dot(tril, lo, preferred_element_type=F32))
    yield

    st = st_ref[...]
    qe = (q * jnp.exp2(b)).astype(BF16)
    inter = lax.dot_general(qe, st.astype(BF16), (((1,), (1,)), ((), ())),
                            preferred_element_type=F32)
    blast = b[C - 1:C, :]
    kd = (k * jnp.exp2(blast - b)).astype(BF16)
    yield

    pair = jnp.where(ti > si, ti ^ si, 0)
    attn = jnp.where(ti == si, jnp.sum(q * k, axis=-1, keepdims=True), 0.0)
    NB = C // SUBLANES
    b3 = b.reshape(NB, SUBLANES, DK)
    sub3 = lax.broadcasted_iota(jnp.int32, (NB, SUBLANES, DK), 1)
    m = C // 2
    while m >= 1:
        blk = 2 * m
        in_b = (row & (blk - 1)) >= m
        if m == 1:
            e = jnp.exp2(la)
        else:
            if blk >= SUBLANES:
                nblk = C // blk
                r = jnp.broadcast_to(b.reshape(nblk, blk, DK)[:, m - 1:m, :],
                                     (nblk, blk, DK)).reshape(C, DK)
            else:
                r3 = None
                for start in range(0, SUBLANES, blk):
                    cand = jnp.broadcast_to(b3[:, start + m - 1:start + m, :], b3.shape)
                    r3 = cand if r3 is None else jnp.where(sub3 >= start, cand, r3)
                r = r3.reshape(C, DK)
            e = jnp.exp2(-jnp.abs(b - r))
        x = jnp.where(in_b, q, k)
        x = (jnp.where(in_b, x * e, x) if m == 1 else x * e).astype(BF16)
        a = lax.dot_general(x, x, (((1,), (1,)), ((), ())),
                            preferred_element_type=F32)
        attn = jnp.where((pair >> (m.bit_length() - 1)) == 1, a, attn)
        m //= 2
        yield

    o = inter + jnp.dot(attn.astype(BF16), v, preferred_element_type=F32)
    yield

    upd = lax.dot_general(v, kd, (((0,), (0,)), ((), ())),
                          preferred_element_type=F32)
    st_ref[...] = st * jnp.exp2(blast) + upd
    yield

    ms = jnp.mean(o * o, axis=-1, keepdims=True)
    on = o * lax.rsqrt(ms + RMS_EPS) * norm_g
    return on * (gate * jax.nn.sigmoid(gate))


def _gla(proj_f, proj_b, proj_s, w_lr, b_lr, norm_g, B, S, cols, xb, w_t, layer, side_rows,
         side_tile=(1024, 512)):
    H = GLA_HEADS
    G = GLA_GROUP
    DK = w_lr.shape[1] // H
    DV = norm_g.shape[1]
    C = min(GLA_CHUNK, S)
    NC = S // C
    WK, WV = G * DK, G * DV
    M, K = xb.shape
    r0, n_side = side_rows
    tm, tn = min(side_tile[0], M), side_tile[1]
    TI = M // tm
    assert S % C == 0 and C % (2 * SUBLANES) == 0 and H == G
    assert M % tm == 0 and n_side % tn == 0 and TI * (n_side // tn) == B * NC
    assert r0 % SUBLANES == 0 and r0 + n_side <= w_t.shape[1]
    assert all(cols[n] % WK == 0 for n in ("gq", "gk")) and all(cols[n] % WV == 0 for n in ("gr", "gv"))
    qo, ko, ro, vo = (cols["gq"] // WK, cols["gk"] // WK, cols["gr"] // WV, cols["gv"] // WV)
    rows = lambda b, h, c: b * NC + c
    blocks = (2 * _nbytes((C, WK), F32) + _nbytes((C, WV), BF16) + _nbytes((C, WV), F32)
              + _nbytes((C, LANES), F32) + _nbytes((LANES, WK), BF16) + _nbytes((C, WV), BF16)
              + _nbytes((tm, K), BF16) + _nbytes((tn, K), F32) + _nbytes((tm, tn), F32))
    return pl.pallas_call(
        functools.partial(_gla_body, C=C, DK=DK, DV=DV, G=G, row_tiles=TI),
        grid=(B, H // G, NC),
        in_specs=[
            pl.BlockSpec((C, WK), lambda b, h, c: (rows(b, h, c), qo + h)),
            pl.BlockSpec((C, WK), lambda b, h, c: (rows(b, h, c), ko + h)),
            pl.BlockSpec((C, WV), lambda b, h, c: (rows(b, h, c), vo + h)),
            pl.BlockSpec((C, WV), lambda b, h, c: (rows(b, h, c), ro + h)),
            pl.BlockSpec((C, LANES), lambda b, h, c: (rows(b, h, c), 0)),
            pl.BlockSpec((LANES, WK), lambda b, h, c: (0, h)),
            pl.BlockSpec((1, WK), lambda b, h, c: (0, h)),
            pl.BlockSpec((1, DV), lambda b, h, c: (0, 0)),
            pl.BlockSpec((tm, K), lambda b, h, c: (rows(b, h, c) % TI, 0)),
            pl.BlockSpec((pl.Element(1), pl.Element(tn), pl.Element(K)),
                         lambda b, h, c: (layer, pl.multiple_of(
                             r0 + tn * (rows(b, h, c) // TI), SUBLANES), 0)),
        ],
        out_specs=[pl.BlockSpec((C, WV), lambda b, h, c: (rows(b, h, c), h)),
                   pl.BlockSpec((tm, tn), lambda b, h, c: (rows(b, h, c) % TI,
                                                           rows(b, h, c) // TI))],
        out_shape=[jax.ShapeDtypeStruct((B * S, H * DV), BF16),
                   jax.ShapeDtypeStruct((M, n_side), F32)],
        scratch_shapes=[pltpu.VMEM((G, DV, DK), F32), pltpu.VMEM((tn, K), BF16)],
        compiler_params=pltpu.CompilerParams(
            dimension_semantics=("arbitrary", "arbitrary", "arbitrary"),
            vmem_limit_bytes=_vmem_limit(blocks, _nbytes((tn, K), BF16) + G * (12 << 20))),
        name="gla_chunked",
    )(proj_f, proj_f, proj_b, proj_f, proj_s, w_lr, b_lr, norm_g, xb, w_t)


def _fox_body(q_ref, k_ref, v_ref, c_ref, o_ref, m_ref, l_ref, acc_ref, cqb_ref, *, T, G):
    qi = pl.program_id(2)
    lane_tiles = T // LANES
    for g in range(G):
        cqb_ref[g] = jnp.broadcast_to(c_ref[g, qi] * LOG2E, (LANES, T)).T
    m_ref[...] = jnp.full_like(m_ref, -jnp.inf)
    l_ref[...] = jnp.zeros_like(l_ref)
    acc_ref[...] = jnp.zeros_like(acc_ref)

    def qk(ki):
        start = pl.multiple_of(ki * T, T)
        return [lax.dot_general(
            q_ref[:, g * LANES:(g + 1) * LANES], k_ref[pl.ds(start, T), g * LANES:(g + 1) * LANES],
            (((1,), (1,)), ((), ())), preferred_element_type=F32) for g in range(G)]

    def softmax_pv(ki, scores, on_diagonal):
        start = pl.multiple_of(ki * T, T)
        for g in range(G):
            head = slice(g * LANES, (g + 1) * LANES)
            t = scores[g] - c_ref[g, ki] * LOG2E
            if on_diagonal:
                ti = lax.broadcasted_iota(jnp.int32, (T, T), 0)
                si = lax.broadcasted_iota(jnp.int32, (T, T), 1)
                t = jnp.where(si <= ti, t, -jnp.inf)
            cq = cqb_ref[g]
            m_prev = m_ref[g]
            m_new = jnp.maximum(m_prev, cq + jnp.max(t, axis=-1, keepdims=True))
            p = jnp.exp2(t + jnp.concatenate([cq - m_new] * lane_tiles, axis=1))
            alpha = jnp.exp2(m_prev - m_new)
            l_ref[g] = alpha * l_ref[g] + jnp.sum(p, axis=-1, keepdims=True)
            acc_ref[g] = alpha * acc_ref[g] + jnp.dot(
                p.astype(BF16), v_ref[pl.ds(start, T), head], preferred_element_type=F32)
            m_ref[g] = m_new

    def pair(kp, carry):
        sa = qk(2 * kp)
        sb = qk(2 * kp + 1)
        softmax_pv(2 * kp, sa, False)
        softmax_pv(2 * kp + 1, sb, False)
        return carry

    lax.fori_loop(0, qi // 2, pair, 0)

    @pl.when(qi % 2 == 1)
    def _():
        softmax_pv(qi - 1, qk(qi - 1), False)

    softmax_pv(qi, qk(qi), True)
    for g in range(G):
        o_ref[:, g * LANES:(g + 1) * LANES] = (acc_ref[g] / l_ref[g]).astype(o_ref.dtype)


def _fox(proj_b, c, B, S, cols, dh):
    H = FOX_HEADS
    G = FOX_GROUP
    T = min(FOX_TILE, S)
    NT = S // T
    W = G * dh
    assert S % T == 0 and H % G == 0 and dh == LANES
    qo, ko, vo = cols["fq"] // W, cols["fk"] // W, cols["fv"] // W
    assert all(cols[n] % W == 0 for n in ("fq", "fk", "fv"))
    c_row = c.reshape(B, H, NT, 1, T)
    blocks = (2 * _nbytes((T, W), BF16) + 2 * _nbytes((S, W), BF16)
              + G * _nbytes((NT, SUBLANES, T), F32))
    return pl.pallas_call(
        functools.partial(_fox_body, T=T, G=G),
        grid=(B, H // G, NT),
        in_specs=[
            pl.BlockSpec((T, W), lambda b, h, qi: (b * NT + qi, qo + h)),
            pl.BlockSpec((S, W), lambda b, h, qi: (b, ko + h)),
            pl.BlockSpec((S, W), lambda b, h, qi: (b, vo + h)),
            pl.BlockSpec((None, G, NT, 1, T), lambda b, h, qi: (b, h, 0, 0, 0)),
        ],
        out_specs=pl.BlockSpec((T, W), lambda b, h, qi: (b * NT + qi, h)),
        out_shape=jax.ShapeDtypeStruct((B * S, H * dh), BF16),
        scratch_shapes=[pltpu.VMEM((G, T, LANES), F32), pltpu.VMEM((G, T, LANES), F32),
                        pltpu.VMEM((G, T, dh), F32), pltpu.VMEM((G, T, LANES), F32)],
        compiler_params=pltpu.CompilerParams(
            dimension_semantics=("parallel", "parallel", "arbitrary"),
            vmem_limit_bytes=_vmem_limit(blocks, (4 * G + 8) * _nbytes((T, T), F32))),
        name="fox_attention",
    )(proj_b, proj_b, proj_b, c_row)


def _merge_body(og_ref, of_ref, wg_ref, wf_ref, ma_ref, mb_ref, o_ref):
    og = og_ref[...]
    of = of_ref[...]
    for c in range(0, o_ref.shape[1], MXU_COLS):
        cols = slice(c, c + MXU_COLS)
        yg = jnp.dot(og, wg_ref[:, cols], preferred_element_type=F32)
        yf = jnp.dot(of, wf_ref[:, cols], preferred_element_type=F32)
        o_ref[:, cols] = (jax.nn.sigmoid(ma_ref[:, cols]) * yg
                          + jax.nn.sigmoid(mb_ref[:, cols]) * yf).astype(o_ref.dtype)


def _merge(og, of, w_bg, w_bf, proj_f, cols, tm=1024, tn=1024):
    M, KG = og.shape
    KF = of.shape[1]
    N = w_bg.shape[1]
    tm, tn = min(tm, M), min(tn, N)
    ao, bo = cols["ma"] // tn, cols["mb"] // tn
    assert cols["ma"] % tn == 0 and cols["mb"] % tn == 0
    blocks = (_nbytes((tm, KG + KF), BF16) + _nbytes((KG + KF, tn), BF16)
              + 2 * _nbytes((tm, tn), F32) + _nbytes((tm, tn), BF16))
    return pl.pallas_call(
        _merge_body,
        grid=(N // tn, M // tm),
        in_specs=[
            pl.BlockSpec((tm, KG), lambda j, i: (i, 0)),
            pl.BlockSpec((tm, KF), lambda j, i: (i, 0)),
            pl.BlockSpec((KG, tn), lambda j, i: (0, j)),
            pl.BlockSpec((KF, tn), lambda j, i: (0, j)),
            pl.BlockSpec((tm, tn), lambda j, i: (i, ao + j)),
            pl.BlockSpec((tm, tn), lambda j, i: (i, bo + j)),
        ],
        out_specs=pl.BlockSpec((tm, tn), lambda j, i: (i, j)),
        out_shape=jax.ShapeDtypeStruct((M, N), BF16),
        compiler_params=pltpu.CompilerParams(
            dimension_semantics=("parallel", "parallel"),
            vmem_limit_bytes=_vmem_limit(blocks, 3 * _nbytes((tm, tn), F32))),
        name="branch_merge",
    )(og, of, w_bg, w_bf, proj_f, proj_f)


def _proj_ln_body(a_ref, w_ref, x_ref, g_ref, b_ref, o_ref, ob_ref, *, alpha, parts):
    pr = o_ref.shape[0] // parts
    ys = [jnp.dot(a_ref[r * pr:(r + 1) * pr, :], w_ref[...], preferred_element_type=F32)
          for r in range(parts)]
    for r in range(parts):
        rows = slice(r * pr, (r + 1) * pr)
        out = _layer_norm(alpha * x_ref[rows, :] + ys[r], g_ref[...], b_ref[...])
        o_ref[rows, :] = out
        ob_ref[rows, :] = out.astype(BF16)


def _proj_ln(a, w, resid, g, b, alpha, tm=512):
    M, K = a.shape
    N = w.shape[1]
    tm = min(tm, M)
    blocks = (_nbytes((tm, K), BF16) + _nbytes((K, N), BF16) + 2 * _nbytes((tm, N), F32)
              + _nbytes((tm, N), BF16))
    return pl.pallas_call(
        functools.partial(_proj_ln_body, alpha=alpha, parts=2),
        grid=(M // tm,),
        in_specs=[
            pl.BlockSpec((tm, K), lambda i: (i, 0)),
            pl.BlockSpec((K, N), lambda i: (0, 0)),
            pl.BlockSpec((tm, N), lambda i: (i, 0)),
            pl.BlockSpec((1, N), lambda i: (0, 0)),
            pl.BlockSpec((1, N), lambda i: (0, 0)),
        ],
        out_specs=[pl.BlockSpec((tm, N), lambda i: (i, 0)),
                   pl.BlockSpec((tm, N), lambda i: (i, 0))],
        out_shape=[jax.ShapeDtypeStruct((M, N), F32), jax.ShapeDtypeStruct((M, N), BF16)],
        compiler_params=pltpu.CompilerParams(
            dimension_semantics=("parallel",),
            vmem_limit_bytes=_vmem_limit(blocks, 3 * _nbytes((tm, N), F32))),
        name="out_proj_layernorm",
    )(a, w, resid, g, b)


def _ffn_up_body(x_ref, wg_ref, wu_ref, cw_ref, cb_ref, h_ref, wgb_ref, wub_ref, gbuf_ref,
                 *, tm, tiles_per_seq, pr, pc):
    i = pl.program_id(1)
    HALO = SUBLANES
    tn = h_ref.shape[1]

    @pl.when(i == 0)
    def _():
        wgb_ref[...] = wg_ref[...].astype(BF16)
        wub_ref[...] = wu_ref[...].astype(BF16)

    @pl.when(i % tiles_per_seq == 0)
    def _():
        gbuf_ref[0:HALO, :] = jnp.zeros((HALO, tn), F32)

    cw = cw_ref[...]
    cb = cb_ref[...]
    parts = [(r, c) for c in range(tn // pc) for r in range(tm // pr)]

    def matmuls(r, c):
        x = x_ref[r * pr:(r + 1) * pr, :]
        cols = slice(c * pc, (c + 1) * pc)
        g = jnp.dot(x, wgb_ref[:, cols], preferred_element_type=F32)
        u = jnp.dot(x, wub_ref[:, cols], preferred_element_type=F32)
        gbuf_ref[HALO + r * pr:HALO + (r + 1) * pr, cols] = g
        return g, u

    def finish(r, c, g, u):
        cols = slice(c * pc, (c + 1) * pc)
        y = cb[:, cols] + cw[CONV_W - 1:CONV_W, cols] * g
        for j in range(CONV_W - 1):
            off = HALO - (CONV_W - 1) + j + r * pr
            y = y + cw[j:j + 1, cols] * gbuf_ref[off:off + pr, cols]
        h_ref[r * pr:(r + 1) * pr, cols] = (jax.nn.gelu(y, approximate=True) * u).astype(h_ref.dtype)

    pending = None
    for r, c in parts:
        gu = matmuls(r, c)
        if pending is not None:
            finish(*pending)
        pending = (r, c) + gu
    finish(*pending)
    gbuf_ref[0:HALO, :] = gbuf_ref[tm:tm + HALO, :]


def _ffn_up(xb, w_gate, w_up, conv_w, conv_b, S, tm=2048, tn=512, pr=512, pc=256):
    M, K = xb.shape
    N = w_gate.shape[1]
    tm, tn = min(tm, S), min(tn, N)
    pr, pc = min(pr, tm), min(pc, tn)
    assert S % tm == 0 and N % tn == 0 and tm % pr == 0 and tn % pc == 0
    blocks = (_nbytes((tm, K), BF16) + 2 * _nbytes((K, tn), F32) + _nbytes((tm, tn), BF16)
              + _nbytes((8, tn), F32) * 2)
    scratch = 2 * _nbytes((K, tn), BF16) + _nbytes((tm + SUBLANES, tn), F32)
    return pl.pallas_call(
        functools.partial(_ffn_up_body, tm=tm, tiles_per_seq=S // tm, pr=pr, pc=pc),
        grid=(N // tn, M // tm),
        in_specs=[
            pl.BlockSpec((tm, K), lambda j, i: (i, 0)),
            pl.BlockSpec((K, tn), lambda j, i: (0, j)),
            pl.BlockSpec((K, tn), lambda j, i: (0, j)),
            pl.BlockSpec((CONV_W, tn), lambda j, i: (0, j)),
            pl.BlockSpec((1, tn), lambda j, i: (0, j)),
        ],
        out_specs=pl.BlockSpec((tm, tn), lambda j, i: (i, j)),
        out_shape=jax.ShapeDtypeStruct((M, N), BF16),
        scratch_shapes=[pltpu.VMEM((K, tn), BF16), pltpu.VMEM((K, tn), BF16),
                        pltpu.VMEM((tm + SUBLANES, tn), F32)],
        compiler_params=pltpu.CompilerParams(
            dimension_semantics=("parallel", "arbitrary"),
            vmem_limit_bytes=_vmem_limit(blocks, scratch + 8 * _nbytes((pr, pc), F32))),
        name="ffn_gate_up",
    )(xb, w_gate, w_up, conv_w, conv_b)


def _ffn_down_body(h_ref, w_ref, x_ref, g_ref, b_ref, o_ref, *, alpha):
    kk = pl.program_id(1)

    @pl.when(kk == 0)
    def _():
        o_ref[...] = alpha * x_ref[...]

    o_ref[...] += jnp.dot(h_ref[...], w_ref[...], preferred_element_type=F32)

    @pl.when(kk == pl.num_programs(1) - 1)
    def _():
        o_ref[...] = _layer_norm(o_ref[...], g_ref[...], b_ref[...])


def _ffn_down(h, w_down, resid, g, b, alpha, tm=512, tk=2816):
    M, K = h.shape
    N = w_down.shape[1]
    tm, tk = min(tm, M), min(tk, K)
    assert M % tm == 0 and K % tk == 0
    blocks = (_nbytes((tm, tk), BF16) + _nbytes((tk, N), BF16) + 2 * _nbytes((tm, N), F32))
    return pl.pallas_call(
        functools.partial(_ffn_down_body, alpha=alpha),
        grid=(M // tm, K // tk),
        in_specs=[
            pl.BlockSpec((tm, tk), lambda i, kk: (i, kk)),
            pl.BlockSpec((tk, N), lambda i, kk: (kk, 0)),
            pl.BlockSpec((tm, N), lambda i, kk: (i, 0)),
            pl.BlockSpec((1, N), lambda i, kk: (0, 0)),
            pl.BlockSpec((1, N), lambda i, kk: (0, 0)),
        ],
        out_specs=pl.BlockSpec((tm, N), lambda i, kk: (i, 0)),
        out_shape=jax.ShapeDtypeStruct((M, N), F32),
        compiler_params=pltpu.CompilerParams(
            dimension_semantics=("parallel", "arbitrary"),
            vmem_limit_bytes=_vmem_limit(blocks, _nbytes((tm, N), F32))),
        name="ffn_down_layernorm",
    )(h, w_down, resid, g, b)


def _ple_body(x_ref, wg_ref, p_ref, wp_ref, o_ref, xb_ref, pb_ref):
    j = pl.program_id(1)
    tn = o_ref.shape[1]

    @pl.when(j == 0)
    def _():
        xb_ref[...] = x_ref[...].astype(BF16)
        pb_ref[...] = p_ref[...].astype(BF16)

    xb = xb_ref[...]
    pb = pb_ref[...]
    for c in range(0, tn, MXU_COLS):
        cols = slice(c, c + MXU_COLS)
        gate = jax.nn.sigmoid(jnp.dot(xb, wg_ref[:, cols], preferred_element_type=F32))
        emb = jnp.dot(pb, wp_ref[:, cols], preferred_element_type=F32)
        resid = x_ref[:, pl.ds(pl.multiple_of(j * tn + c, MXU_COLS), MXU_COLS)]
        o_ref[:, cols] = resid + gate * emb


def _ple(x, p, w_gate, w_proj, tm=1024, tn=1024):
    M, K = x.shape
    N = w_gate.shape[1]
    P = p.shape[1]
    tm, tn = min(tm, M), min(tn, N)
    assert K == N and M % tm == 0 and N % tn == 0
    blocks = (_nbytes((tm, K), F32) + _nbytes((K, tn), BF16) + _nbytes((tm, P), F32)
              + _nbytes((P, tn), BF16) + _nbytes((tm, tn), F32))
    scratch = _nbytes((tm, K), BF16) + _nbytes((tm, P), BF16)
    return pl.pallas_call(
        _ple_body,
        grid=(M // tm, N // tn),
        in_specs=[
            pl.BlockSpec((tm, K), lambda i, j: (i, 0)),
            pl.BlockSpec((K, tn), lambda i, j: (0, j)),
            pl.BlockSpec((tm, P), lambda i, j: (i, 0)),
            pl.BlockSpec((P, tn), lambda i, j: (0, j)),
        ],
        out_specs=pl.BlockSpec((tm, tn), lambda i, j: (i, j)),
        out_shape=jax.ShapeDtypeStruct((M, N), F32),
        scratch_shapes=[pltpu.VMEM((tm, K), BF16), pltpu.VMEM((tm, P), BF16)],
        compiler_params=pltpu.CompilerParams(
            dimension_semantics=("parallel", "arbitrary"),
            vmem_limit_bytes=_vmem_limit(blocks, scratch + 3 * _nbytes((tm, tn), F32))),
        name="ple_gate",
    )(x, w_gate, p, w_proj)


def _cast_small_body(x_ref, w_ref, xb_ref, o_ref):
    xb = x_ref[...].astype(BF16)
    xb_ref[...] = xb
    o_ref[...] = lax.dot_general(xb, w_ref[...].astype(BF16), (((1,), (1,)), ((), ())),
                                 preferred_element_type=F32)


def _cast_and_small_proj(x, w_small, tm=1024):
    M, K = x.shape
    ns = w_small.shape[0]
    tm = min(tm, M)
    assert M % tm == 0
    blocks = _nbytes((tm, K), F32) + _nbytes((ns, K), F32) + _nbytes((tm, K), BF16) \
        + _nbytes((tm, ns), F32)
    return pl.pallas_call(
        _cast_small_body,
        grid=(M // tm,),
        in_specs=[pl.BlockSpec((tm, K), lambda i: (i, 0)),
                  pl.BlockSpec((ns, K), lambda i: (0, 0))],
        out_specs=[pl.BlockSpec((tm, K), lambda i: (i, 0)),
                   pl.BlockSpec((tm, ns), lambda i: (i, 0))],
        out_shape=[jax.ShapeDtypeStruct((M, K), BF16), jax.ShapeDtypeStruct((M, ns), F32)],
        compiler_params=pltpu.CompilerParams(
            dimension_semantics=("parallel",),
            vmem_limit_bytes=_vmem_limit(blocks, _nbytes((tm, K), BF16))),
        name="in_proj_small",
    )(x, w_small)


def _in_proj(x, w_in, layer, rank, later_weights):
    M, D = x.shape
    gla_qk = D // 2
    fox_w = FOX_HEADS * LANES
    names = ("gq", "gk", "gv", "gr", "glr", "fq", "fk", "fv", "ff", "ma", "mb")
    widths = (gla_qk, gla_qk, D, D, rank, fox_w, fox_w, fox_w, FOX_HEADS, D, D)
    assert sum(widths) == w_in.shape[2]
    width, offset, off = {}, {}, 0
    for n, wd in zip(names, widths):
        width[n], offset[n] = wd, off
        off += wd
    w_t = jnp.swapaxes(w_in, 1, 2)

    out_scale = {"gq": (gla_qk // GLA_HEADS) ** -0.5, "fq": LANES ** -0.5 * LOG2E}
    T = IN_PROJ_TILE

    def project(order, out_dtype, name, side_casts=()):
        src_rows, scales, cols, o = [], [], {}, 0
        for n in order:
            assert width[n] % T == 0
            cols[n] = o
            o += width[n]
            src_rows += [offset[n] + t for t in range(0, width[n], T)]
            scales.append(jnp.full((1, width[n]), out_scale.get(n, 1.0), F32))
        out, casts = _matmul_nt(xb, w_t, layer, src_rows, jnp.concatenate(scales, axis=1),
                                out_dtype, 1024, T, name, side_casts)
        return (out, cols), casts

    small, cols_s = [], {}
    for t, n in enumerate(("glr", "ff")):
        wseg = w_t[layer, offset[n]:offset[n] + width[n], :]
        small.append(jnp.pad(wseg, ((0, LANES - width[n]), (0, 0))))
        cols_s[n] = t * LANES
    w_small = jnp.concatenate(small, axis=0)
    xb, proj_s = _cast_and_small_proj(x, w_small)

    proj_b, _ = project(("gv", "fq", "fk", "fv"), BF16, "in_proj_bf16")
    proj_f, weights_bf16 = project(("gq", "gk", "gr"), F32, "in_proj_f32", later_weights)
    assert offset["mb"] == offset["ma"] + width["ma"]
    gates = (xb, w_t, (offset["ma"], width["ma"] + width["mb"]), {"ma": 0, "mb": width["ma"]})
    return proj_b, proj_f, (proj_s, cols_s), gates, weights_bf16


def kernel(x, p, w_in, w_gla_lr, b_gla_lr, gla_norm_g, b_forget, w_branch_gla, w_branch_fox,
           w_out, ln1_g, ln1_b, w_gate, w_up, conv_w, conv_b, w_down, ln2_g, ln2_b,
           w_ple_gate, w_ple_proj):
    B, S, D = x.shape
    depth = w_in.shape[0]
    alpha = (2 * depth) ** 0.25
    M = B * S
    xf = x.reshape(M, D)
    for i in range(depth):
        rank = w_gla_lr.shape[1]
        later = (w_branch_gla[i], w_branch_fox[i], w_out[i], w_down[i], w_ple_gate[i])
        (proj_b, cols_b), (proj_f, cols_f), (proj_s, cols_s), gates, later_bf16 = _in_proj(
            xf, w_in, i, rank, later)
        xb, w_t, gate_rows, cols_m = gates
        w_bg, w_bf, w_o, w_d, w_pg = later_bf16

        ff = proj_s[:, cols_s["ff"]:cols_s["ff"] + FOX_HEADS]
        c = _forget_cumsum(ff, b_forget[i], B, S)
        of = _fox(proj_b, c, B, S, cols_b, LANES)

        w_lr = jnp.pad(w_gla_lr[i], ((0, LANES - rank), (0, 0))).astype(BF16)
        og, proj_m = _gla(proj_f, proj_b, proj_s, w_lr, b_gla_lr[i].reshape(1, -1),
                          gla_norm_g[i].reshape(1, -1), B, S, {**cols_f, **cols_b},
                          xb, w_t, i, gate_rows)

        merged = _merge(og, of, w_bg, w_bf, proj_m, cols_m)
        x1, x1b = _proj_ln(merged, w_o, xf, ln1_g[i].reshape(1, -1),
                           ln1_b[i].reshape(1, -1), alpha)
        h = _ffn_up(x1b, w_gate[i], w_up[i], conv_w[i], conv_b[i].reshape(1, -1), S)
        x2 = _ffn_down(h, w_d, x1, ln2_g[i].reshape(1, -1), ln2_b[i].reshape(1, -1), alpha)
        xf = _ple(x2, p[i].reshape(M, -1), w_pg, w_ple_proj[i].astype(BF16))
    return xf.reshape(B, S, D)
```

```python
import functools

import jax
import jax.numpy as jnp
from jax import lax
from jax.experimental import pallas as pl
from jax.experimental.pallas import tpu as pltpu

F32 = jnp.float32
BF16 = jnp.bfloat16

GLA_HEADS = 4
GLA_TAU = 16.0
FOX_HEADS = 8
CONV_W = 3
LN_EPS = 1e-5
RMS_EPS = 1e-6
LOG2E = 1.4426950408889634

LANES = 128
SUBLANES = 8
MXU_COLS = 256
VMEM_BYTES_V7X = 64 * 1024 * 1024

IN_PROJ_TILE = 1024
GLA_CHUNK = 128
GLA_GROUP = 4
FOX_TILE = 512
FOX_GROUP = 4


def _vmem_limit(block_bytes, extra_bytes=0):
    est = 2 * block_bytes + extra_bytes + (4 << 20)
    return int(min(max(est, 16 << 20), VMEM_BYTES_V7X - (8 << 20)))


def _nbytes(shape, dtype):
    n = 1
    for s in shape:
        n *= s
    return n * jnp.dtype(dtype).itemsize


def _log_sigmoid(z):
    return jnp.minimum(z, 0.0) - jnp.log(1.0 + jnp.exp(-jnp.abs(z)))


def _layer_norm(y, g, b):
    mu = jnp.mean(y, axis=-1, keepdims=True)
    d = y - mu
    var = jnp.mean(d * d, axis=-1, keepdims=True)
    return d * lax.rsqrt(var + LN_EPS) * g + b


def _mm_nt_body(rows_ref, a_ref, wt_ref, cs_ref, *refs, n_side):
    del rows_ref
    side_in, o_ref, side_out, wb_ref = (refs[:n_side], refs[n_side],
                                        refs[n_side + 1:2 * n_side + 1], refs[-1])

    @pl.when(pl.program_id(1) == 0)
    def _():
        wb_ref[...] = wt_ref[0].astype(BF16)

    a = a_ref[...]
    for c in range(0, o_ref.shape[1], MXU_COLS):
        cols = slice(c, c + MXU_COLS)
        acc = lax.dot_general(a, wb_ref[cols, :], (((1,), (1,)), ((), ())),
                              preferred_element_type=F32)
        o_ref[:, cols] = (acc * cs_ref[:, cols]).astype(o_ref.dtype)
    for src, dst in zip(side_in, side_out):
        dst[...] = src[...].astype(dst.dtype)


def _matmul_nt(a, w_t, layer, src_rows, col_scale, out_dtype, tm, tn, name, side_casts=()):
    M, K = a.shape
    nt = len(src_rows)
    tm = min(tm, M)
    assert M % tm == 0 and col_scale.shape == (1, nt * tn)
    assert all(r % SUBLANES == 0 and r + tn <= w_t.shape[1] for r in src_rows)
    TI = M // tm
    steps = nt * TI
    BF16_ROWS = 2 * SUBLANES
    assert all(s.shape[0] % (steps * BF16_ROWS) == 0 for s in side_casts)
    slab = lambda s: (s.shape[0] // steps, s.shape[1])
    side_specs = [pl.BlockSpec(slab(s), lambda j, i, rows: (j * TI + i, 0)) for s in side_casts]
    grid_spec = pltpu.PrefetchScalarGridSpec(
        num_scalar_prefetch=1,
        grid=(nt, TI),
        in_specs=[pl.BlockSpec((tm, K), lambda j, i, rows: (i, 0)),
                  pl.BlockSpec((pl.Element(1), pl.Element(tn), pl.Element(K)),
                               lambda j, i, rows: (layer, pl.multiple_of(rows[j], SUBLANES), 0)),
                  pl.BlockSpec((1, tn), lambda j, i, rows: (0, j))] + side_specs,
        out_specs=[pl.BlockSpec((tm, tn), lambda j, i, rows: (i, j))] + side_specs,
        scratch_shapes=[pltpu.VMEM((tn, K), BF16)],
    )
    blocks = (_nbytes((tm, K), a.dtype) + _nbytes((tn, K), w_t.dtype)
              + _nbytes((tm, tn), out_dtype)
              + sum(_nbytes(slab(s), F32) + _nbytes(slab(s), BF16) for s in side_casts))
    outs = pl.pallas_call(
        functools.partial(_mm_nt_body, n_side=len(side_casts)),
        grid_spec=grid_spec,
        out_shape=[jax.ShapeDtypeStruct((M, nt * tn), out_dtype)]
        + [jax.ShapeDtypeStruct(s.shape, BF16) for s in side_casts],
        compiler_params=pltpu.CompilerParams(
            dimension_semantics=("arbitrary", "arbitrary"),
            vmem_limit_bytes=_vmem_limit(blocks, _nbytes((tn, K), BF16) + _nbytes((tm, tn), F32))),
        name=name,
    )(jnp.asarray(src_rows, jnp.int32), a, w_t, col_scale, *side_casts)
    return outs[0], list(outs[1:])


def _fcum_body(ff_ref, bias_ref, c_ref, *, groups):
    z = ff_ref[...] + bias_ref[...]
    x = _log_sigmoid(z)
    rows, lanes = x.shape
    lane = lax.broadcasted_iota(jnp.int32, x.shape, 1)
    s = 1
    while s < lanes:
        x = x + jnp.where(lane >= s, pltpu.roll(x, s, axis=1), 0.0)
        s *= 2
    tot = jnp.broadcast_to(x[:, lanes - 1:lanes], x.shape)
    grp = lax.broadcasted_iota(jnp.int32, x.shape, 0) % groups
    inc = tot
    s = 1
    while s < groups:
        inc = inc + jnp.where(grp >= s, pltpu.roll(inc, s, axis=0), 0.0)
        s *= 2
    c_ref[...] = x + (inc - tot)


def _forget_cumsum(ff, b_forget, B, S):
    H = ff.shape[1]
    groups = S // LANES
    ff_t = ff.reshape(B, S, H).transpose(0, 2, 1).reshape(B * H * groups, LANES)
    bias = jnp.broadcast_to(b_forget.reshape(1, H, 1, 1),
                            (B, H, groups, LANES)).reshape(B * H * groups, LANES)
    c = pl.pallas_call(
        functools.partial(_fcum_body, groups=groups),
        out_shape=jax.ShapeDtypeStruct(ff_t.shape, F32),
        name="fox_forget_cumsum",
    )(ff_t, bias)
    return c.reshape(B, H, S)


def _gla_body(q_ref, k_ref, v_ref, r_ref, lr_ref, wlr_ref, blr_ref, g_ref, xb_ref, wt_ref,
              *refs, C, DK, DV, G, row_tiles, n_side):
    side_in, (o_ref, pm_ref) = refs[:n_side], refs[n_side:n_side + 2]
    side_out, (st_ref, wb_ref) = refs[n_side + 2:2 * n_side + 2], refs[2 * n_side + 2:]
    for src, dst in zip(side_in, side_out):
        dst[...] = src[...].astype(dst.dtype)
    c = pl.program_id(2)
    step = pl.program_id(0) * pl.num_programs(2) + c

    @pl.when(c == 0)
    def _():
        st_ref[...] = jnp.zeros_like(st_ref)

    @pl.when(step % row_tiles == 0)
    def _():
        wb_ref[...] = wt_ref[0].astype(BF16)

    def store_head(g):
        def store(value):
            o_ref[:, g * DV:(g + 1) * DV] = value.astype(o_ref.dtype)
        return store

    lr = lr_ref[...].astype(BF16)
    chains = [(_side_projection(xb_ref, wb_ref, pm_ref), None)]
    for g in range(G):
        kcols = slice(g * DK, (g + 1) * DK)
        vcols = slice(g * DV, (g + 1) * DV)
        chains.append((_gla_head(
            q_ref[:, kcols], k_ref[:, kcols], v_ref[:, vcols], r_ref[:, vcols], lr,
            wlr_ref[:, kcols], blr_ref[:, kcols], g_ref[...], st_ref.at[g],
            C=C, DK=DK, DV=DV), store_head(g)))
    while chains:
        for chain in list(chains):
            try:
                next(chain[0])
            except StopIteration as done:
                if chain[1] is not None:
                    chain[1](done.value)
                chains.remove(chain)


def _side_projection(xb_ref, wb_ref, out_ref, parts=2):
    pr = out_ref.shape[0] // parts
    for col in range(0, out_ref.shape[1], MXU_COLS):
        for r in range(parts):
            rows = slice(r * pr, (r + 1) * pr)
            out_ref[rows, col:col + MXU_COLS] = lax.dot_general(
                xb_ref[rows, :], wb_ref[col:col + MXU_COLS, :], (((1,), (1,)), ((), ())),
                preferred_element_type=F32).astype(out_ref.dtype)
            yield


def _gla_head(q, k, v, gate, lr, wlr, blr, norm_g, st_ref, *, C, DK, DV):
    z = jnp.dot(lr, wlr, preferred_element_type=F32) + blr
    yield
    la = _log_sigmoid(z) * (LOG2E / GLA_TAU)

    row = lax.broadcasted_iota(jnp.int32, (C, DK), 0)
    ti = lax.broadcasted_iota(jnp.int32, (C, C), 0)
    si = lax.broadcasted_iota(jnp.int32, (C, C), 1)
    tril = jnp.where(si <= ti, 1.0, 0.0).astype(BF16)
    hi = la.astype(BF16)
    rest = la - hi.astype(F32)
    mid = rest.astype(BF16)
    lo = (rest - mid.astype(F32)).astype(BF16)
    yield
    b = (jnp.dot(tril, hi, preferred_element_type=F32)
         + jnp.dot(tril, mid, preferred_element_type=F32)
         + jnp.dot(tril, lo, preferred_element_type=F32))
    yield

    st = st_ref[...]
    qe = (q * jnp.exp2(b)).astype(BF16)
    inter = lax.dot_general(qe, st.astype(BF16), (((1,), (1,)), ((), ())),
                            preferred_element_type=F32)
    blast = b[C - 1:C, :]
    kd = (k * jnp.exp2(blast - b)).astype(BF16)
    yield

    pair = jnp.where(ti > si, ti ^ si, 0)
    attn = jnp.where(ti == si, jnp.sum(q * k, axis=-1, keepdims=True), 0.0)
    NB = C // SUBLANES
    b3 = b.reshape(NB, SUBLANES, DK)
    sub3 = lax.broadcasted_iota(jnp.int32, (NB, SUBLANES, DK), 1)
    m = C // 2
    while m >= 1:
        blk = 2 * m
        in_b = (row & (blk - 1)) >= m
        if m == 1:
            e = jnp.exp2(la)
        else:
            if blk >= SUBLANES:
                nblk = C // blk
                r = jnp.broadcast_to(b.reshape(nblk, blk, DK)[:, m - 1:m, :],
                                     (nblk, blk, DK)).reshape(C, DK)
            else:
                r3 = None
                for start in range(0, SUBLANES, blk):
                    cand = jnp.broadcast_to(b3[:, start + m - 1:start + m, :], b3.shape)
                    r3 = cand if r3 is None else jnp.where(sub3 >= start, cand, r3)
                r = r3.reshape(C, DK)
            e = jnp.exp2(-jnp.abs(b - r))
        x = jnp.where(in_b, q, k)
        x = (jnp.where(in_b, x * e, x) if m == 1 else x * e).astype(BF16)
        a = lax.dot_general(x, x, (((1,), (1,)), ((), ())),
                            preferred_element_type=F32)
        attn = jnp.where((pair >> (m.bit_length() - 1)) == 1, a, attn)
        m //= 2
        yield

    o = inter + jnp.dot(attn.astype(BF16), v, preferred_element_type=F32)
    yield

    upd = lax.dot_general(v, kd, (((0,), (0,)), ((), ())),
                          preferred_element_type=F32)
    st_ref[...] = st * jnp.exp2(blast) + upd
    yield

    ms = jnp.mean(o * o, axis=-1, keepdims=True)
    on = o * lax.rsqrt(ms + RMS_EPS) * norm_g
    return on * (gate * jax.nn.sigmoid(gate))


def _gla(proj_f, proj_b, proj_s, w_lr, b_lr, norm_g, B, S, cols, xb, w_t, layer, side_rows,
         side_casts=(), side_tile=(1024, 512)):
    H = GLA_HEADS
    G = GLA_GROUP
    DK = w_lr.shape[1] // H
    DV = norm_g.shape[1]
    C = min(GLA_CHUNK, S)
    NC = S // C
    WK, WV = G * DK, G * DV
    M, K = xb.shape
    r0, n_side = side_rows
    tm, tn = min(side_tile[0], M), side_tile[1]
    TI = M // tm
    assert S % C == 0 and C % (2 * SUBLANES) == 0 and H == G
    assert M % tm == 0 and n_side % tn == 0 and TI * (n_side // tn) == B * NC
    assert r0 % SUBLANES == 0 and r0 + n_side <= w_t.shape[1]
    assert all(cols[n] % WK == 0 for n in ("gq", "gk")) and all(cols[n] % WV == 0 for n in ("gr", "gv"))
    qo, ko, ro, vo = (cols["gq"] // WK, cols["gk"] // WK, cols["gr"] // WV, cols["gv"] // WV)
    rows = lambda b, h, c: b * NC + c
    steps = B * NC
    BF16_ROWS = 2 * SUBLANES
    assert all(s.shape[0] % (steps * BF16_ROWS) == 0 for s in side_casts)
    slab = lambda s: (s.shape[0] // steps, s.shape[1])
    side_specs = [pl.BlockSpec(slab(s), lambda b, h, c: (rows(b, h, c), 0)) for s in side_casts]
    blocks = (2 * _nbytes((C, WK), F32) + _nbytes((C, WV), BF16) + _nbytes((C, WV), F32)
              + _nbytes((C, LANES), F32) + _nbytes((LANES, WK), BF16) + _nbytes((C, WV), BF16)
              + _nbytes((tm, K), BF16) + _nbytes((tn, K), F32) + _nbytes((tm, tn), F32)
              + sum(_nbytes(slab(s), F32) + _nbytes(slab(s), BF16) for s in side_casts))
    outs = pl.pallas_call(
        functools.partial(_gla_body, C=C, DK=DK, DV=DV, G=G, row_tiles=TI,
                          n_side=len(side_casts)),
        grid=(B, H // G, NC),
        in_specs=[
            pl.BlockSpec((C, WK), lambda b, h, c: (rows(b, h, c), qo + h)),
            pl.BlockSpec((C, WK), lambda b, h, c: (rows(b, h, c), ko + h)),
            pl.BlockSpec((C, WV), lambda b, h, c: (rows(b, h, c), vo + h)),
            pl.BlockSpec((C, WV), lambda b, h, c: (rows(b, h, c), ro + h)),
            pl.BlockSpec((C, LANES), lambda b, h, c: (rows(b, h, c), 0)),
            pl.BlockSpec((LANES, WK), lambda b, h, c: (0, h)),
            pl.BlockSpec((1, WK), lambda b, h, c: (0, h)),
            pl.BlockSpec((1, DV), lambda b, h, c: (0, 0)),
            pl.BlockSpec((tm, K), lambda b, h, c: (rows(b, h, c) % TI, 0)),
            pl.BlockSpec((pl.Element(1), pl.Element(tn), pl.Element(K)),
                         lambda b, h, c: (layer, pl.multiple_of(
                             r0 + tn * (rows(b, h, c) // TI), SUBLANES), 0)),
        ] + side_specs,
        out_specs=[pl.BlockSpec((C, WV), lambda b, h, c: (rows(b, h, c), h)),
                   pl.BlockSpec((tm, tn), lambda b, h, c: (rows(b, h, c) % TI,
                                                           rows(b, h, c) // TI))] + side_specs,
        out_shape=[jax.ShapeDtypeStruct((B * S, H * DV), BF16),
                   jax.ShapeDtypeStruct((M, n_side), F32)]
        + [jax.ShapeDtypeStruct(s.shape, BF16) for s in side_casts],
        scratch_shapes=[pltpu.VMEM((G, DV, DK), F32), pltpu.VMEM((tn, K), BF16)],
        compiler_params=pltpu.CompilerParams(
            dimension_semantics=("arbitrary", "arbitrary", "arbitrary"),
            vmem_limit_bytes=_vmem_limit(blocks, _nbytes((tn, K), BF16) + G * (12 << 20))),
        name="gla_chunked",
    )(proj_f, proj_f, proj_b, proj_f, proj_s, w_lr, b_lr, norm_g, xb, w_t, *side_casts)
    return outs[0], outs[1], list(outs[2:])


def _fox_body(q_ref, k_ref, v_ref, c_ref, o_ref, m_ref, l_ref, acc_ref, cqb_ref, *, T, G):
    qi = pl.program_id(2)
    lane_tiles = T // LANES
    for g in range(G):
        cqb_ref[g] = jnp.broadcast_to(c_ref[g, qi] * LOG2E, (LANES, T)).T
    m_ref[...] = jnp.full_like(m_ref, -jnp.inf)
    l_ref[...] = jnp.zeros_like(l_ref)
    acc_ref[...] = jnp.zeros_like(acc_ref)

    def qk(ki):
        start = pl.multiple_of(ki * T, T)
        return [lax.dot_general(
            q_ref[:, g * LANES:(g + 1) * LANES], k_ref[pl.ds(start, T), g * LANES:(g + 1) * LANES],
            (((1,), (1,)), ((), ())), preferred_element_type=F32) for g in range(G)]

    def softmax_pv(ki, scores, on_diagonal):
        start = pl.multiple_of(ki * T, T)
        for g in range(G):
            head = slice(g * LANES, (g + 1) * LANES)
            t = scores[g] - c_ref[g, ki] * LOG2E
            if on_diagonal:
                ti = lax.broadcasted_iota(jnp.int32, (T, T), 0)
                si = lax.broadcasted_iota(jnp.int32, (T, T), 1)
                t = jnp.where(si <= ti, t, -jnp.inf)
            cq = cqb_ref[g]
            m_prev = m_ref[g]
            m_new = jnp.maximum(m_prev, cq + jnp.max(t, axis=-1, keepdims=True))
            p = jnp.exp2(t + jnp.concatenate([cq - m_new] * lane_tiles, axis=1))
            alpha = jnp.exp2(m_prev - m_new)
            l_ref[g] = alpha * l_ref[g] + jnp.sum(p, axis=-1, keepdims=True)
            acc_ref[g] = alpha * acc_ref[g] + jnp.dot(
                p.astype(BF16), v_ref[pl.ds(start, T), head], preferred_element_type=F32)
            m_ref[g] = m_new

    def pair(kp, carry):
        sa = qk(2 * kp)
        sb = qk(2 * kp + 1)
        softmax_pv(2 * kp, sa, False)
        softmax_pv(2 * kp + 1, sb, False)
        return carry

    lax.fori_loop(0, qi // 2, pair, 0)

    @pl.when(qi % 2 == 1)
    def _():
        softmax_pv(qi - 1, qk(qi - 1), False)

    softmax_pv(qi, qk(qi), True)
    for g in range(G):
        o_ref[:, g * LANES:(g + 1) * LANES] = (acc_ref[g] / l_ref[g]).astype(o_ref.dtype)


def _fox(proj_b, c, B, S, cols, dh):
    H = FOX_HEADS
    G = FOX_GROUP
    T = min(FOX_TILE, S)
    NT = S // T
    W = G * dh
    assert S % T == 0 and H % G == 0 and dh == LANES
    qo, ko, vo = cols["fq"] // W, cols["fk"] // W, cols["fv"] // W
    assert all(cols[n] % W == 0 for n in ("fq", "fk", "fv"))
    c_row = c.reshape(B, H, NT, 1, T)
    blocks = (2 * _nbytes((T, W), BF16) + 2 * _nbytes((S, W), BF16)
              + G * _nbytes((NT, SUBLANES, T), F32))
    return pl.pallas_call(
        functools.partial(_fox_body, T=T, G=G),
        grid=(B, H // G, NT),
        in_specs=[
            pl.BlockSpec((T, W), lambda b, h, qi: (b * NT + qi, qo + h)),
            pl.BlockSpec((S, W), lambda b, h, qi: (b, ko + h)),
            pl.BlockSpec((S, W), lambda b, h, qi: (b, vo + h)),
            pl.BlockSpec((None, G, NT, 1, T), lambda b, h, qi: (b, h, 0, 0, 0)),
        ],
        out_specs=pl.BlockSpec((T, W), lambda b, h, qi: (b * NT + qi, h)),
        out_shape=jax.ShapeDtypeStruct((B * S, H * dh), BF16),
        scratch_shapes=[pltpu.VMEM((G, T, LANES), F32), pltpu.VMEM((G, T, LANES), F32),
                        pltpu.VMEM((G, T, dh), F32), pltpu.VMEM((G, T, LANES), F32)],
        compiler_params=pltpu.CompilerParams(
            dimension_semantics=("parallel", "parallel", "arbitrary"),
            vmem_limit_bytes=_vmem_limit(blocks, (4 * G + 8) * _nbytes((T, T), F32))),
        name="fox_attention",
    )(proj_b, proj_b, proj_b, c_row)


def _merge_body(og_ref, of_ref, wg_ref, wf_ref, ma_ref, mb_ref, o_ref):
    og = og_ref[...]
    of = of_ref[...]
    for c in range(0, o_ref.shape[1], MXU_COLS):
        cols = slice(c, c + MXU_COLS)
        yg = jnp.dot(og, wg_ref[:, cols], preferred_element_type=F32)
        yf = jnp.dot(of, wf_ref[:, cols], preferred_element_type=F32)
        o_ref[:, cols] = (jax.nn.sigmoid(ma_ref[:, cols]) * yg
                          + jax.nn.sigmoid(mb_ref[:, cols]) * yf).astype(o_ref.dtype)


def _merge(og, of, w_bg, w_bf, proj_f, cols, tm=1024, tn=1024):
    M, KG = og.shape
    KF = of.shape[1]
    N = w_bg.shape[1]
    tm, tn = min(tm, M), min(tn, N)
    ao, bo = cols["ma"] // tn, cols["mb"] // tn
    assert cols["ma"] % tn == 0 and cols["mb"] % tn == 0
    blocks = (_nbytes((tm, KG + KF), BF16) + _nbytes((KG + KF, tn), BF16)
              + 2 * _nbytes((tm, tn), F32) + _nbytes((tm, tn), BF16))
    return pl.pallas_call(
        _merge_body,
        grid=(N // tn, M // tm),
        in_specs=[
            pl.BlockSpec((tm, KG), lambda j, i: (i, 0)),
            pl.BlockSpec((tm, KF), lambda j, i: (i, 0)),
            pl.BlockSpec((KG, tn), lambda j, i: (0, j)),
            pl.BlockSpec((KF, tn), lambda j, i: (0, j)),
            pl.BlockSpec((tm, tn), lambda j, i: (i, ao + j)),
            pl.BlockSpec((tm, tn), lambda j, i: (i, bo + j)),
        ],
        out_specs=pl.BlockSpec((tm, tn), lambda j, i: (i, j)),
        out_shape=jax.ShapeDtypeStruct((M, N), BF16),
        compiler_params=pltpu.CompilerParams(
            dimension_semantics=("parallel", "parallel"),
            vmem_limit_bytes=_vmem_limit(blocks, 3 * _nbytes((tm, tn), F32))),
        name="branch_merge",
    )(og, of, w_bg, w_bf, proj_f, proj_f)


def _proj_ln_body(a_ref, w_ref, x_ref, g_ref, b_ref, o_ref, ob_ref, *, alpha, parts):
    pr = o_ref.shape[0] // parts
    ys = [jnp.dot(a_ref[r * pr:(r + 1) * pr, :], w_ref[...], preferred_element_type=F32)
          for r in range(parts)]
    for r in range(parts):
        rows = slice(r * pr, (r + 1) * pr)
        out = _layer_norm(alpha * x_ref[rows, :] + ys[r], g_ref[...], b_ref[...])
        o_ref[rows, :] = out
        ob_ref[rows, :] = out.astype(BF16)


def _proj_ln(a, w, resid, g, b, alpha, tm=512):
    M, K = a.shape
    N = w.shape[1]
    tm = min(tm, M)
    blocks = (_nbytes((tm, K), BF16) + _nbytes((K, N), BF16) + 2 * _nbytes((tm, N), F32)
              + _nbytes((tm, N), BF16))
    return pl.pallas_call(
        functools.partial(_proj_ln_body, alpha=alpha, parts=2),
        grid=(M // tm,),
        in_specs=[
            pl.BlockSpec((tm, K), lambda i: (i, 0)),
            pl.BlockSpec((K, N), lambda i: (0, 0)),
            pl.BlockSpec((tm, N), lambda i: (i, 0)),
            pl.BlockSpec((1, N), lambda i: (0, 0)),
            pl.BlockSpec((1, N), lambda i: (0, 0)),
        ],
        out_specs=[pl.BlockSpec((tm, N), lambda i: (i, 0)),
                   pl.BlockSpec((tm, N), lambda i: (i, 0))],
        out_shape=[jax.ShapeDtypeStruct((M, N), F32), jax.ShapeDtypeStruct((M, N), BF16)],
        compiler_params=pltpu.CompilerParams(
            dimension_semantics=("parallel",),
            vmem_limit_bytes=_vmem_limit(blocks, 3 * _nbytes((tm, N), F32))),
        name="out_proj_layernorm",
    )(a, w, resid, g, b)


def _ffn_up_body(x_ref, wg_ref, wu_ref, cw_ref, cb_ref, h_ref, wgb_ref, wub_ref, gbuf_ref,
                 *, tm, tiles_per_seq, pr, pc):
    i = pl.program_id(1)
    HALO = SUBLANES
    tn = h_ref.shape[1]

    @pl.when(i == 0)
    def _():
        wgb_ref[...] = wg_ref[...].astype(BF16)
        wub_ref[...] = wu_ref[...].astype(BF16)

    @pl.when(i % tiles_per_seq == 0)
    def _():
        gbuf_ref[0:HALO, :] = jnp.zeros((HALO, tn), F32)

    cw = cw_ref[...]
    cb = cb_ref[...]
    parts = [(r, c) for c in range(tn // pc) for r in range(tm // pr)]

    def matmuls(r, c):
        x = x_ref[r * pr:(r + 1) * pr, :]
        cols = slice(c * pc, (c + 1) * pc)
        g = jnp.dot(x, wgb_ref[:, cols], preferred_element_type=F32)
        u = jnp.dot(x, wub_ref[:, cols], preferred_element_type=F32)
        gbuf_ref[HALO + r * pr:HALO + (r + 1) * pr, cols] = g
        return g, u

    def finish(r, c, g, u):
        cols = slice(c * pc, (c + 1) * pc)
        y = cb[:, cols] + cw[CONV_W - 1:CONV_W, cols] * g
        for j in range(CONV_W - 1):
            off = HALO - (CONV_W - 1) + j + r * pr
            y = y + cw[j:j + 1, cols] * gbuf_ref[off:off + pr, cols]
        h_ref[r * pr:(r + 1) * pr, cols] = (jax.nn.gelu(y, approximate=True) * u).astype(h_ref.dtype)

    pending = None
    for r, c in parts:
        gu = matmuls(r, c)
        if pending is not None:
            finish(*pending)
        pending = (r, c) + gu
    finish(*pending)
    gbuf_ref[0:HALO, :] = gbuf_ref[tm:tm + HALO, :]


def _ffn_up(xb, w_gate, w_up, conv_w, conv_b, S, tm=2048, tn=512, pr=512, pc=256):
    M, K = xb.shape
    N = w_gate.shape[1]
    tm, tn = min(tm, S), min(tn, N)
    pr, pc = min(pr, tm), min(pc, tn)
    assert S % tm == 0 and N % tn == 0 and tm % pr == 0 and tn % pc == 0
    blocks = (_nbytes((tm, K), BF16) + 2 * _nbytes((K, tn), F32) + _nbytes((tm, tn), BF16)
              + _nbytes((8, tn), F32) * 2)
    scratch = 2 * _nbytes((K, tn), BF16) + _nbytes((tm + SUBLANES, tn), F32)
    return pl.pallas_call(
        functools.partial(_ffn_up_body, tm=tm, tiles_per_seq=S // tm, pr=pr, pc=pc),
        grid=(N // tn, M // tm),
        in_specs=[
            pl.BlockSpec((tm, K), lambda j, i: (i, 0)),
            pl.BlockSpec((K, tn), lambda j, i: (0, j)),
            pl.BlockSpec((K, tn), lambda j, i: (0, j)),
            pl.BlockSpec((CONV_W, tn), lambda j, i: (0, j)),
            pl.BlockSpec((1, tn), lambda j, i: (0, j)),
        ],
        out_specs=pl.BlockSpec((tm, tn), lambda j, i: (i, j)),
        out_shape=jax.ShapeDtypeStruct((M, N), BF16),
        scratch_shapes=[pltpu.VMEM((K, tn), BF16), pltpu.VMEM((K, tn), BF16),
                        pltpu.VMEM((tm + SUBLANES, tn), F32)],
        compiler_params=pltpu.CompilerParams(
            dimension_semantics=("parallel", "arbitrary"),
            vmem_limit_bytes=_vmem_limit(blocks, scratch + 8 * _nbytes((pr, pc), F32))),
        name="ffn_gate_up",
    )(xb, w_gate, w_up, conv_w, conv_b)


def _ffn_down_body(h_ref, w_ref, x_ref, g_ref, b_ref, o_ref, *, alpha):
    kk = pl.program_id(1)

    @pl.when(kk == 0)
    def _():
        o_ref[...] = alpha * x_ref[...]

    o_ref[...] += jnp.dot(h_ref[...], w_ref[...], preferred_element_type=F32)

    @pl.when(kk == pl.num_programs(1) - 1)
    def _():
        o_ref[...] = _layer_norm(o_ref[...], g_ref[...], b_ref[...])


def _ffn_down(h, w_down, resid, g, b, alpha, tm=512, tk=2816):
    M, K = h.shape
    N = w_down.shape[1]
    tm, tk = min(tm, M), min(tk, K)
    assert M % tm == 0 and K % tk == 0
    blocks = (_nbytes((tm, tk), BF16) + _nbytes((tk, N), BF16) + 2 * _nbytes((tm, N), F32))
    return pl.pallas_call(
        functools.partial(_ffn_down_body, alpha=alpha),
        grid=(M // tm, K // tk),
        in_specs=[
            pl.BlockSpec((tm, tk), lambda i, kk: (i, kk)),
            pl.BlockSpec((tk, N), lambda i, kk: (kk, 0)),
            pl.BlockSpec((tm, N), lambda i, kk: (i, 0)),
            pl.BlockSpec((1, N), lambda i, kk: (0, 0)),
            pl.BlockSpec((1, N), lambda i, kk: (0, 0)),
        ],
        out_specs=pl.BlockSpec((tm, N), lambda i, kk: (i, 0)),
        out_shape=jax.ShapeDtypeStruct((M, N), F32),
        compiler_params=pltpu.CompilerParams(
            dimension_semantics=("parallel", "arbitrary"),
            vmem_limit_bytes=_vmem_limit(blocks, _nbytes((tm, N), F32))),
        name="ffn_down_layernorm",
    )(h, w_down, resid, g, b)


def _ple_body(x_ref, wg_ref, p_ref, wp_ref, o_ref, xb_ref, pb_ref):
    j = pl.program_id(1)
    tn = o_ref.shape[1]

    @pl.when(j == 0)
    def _():
        xb_ref[...] = x_ref[...].astype(BF16)
        pb_ref[...] = p_ref[...].astype(BF16)

    xb = xb_ref[...]
    pb = pb_ref[...]
    for c in range(0, tn, MXU_COLS):
        cols = slice(c, c + MXU_COLS)
        gate = jax.nn.sigmoid(jnp.dot(xb, wg_ref[:, cols], preferred_element_type=F32))
        emb = jnp.dot(pb, wp_ref[:, cols], preferred_element_type=F32)
        resid = x_ref[:, pl.ds(pl.multiple_of(j * tn + c, MXU_COLS), MXU_COLS)]
        o_ref[:, cols] = resid + gate * emb


def _ple(x, p, w_gate, w_proj, tm=1024, tn=1024):
    M, K = x.shape
    N = w_gate.shape[1]
    P = p.shape[1]
    tm, tn = min(tm, M), min(tn, N)
    assert K == N and M % tm == 0 and N % tn == 0
    blocks = (_nbytes((tm, K), F32) + _nbytes((K, tn), BF16) + _nbytes((tm, P), F32)
              + _nbytes((P, tn), BF16) + _nbytes((tm, tn), F32))
    scratch = _nbytes((tm, K), BF16) + _nbytes((tm, P), BF16)
    return pl.pallas_call(
        _ple_body,
        grid=(M // tm, N // tn),
        in_specs=[
            pl.BlockSpec((tm, K), lambda i, j: (i, 0)),
            pl.BlockSpec((K, tn), lambda i, j: (0, j)),
            pl.BlockSpec((tm, P), lambda i, j: (i, 0)),
            pl.BlockSpec((P, tn), lambda i, j: (0, j)),
        ],
        out_specs=pl.BlockSpec((tm, tn), lambda i, j: (i, j)),
        out_shape=jax.ShapeDtypeStruct((M, N), F32),
        scratch_shapes=[pltpu.VMEM((tm, K), BF16), pltpu.VMEM((tm, P), BF16)],
        compiler_params=pltpu.CompilerParams(
            dimension_semantics=("parallel", "arbitrary"),
            vmem_limit_bytes=_vmem_limit(blocks, scratch + 3 * _nbytes((tm, tn), F32))),
        name="ple_gate",
    )(x, w_gate, p, w_proj)


def _cast_small_body(x_ref, w_ref, xb_ref, o_ref):
    xb = x_ref[...].astype(BF16)
    xb_ref[...] = xb
    o_ref[...] = lax.dot_general(xb, w_ref[...].astype(BF16), (((1,), (1,)), ((), ())),
                                 preferred_element_type=F32)


def _cast_and_small_proj(x, w_small, tm=1024):
    M, K = x.shape
    ns = w_small.shape[0]
    tm = min(tm, M)
    assert M % tm == 0
    blocks = _nbytes((tm, K), F32) + _nbytes((ns, K), F32) + _nbytes((tm, K), BF16) \
        + _nbytes((tm, ns), F32)
    return pl.pallas_call(
        _cast_small_body,
        grid=(M // tm,),
        in_specs=[pl.BlockSpec((tm, K), lambda i: (i, 0)),
                  pl.BlockSpec((ns, K), lambda i: (0, 0))],
        out_specs=[pl.BlockSpec((tm, K), lambda i: (i, 0)),
                   pl.BlockSpec((tm, ns), lambda i: (i, 0))],
        out_shape=[jax.ShapeDtypeStruct((M, K), BF16), jax.ShapeDtypeStruct((M, ns), F32)],
        compiler_params=pltpu.CompilerParams(
            dimension_semantics=("parallel",),
            vmem_limit_bytes=_vmem_limit(blocks, _nbytes((tm, K), BF16))),
        name="in_proj_small",
    )(x, w_small)


def _in_proj(x, w_in, layer, rank, later_weights):
    M, D = x.shape
    gla_qk = D // 2
    fox_w = FOX_HEADS * LANES
    names = ("gq", "gk", "gv", "gr", "glr", "fq", "fk", "fv", "ff", "ma", "mb")
    widths = (gla_qk, gla_qk, D, D, rank, fox_w, fox_w, fox_w, FOX_HEADS, D, D)
    assert sum(widths) == w_in.shape[2]
    width, offset, off = {}, {}, 0
    for n, wd in zip(names, widths):
        width[n], offset[n] = wd, off
        off += wd
    w_t = jnp.swapaxes(w_in, 1, 2)

    out_scale = {"gq": (gla_qk // GLA_HEADS) ** -0.5, "fq": LANES ** -0.5 * LOG2E}
    T = IN_PROJ_TILE

    def project(order, out_dtype, name, side_casts=()):
        src_rows, scales, cols, o = [], [], {}, 0
        for n in order:
            assert width[n] % T == 0
            cols[n] = o
            o += width[n]
            src_rows += [offset[n] + t for t in range(0, width[n], T)]
            scales.append(jnp.full((1, width[n]), out_scale.get(n, 1.0), F32))
        out, casts = _matmul_nt(xb, w_t, layer, src_rows, jnp.concatenate(scales, axis=1),
                                out_dtype, 1024, T, name, side_casts)
        return (out, cols), casts

    small, cols_s = [], {}
    for t, n in enumerate(("glr", "ff")):
        wseg = w_t[layer, offset[n]:offset[n] + width[n], :]
        small.append(jnp.pad(wseg, ((0, LANES - width[n]), (0, 0))))
        cols_s[n] = t * LANES
    w_small = jnp.concatenate(small, axis=0)
    xb, proj_s = _cast_and_small_proj(x, w_small)

    proj_b, _ = project(("gv", "fq", "fk", "fv"), BF16, "in_proj_bf16")
    proj_f, weights_bf16 = project(("gq", "gk", "gr"), F32, "in_proj_f32", later_weights)
    assert offset["mb"] == offset["ma"] + width["ma"]
    gates = (xb, w_t, (offset["ma"], width["ma"] + width["mb"]), {"ma": 0, "mb": width["ma"]})
    return proj_b, proj_f, (proj_s, cols_s), gates, weights_bf16


def kernel(x, p, w_in, w_gla_lr, b_gla_lr, gla_norm_g, b_forget, w_branch_gla, w_branch_fox,
           w_out, ln1_g, ln1_b, w_gate, w_up, conv_w, conv_b, w_down, ln2_g, ln2_b,
           w_ple_gate, w_ple_proj):
    B, S, D = x.shape
    depth = w_in.shape[0]
    alpha = (2 * depth) ** 0.25
    M = B * S
    xf = x.reshape(M, D)
    for i in range(depth):
        rank = w_gla_lr.shape[1]
        (proj_b, cols_b), (proj_f, cols_f), (proj_s, cols_s), gates, (w_d,) = _in_proj(
            xf, w_in, i, rank, (w_down[i],))
        xb, w_t, gate_rows, cols_m = gates

        ff = proj_s[:, cols_s["ff"]:cols_s["ff"] + FOX_HEADS]
        c = _forget_cumsum(ff, b_forget[i], B, S)
        of = _fox(proj_b, c, B, S, cols_b, LANES)

        w_lr = jnp.pad(w_gla_lr[i], ((0, LANES - rank), (0, 0))).astype(BF16)
        og, proj_m, (w_bg, w_bf, w_o, w_pg) = _gla(
            proj_f, proj_b, proj_s, w_lr, b_gla_lr[i].reshape(1, -1),
            gla_norm_g[i].reshape(1, -1), B, S, {**cols_f, **cols_b}, xb, w_t, i, gate_rows,
            (w_branch_gla[i], w_branch_fox[i], w_out[i], w_ple_gate[i]))

        merged = _merge(og, of, w_bg, w_bf, proj_m, cols_m)
        x1, x1b = _proj_ln(merged, w_o, xf, ln1_g[i].reshape(1, -1),
                           ln1_b[i].reshape(1, -1), alpha)
        h = _ffn_up(x1b, w_gate[i], w_up[i], conv_w[i], conv_b[i].reshape(1, -1), S)
        x2 = _ffn_down(h, w_d, x1, ln2_g[i].reshape(1, -1), ln2_b[i].reshape(1, -1), alpha)
        xf = _ple(x2, p[i].reshape(M, -1), w_pg, w_ple_proj[i].astype(BF16))
    return xf.reshape(B, S, D)
```
